```python
import jax, jax.numpy as jnp
from jax import lax
import numpy as np

D_MODEL = 2048
BATCH = 2
SEQ = 4096
DEPTH = 2

N_MIXERS = 2
N_A = (DEPTH + 1) // 2
N_B = DEPTH // 2
MEM_LEN = 256
MIX_W = D_MODEL
TOK_W = 3 * D_MODEL // 4
X_HEADS = 4
X_HEAD_DIM = (MIX_W - TOK_W) // X_HEADS
X_W = X_HEADS * X_HEAD_DIM
GLA_HEADS = 4
GLA_DV = TOK_W // GLA_HEADS
GLA_DK = GLA_DV // 2
GLA_KW = GLA_HEADS * GLA_DK
GLA_RANK = 16
GLA_TAU = 16.0
GLA_CHUNK = 64
GLA_IN = 2 * GLA_KW + 2 * TOK_W + GLA_RANK + X_W
GLA_SPLITS = (GLA_KW, 2 * GLA_KW, 2 * GLA_KW + TOK_W, 2 * GLA_KW + 2 * TOK_W, 2 * GLA_KW + 2 * TOK_W + GLA_RANK)
CONV_GROUPS = 4
CONV_W = TOK_W
CONV_K = 3
CONV_IN = 3 * CONV_W + X_W
CONV_SPLITS = (CONV_W, 2 * CONV_W, 3 * CONV_W)
D_FF = ((8 * D_MODEL // 3 + 255) // 256) * 256
EPS = 1e-6

kernel_name = "hybrid_gla_shortconv_macaron_memxattn"


def rmsnorm(x, g):
    x32 = x.astype(jnp.float32)
    y = x32 * lax.rsqrt(jnp.mean(x32 * x32, axis=-1, keepdims=True) + EPS)
    return (y * g.astype(jnp.float32)).astype(x.dtype)


def swiglu(x, w_in, w_out):
    gate, up = jnp.split(x @ w_in, 2, axis=-1)
    return (jax.nn.silu(gate) * up) @ w_out


def memory_attention(q, mem_n, w_mem_kv):
    b, s, _ = q.shape
    k, v = jnp.split(mem_n @ w_mem_kv, 2, axis=-1)
    q = q.reshape(b, s, X_HEADS, X_HEAD_DIM)
    k = k.reshape(b, -1, X_HEADS, X_HEAD_DIM)
    v = v.reshape(b, -1, X_HEADS, X_HEAD_DIM)
    scores = jnp.einsum('bshd,bmhd->bhsm', q, k).astype(jnp.float32) * (X_HEAD_DIM ** -0.5)
    p = jax.nn.softmax(scores, axis=-1).astype(v.dtype)
    o = jnp.einsum('bhsm,bmhd->bshd', p, v)
    return o.reshape(b, s, X_W)


def gla_attention(q, k, v, log_a):
    b, s, h, dk = q.shape
    dv = v.shape[-1]
    n = s // GLA_CHUNK

    def to_chunks(t):
        return t.reshape(b, n, GLA_CHUNK, h, t.shape[-1]).transpose(1, 0, 3, 2, 4)

    causal = jnp.tril(jnp.ones((GLA_CHUNK, GLA_CHUNK), dtype=bool))[:, :, None]

    def step(state, inp):
        qc, kc, vc, gc = inp
        cum = jnp.cumsum(gc, axis=2)
        o_inter = jnp.einsum('bhtk,bhkv->bhtv', qc * jnp.exp(cum), state)
        diff = cum[:, :, :, None, :] - cum[:, :, None, :, :]
        decay = jnp.exp(jnp.where(causal, diff, -jnp.inf))
        attn = jnp.einsum('bhtsk,bhsk->bhts', qc[:, :, :, None, :] * decay, kc)
        o_intra = jnp.einsum('bhts,bhsv->bhtv', attn, vc)
        last = cum[:, :, -1:, :]
        state = (jnp.exp(last[:, :, 0, :])[..., None] * state
                 + jnp.einsum('bhsk,bhsv->bhkv', kc * jnp.exp(last - cum), vc))
        return state, o_inter + o_intra

    s0 = jnp.zeros((b, h, dk, dv), jnp.float32)
    _, o = lax.scan(step, s0, (to_chunks(q), to_chunks(k), to_chunks(v), to_chunks(log_a)))
    return o.transpose(1, 0, 3, 2, 4).reshape(b, s, h, dv)


def gla_mixer(h, w_in, gate_w2, gate_b, head_g):
    b, s, _ = h.shape
    q, k, v, r, g_low, xq = jnp.split(h @ w_in, GLA_SPLITS, axis=-1)
    z = (g_low @ gate_w2 + gate_b).astype(jnp.float32)
    log_a = (jax.nn.log_sigmoid(z) / GLA_TAU).reshape(b, s, GLA_HEADS, GLA_DK)
    q = q.reshape(b, s, GLA_HEADS, GLA_DK).astype(jnp.float32) * (GLA_DK ** -0.5)
    k = k.reshape(b, s, GLA_HEADS, GLA_DK).astype(jnp.float32)
    v = v.reshape(b, s, GLA_HEADS, GLA_DV).astype(jnp.float32)
    o = gla_attention(q, k, v, log_a)
    o = rmsnorm(o, head_g).astype(h.dtype).reshape(b, s, TOK_W)
    return o * jax.nn.silu(r), xq


def conv_mixer(h, w_in, conv_w, conv_b):
    b_gate, c_gate, xt, xq = jnp.split(h @ w_in, CONV_SPLITS, axis=-1)
    u = c_gate * xt
    y = lax.conv_general_dilated(u, conv_w[:, None, :], window_strides=(1,),
                                 padding=[(CONV_K - 1, 0)],
                                 dimension_numbers=('NWC', 'WIO', 'NWC'),
                                 feature_group_count=CONV_W) + conv_b
    return b_gate * y, xq


def setup_inputs(seed: int = 0) -> dict:
    key = jax.random.key(seed)
    ks = jax.random.split(key, 20)

    def nrm(k, shape, scale):
        return jax.random.normal(k, shape, jnp.float32) * scale

    def gain(k, shape):
        return 1.0 + 0.02 * jax.random.normal(k, shape, jnp.float32)

    return {
        "x": nrm(ks[0], (BATCH, SEQ, D_MODEL), 1.0),
        "mem": nrm(ks[1], (BATCH, MEM_LEN, D_MODEL), 1.0),
        "ffn_pre_g": gain(ks[2], (DEPTH, 2, D_MODEL)),
        "ffn_w_in": nrm(ks[3], (DEPTH, 2, D_MODEL, 2 * D_FF), D_MODEL ** -0.5),
        "ffn_w_out": nrm(ks[4], (DEPTH, 2, D_FF, D_MODEL), D_FF ** -0.5),
        "ffn_post_g": gain(ks[5], (DEPTH, 2, D_MODEL)),
        "mix_pre_g": gain(ks[6], (DEPTH, D_MODEL)),
        "mix_post_g": gain(ks[7], (DEPTH, D_MODEL)),
        "mem_g": gain(ks[8], (DEPTH, D_MODEL)),
        "w_mem_kv": nrm(ks[9], (DEPTH, D_MODEL, 2 * X_W), D_MODEL ** -0.5),
        "w_mix_out": nrm(ks[10], (DEPTH, MIX_W, D_MODEL), MIX_W ** -0.5),
        "gla_w_in": nrm(ks[11], (N_A, D_MODEL, GLA_IN), D_MODEL ** -0.5),
        "gla_gate_w2": nrm(ks[12], (N_A, GLA_RANK, GLA_KW), GLA_RANK ** -0.5),
        "gla_gate_b": nrm(ks[13], (N_A, GLA_KW), 0.1),
        "gla_head_g": gain(ks[14], (N_A, GLA_DV)),
        "conv_w_in": nrm(ks[15], (N_B, D_MODEL, CONV_IN), D_MODEL ** -0.5),
        "conv_w": nrm(ks[16], (N_B, CONV_K, CONV_W), CONV_K ** -0.5),
        "conv_b": nrm(ks[17], (N_B, CONV_W), 0.02),
    }


def reference(x, mem, ffn_pre_g, ffn_w_in, ffn_w_out, ffn_post_g, mix_pre_g, mix_post_g,
              mem_g, w_mem_kv, w_mix_out, gla_w_in, gla_gate_w2, gla_gate_b, gla_head_g,
              conv_w_in, conv_w, conv_b):
    for i in range(DEPTH):
        x = x + 0.5 * rmsnorm(swiglu(rmsnorm(x, ffn_pre_g[i, 0]), ffn_w_in[i, 0], ffn_w_out[i, 0]),
                              ffn_post_g[i, 0])
        hn = rmsnorm(x, mix_pre_g[i])
        j = i // N_MIXERS
        if i % N_MIXERS == 0:
            tok, xq = gla_mixer(hn, gla_w_in[j], gla_gate_w2[j], gla_gate_b[j], gla_head_g[j])
        else:
            tok, xq = conv_mixer(hn, conv_w_in[j], conv_w[j], conv_b[j])
        xo = memory_attention(xq, rmsnorm(mem, mem_g[i]), w_mem_kv[i])
        mixed = jnp.concatenate([tok, xo], axis=-1) @ w_mix_out[i]
        x = x + rmsnorm(mixed, mix_post_g[i])
        x = x + 0.5 * rmsnorm(swiglu(rmsnorm(x, ffn_pre_g[i, 1]), ffn_w_in[i, 1], ffn_w_out[i, 1]),
                              ffn_post_g[i, 1])
    return x
```

```python
import functools

import jax
import jax.numpy as jnp
from jax import lax
from jax.experimental import pallas as pl
from jax.experimental.pallas import tpu as pltpu

F32 = jnp.float32
BF16 = jnp.bfloat16

D_MODEL = 2048
DEPTH = 2
MEM_LEN = 256
TOK_W = 1536
X_HEADS = 4
X_HEAD_DIM = 128
X_W = 512
GLA_HEADS = 4
GLA_DV = 384
GLA_DK = 192
GLA_KW = 768
GLA_RANK = 16
GLA_TAU = 16.0
CONV_W = 1536
CONV_K = 3
D_FF = 5632
EPS = 1e-6

LANE = 128
BF16_ROWS = 16
MXU_DIM = 256
VMEM_LIMIT = 56 * 1024 * 1024

GLA_DKP = MXU_DIM
GLA_RANKP = LANE
GLA_CHUNK = 64
GLA_NP = 2 * TOK_W + 2 * GLA_HEADS * GLA_DKP + X_W + GLA_RANKP
CONV_NP = 3 * CONV_W + X_W


def _rmsnorm(x, g):
    ms = jnp.mean(x * x, axis=-1, keepdims=True)
    return x * lax.rsqrt(ms + EPS) * g


def _dot(a, b):
    return jnp.dot(a, b, preferred_element_type=F32)


def _dot_nt(a, b):
    return lax.dot_general(a, b, (((1,), (1,)), ((), ())), preferred_element_type=F32)


def _dot_tn(a, b):
    return lax.dot_general(a, b, (((0,), (0,)), ((), ())), preferred_element_type=F32)


def _ffn_kernel(x_ref, pre_g_ref, wg_ref, wu_ref, wo_ref, post_g_ref, o_ref, h_ref, acc_ref):
    j = pl.program_id(1)

    @pl.when(j == 0)
    def _():
        h_ref[...] = _rmsnorm(x_ref[...], pre_g_ref[...]).astype(BF16)
        acc_ref[...] = jnp.zeros_like(acc_ref)

    h = h_ref[...]
    gate = _dot(h, wg_ref[...])
    up = _dot(h, wu_ref[...])
    act = (gate * jax.nn.sigmoid(gate) * up).astype(BF16)
    acc_ref[...] += _dot(act, wo_ref[...])

    @pl.when(j == pl.num_programs(1) - 1)
    def _():
        o_ref[...] = x_ref[...] + 0.5 * _rmsnorm(acc_ref[...], post_g_ref[...])


def _ffn(x, pre_g, w_in, w_out, post_g, *, tm=512, tf=512):
    t, d = x.shape
    nf = D_FF // tf
    return pl.pallas_call(
        _ffn_kernel,
        grid=(t // tm, nf),
        in_specs=[
            pl.BlockSpec((tm, d), lambda i, j: (i, 0)),
            pl.BlockSpec((1, d), lambda i, j: (0, 0)),
            pl.BlockSpec((d, tf), lambda i, j: (0, j)),
            pl.BlockSpec((d, tf), lambda i, j: (0, j + nf)),
            pl.BlockSpec((tf, d), lambda i, j: (j, 0)),
            pl.BlockSpec((1, d), lambda i, j: (0, 0)),
        ],
        out_specs=pl.BlockSpec((tm, d), lambda i, j: (i, 0)),
        out_shape=jax.ShapeDtypeStruct((t, d), F32),
        scratch_shapes=[pltpu.VMEM((tm, d), BF16), pltpu.VMEM((tm, d), F32)],
        compiler_params=pltpu.CompilerParams(
            dimension_semantics=("parallel", "arbitrary"), vmem_limit_bytes=VMEM_LIMIT),
        name="ffn",
    )(x, pre_g, w_in, w_in, w_out, post_g)


def _norm_matmul_kernel(x_ref, g_ref, w_ref, o_ref, h_ref):
    @pl.when(pl.program_id(1) == 0)
    def _():
        h_ref[...] = _rmsnorm(x_ref[...], g_ref[...]).astype(BF16)

    o_ref[...] = _dot(h_ref[...], w_ref[...]).astype(o_ref.dtype)


def _norm_matmul(x, g, w, *, tm, tn, name):
    t, d = x.shape
    n = w.shape[1]
    return pl.pallas_call(
        _norm_matmul_kernel,
        grid=(t // tm, n // tn),
        in_specs=[
            pl.BlockSpec((tm, d), lambda i, j: (i, 0)),
            pl.BlockSpec((1, d), lambda i, j: (0, 0)),
            pl.BlockSpec((d, tn), lambda i, j: (0, j)),
        ],
        out_specs=pl.BlockSpec((tm, tn), lambda i, j: (i, j)),
        out_shape=jax.ShapeDtypeStruct((t, n), BF16),
        scratch_shapes=[pltpu.VMEM((tm, d), BF16)],
        compiler_params=pltpu.CompilerParams(
            dimension_semantics=("parallel", "arbitrary"), vmem_limit_bytes=VMEM_LIMIT),
        name=name,
    )(x, g, w)


def _gla_kernel(v_ref, r_ref, q_ref, k_ref, gl_ref, w2_ref, b2_ref, hg_ref, o_ref, st_ref, *, tc):
    c_len = GLA_CHUNK

    @pl.when(pl.program_id(2) == 0)
    def _():
        st_ref[...] = jnp.zeros_like(st_ref)

    row = lax.broadcasted_iota(jnp.int32, (c_len, GLA_DKP), 0)
    ti = lax.broadcasted_iota(jnp.int32, (c_len, c_len), 0)
    si = lax.broadcasted_iota(jnp.int32, (c_len, c_len), 1)
    txs = ti ^ si
    w2 = w2_ref[0]
    b2 = b2_ref[0]

    for c in range(tc // c_len):
        rows = pl.ds(c * c_len, c_len)
        q = q_ref[rows, :].astype(F32) * (GLA_DK ** -0.5)
        k = k_ref[rows, :].astype(F32)
        v = v_ref[rows, :]
        z = _dot(gl_ref[rows, :], w2) + b2
        log_a = (jnp.minimum(z, 0.0) - jnp.log1p(jnp.exp(-jnp.abs(z)))) * (1.0 / GLA_TAU)

        cum = log_a
        d = 1
        while d < c_len:
            cum = cum + jnp.where(row >= d, pltpu.roll(cum, d, axis=0), 0.0)
            d *= 2

        attn = jnp.where(ti == si, _dot_nt(q.astype(BF16), k.astype(BF16)), 0.0)
        g_h = cum
        h = 1
        while h < c_len:
            if h > 1:
                g_h = jnp.where((row & (h // 2)) != 0, pltpu.roll(g_h, h // 2, axis=0), g_h)
            e_q = jnp.minimum(cum - g_h, 0.0)
            e_k = jnp.minimum(pltpu.roll(g_h, c_len - h, axis=0) - cum, 0.0)
            q_h = (q * jnp.exp(e_q)).astype(BF16)
            k_h = (k * jnp.exp(e_k)).astype(BF16)
            sel = (txs >= h) & (txs < 2 * h) & (ti > si)
            attn = jnp.where(sel, _dot_nt(q_h, k_h), attn)
            h *= 2

        st = st_ref[...]
        q_in = (q * jnp.exp(cum)).astype(BF16)
        o = _dot_nt(q_in, st.astype(BF16)) + _dot(attn.astype(BF16), v)
        last = cum[c_len - 1:c_len, :]
        k_out = (k * jnp.exp(last - cum)).astype(BF16)
        st_ref[...] = st * jnp.exp(last) + _dot_tn(v, k_out)

        o = _rmsnorm(o, hg_ref[...])
        r = r_ref[rows, :].astype(F32)
        o_ref[rows, :] = (o * (r * jax.nn.sigmoid(r))).astype(BF16)


def _gla(p, w2, b2, head_g, *, batch, seq, tc=256):
    t = p.shape[0]
    nc = seq // tc
    rmap = lambda b, h, i: b * nc + i
    n_v = TOK_W // GLA_DV
    n_q = 2 * TOK_W // GLA_DKP
    n_g = (GLA_NP - GLA_RANKP) // GLA_RANKP
    return pl.pallas_call(
        functools.partial(_gla_kernel, tc=tc),
        grid=(batch, GLA_HEADS, nc),
        in_specs=[
            pl.BlockSpec((tc, GLA_DV), lambda b, h, i: (rmap(b, h, i), h)),
            pl.BlockSpec((tc, GLA_DV), lambda b, h, i: (rmap(b, h, i), n_v + h)),
            pl.BlockSpec((tc, GLA_DKP), lambda b, h, i: (rmap(b, h, i), n_q + h)),
            pl.BlockSpec((tc, GLA_DKP), lambda b, h, i: (rmap(b, h, i), n_q + GLA_HEADS + h)),
            pl.BlockSpec((tc, GLA_RANKP), lambda b, h, i: (rmap(b, h, i), n_g)),
            pl.BlockSpec((1, GLA_RANKP, GLA_DKP), lambda b, h, i: (h, 0, 0)),
            pl.BlockSpec((1, 1, GLA_DKP), lambda b, h, i: (h, 0, 0)),
            pl.BlockSpec((1, GLA_DV), lambda b, h, i: (0, 0)),
        ],
        out_specs=pl.BlockSpec((tc, GLA_DV), lambda b, h, i: (rmap(b, h, i), h)),
        out_shape=jax.ShapeDtypeStruct((t, TOK_W), BF16),
        scratch_shapes=[pltpu.VMEM((GLA_DV, GLA_DKP), F32)],
        compiler_params=pltpu.CompilerParams(
            dimension_semantics=("parallel", "parallel", "arbitrary"), vmem_limit_bytes=VMEM_LIMIT),
        name="gla",
    )(p, p, p, p, p, w2, b2, head_g)


def _conv_kernel(b_ref, c_ref, xt_ref, cp_ref, xp_ref, w_ref, bias_ref, o_ref, *, tiles_per_seq):
    i = pl.program_id(0)
    u = c_ref[...].astype(F32) * xt_ref[...].astype(F32)
    keep = jnp.where(i % tiles_per_seq != 0, 1.0, 0.0)
    u_prev = cp_ref[...].astype(F32) * xp_ref[...].astype(F32) * keep
    row = lax.broadcasted_iota(jnp.int32, u.shape, 0)
    u1 = jnp.where(row == 0, u_prev[BF16_ROWS - 1:BF16_ROWS, :], pltpu.roll(u, 1, axis=0))
    u2 = pltpu.roll(u, 2, axis=0)
    u2 = jnp.where(row == 0, u_prev[BF16_ROWS - 2:BF16_ROWS - 1, :], u2)
    u2 = jnp.where(row == 1, u_prev[BF16_ROWS - 1:BF16_ROWS, :], u2)
    w = w_ref[...]
    y = u2 * w[0:1, :] + u1 * w[1:2, :] + u * w[2:3, :] + bias_ref[...]
    o_ref[...] = (b_ref[...].astype(F32) * y).astype(BF16)


def _conv(p, conv_w, conv_b, *, seq, ts=512):
    t = p.shape[0]
    nb = CONV_W // CONV_W
    prev = lambda i: jnp.maximum(i * (ts // BF16_ROWS) - 1, 0)
    return pl.pallas_call(
        functools.partial(_conv_kernel, tiles_per_seq=seq // ts),
        grid=(t // ts,),
        in_specs=[
            pl.BlockSpec((ts, CONV_W), lambda i: (i, 0)),
            pl.BlockSpec((ts, CONV_W), lambda i: (i, nb)),
            pl.BlockSpec((ts, CONV_W), lambda i: (i, 2 * nb)),
            pl.BlockSpec((BF16_ROWS, CONV_W), lambda i: (prev(i), nb)),
            pl.BlockSpec((BF16_ROWS, CONV_W), lambda i: (prev(i), 2 * nb)),
            pl.BlockSpec((CONV_K, CONV_W), lambda i: (0, 0)),
            pl.BlockSpec((1, CONV_W), lambda i: (0, 0)),
        ],
        out_specs=pl.BlockSpec((ts, CONV_W), lambda i: (i, 0)),
        out_shape=jax.ShapeDtypeStruct((t, CONV_W), BF16),
        compiler_params=pltpu.CompilerParams(
            dimension_semantics=("parallel",), vmem_limit_bytes=VMEM_LIMIT),
        name="conv",
    )(p, p, p, p, p, conv_w, conv_b)


def _mixout_kernel(tok_ref, xq_ref, kv_ref, wm_ref, x_ref, g_ref, o_ref, cat_ref):
    cat_ref[:, :TOK_W] = tok_ref[...]
    for h in range(X_HEADS):
        lo = h * X_HEAD_DIM
        qh = xq_ref[:, lo:lo + X_HEAD_DIM]
        kh = kv_ref[:, lo:lo + X_HEAD_DIM]
        vh = kv_ref[:, X_W + lo:X_W + lo + X_HEAD_DIM]
        s = _dot_nt(qh, kh) * (X_HEAD_DIM ** -0.5)
        e = jnp.exp(s - jnp.max(s, axis=-1, keepdims=True))
        pr = (e / jnp.sum(e, axis=-1, keepdims=True)).astype(BF16)
        cat_ref[:, TOK_W + lo:TOK_W + lo + X_HEAD_DIM] = _dot(pr, vh).astype(BF16)
    mixed = _dot(cat_ref[...], wm_ref[...])
    o_ref[...] = x_ref[...] + _rmsnorm(mixed, g_ref[...])


def _mixout(tok, p, xq_block, kv, wm, x, g, *, seq, tm=512):
    t, d = x.shape
    per_seq = seq // tm
    return pl.pallas_call(
        _mixout_kernel,
        grid=(t // tm,),
        in_specs=[
            pl.BlockSpec((tm, TOK_W), lambda i: (i, 0)),
            pl.BlockSpec((tm, X_W), lambda i: (i, xq_block)),
            pl.BlockSpec((MEM_LEN, 2 * X_W), lambda i: (i // per_seq, 0)),
            pl.BlockSpec((d, d), lambda i: (0, 0)),
            pl.BlockSpec((tm, d), lambda i: (i, 0)),
            pl.BlockSpec((1, d), lambda i: (0, 0)),
        ],
        out_specs=pl.BlockSpec((tm, d), lambda i: (i, 0)),
        out_shape=jax.ShapeDtypeStruct((t, d), F32),
        scratch_shapes=[pltpu.VMEM((tm, d), BF16)],
        compiler_params=pltpu.CompilerParams(
            dimension_semantics=("parallel",), vmem_limit_bytes=VMEM_LIMIT),
        name="mixout",
    )(tok, p, kv, wm, x, g)


def _pad_heads(w, width, padded):
    lead = w.shape[:-1]
    w = w.reshape(lead + (GLA_HEADS, width))
    w = jnp.pad(w, [(0, 0)] * len(lead) + [(0, 0), (0, padded - width)])
    return w.reshape(lead + (GLA_HEADS * padded,))


def _gla_weights(w_in, gate_w2, gate_b):
    wq = _pad_heads(w_in[:, :GLA_KW], GLA_DK, GLA_DKP)
    wk = _pad_heads(w_in[:, GLA_KW:2 * GLA_KW], GLA_DK, GLA_DKP)
    wv = w_in[:, 2 * GLA_KW:2 * GLA_KW + TOK_W]
    wr = w_in[:, 2 * GLA_KW + TOK_W:2 * GLA_KW + 2 * TOK_W]
    o_g = 2 * GLA_KW + 2 * TOK_W
    wg = jnp.pad(w_in[:, o_g:o_g + GLA_RANK], ((0, 0), (0, GLA_RANKP - GLA_RANK)))
    wxq = w_in[:, o_g + GLA_RANK:]
    w_all = jnp.concatenate([wv, wr, wq, wk, wxq, wg], axis=1).astype(BF16)
    w2 = jnp.pad(_pad_heads(gate_w2, GLA_DK, GLA_DKP), ((0, GLA_RANKP - GLA_RANK), (0, 0)))
    w2 = w2.reshape(GLA_RANKP, GLA_HEADS, GLA_DKP).transpose(1, 0, 2).astype(BF16)
    b2 = _pad_heads(gate_b, GLA_DK, GLA_DKP).reshape(GLA_HEADS, 1, GLA_DKP)
    return w_all, w2, b2


def kernel(x, mem, ffn_pre_g, ffn_w_in, ffn_w_out, ffn_post_g, mix_pre_g, mix_post_g, mem_g,
           w_mem_kv, w_mix_out, gla_w_in, gla_gate_w2, gla_gate_b, gla_head_g,
           conv_w_in, conv_w, conv_b):
    batch, seq, d = x.shape
    x = x.reshape(batch * seq, d)
    mem2 = mem.reshape(batch * MEM_LEN, d)

    def ffn(x, i, k):
        return _ffn(x, ffn_pre_g[i, k][None], ffn_w_in[i, k].astype(BF16),
                    ffn_w_out[i, k].astype(BF16), ffn_post_g[i, k][None])

    for i in range(DEPTH):
        x = ffn(x, i, 0)
        j = i // 2
        kv = _norm_matmul(mem2, mem_g[i][None], w_mem_kv[i].astype(BF16),
                          tm=batch * MEM_LEN, tn=2 * X_W, name="mem_kv")
        if i % 2 == 0:
            w_all, w2, b2 = _gla_weights(gla_w_in[j], gla_gate_w2[j], gla_gate_b[j])
            p = _norm_matmul(x, mix_pre_g[i][None], w_all, tm=1024, tn=GLA_NP // 5, name="gla_proj")
            tok = _gla(p, w2, b2, gla_head_g[j][None], batch=batch, seq=seq)
            xq_block = (2 * TOK_W + 2 * GLA_HEADS * GLA_DKP) // X_W
        else:
            p = _norm_matmul(x, mix_pre_g[i][None], conv_w_in[j].astype(BF16),
                             tm=1024, tn=CONV_NP // 5, name="conv_proj")
            tok = _conv(p, conv_w[j], conv_b[j][None], seq=seq)
            xq_block = 3 * CONV_W // X_W
        x = _mixout(tok, p, xq_block, kv, w_mix_out[i].astype(BF16), x, mix_post_g[i][None], seq=seq)
        x = ffn(x, i, 1)
    return x.reshape(batch, seq, d)
```

```python
import functools

import jax
import jax.numpy as jnp
from jax import lax
from jax.experimental import pallas as pl
from jax.experimental.pallas import tpu as pltpu

F32 = jnp.float32
BF16 = jnp.bfloat16

D_MODEL = 2048
DEPTH = 2
MEM_LEN = 256
TOK_W = 1536
X_HEADS = 4
X_HEAD_DIM = 128
X_W = 512
GLA_HEADS = 4
GLA_DV = 384
GLA_DK = 192
GLA_KW = 768
GLA_RANK = 16
GLA_TAU = 16.0
CONV_W = 1536
CONV_K = 3
D_FF = 5632
EPS = 1e-6
LOG2E = 1.4426950408889634

LANE = 128
SUBLANE = 8
BF16_ROWS = 16
MXU_DIM = 256
VMEM_LIMIT = 56 * 1024 * 1024
VMEM_LIMIT_FFN = 60 * 1024 * 1024

GLA_DKP = MXU_DIM
GLA_RANKP = LANE
GLA_CHUNK = 64
GLA_NP = 2 * TOK_W + 2 * GLA_HEADS * GLA_DKP + X_W + GLA_RANKP
CONV_NP = 3 * CONV_W + X_W


def _rmsnorm(x, g):
    ms = jnp.mean(x * x, axis=-1, keepdims=True)
    return x * lax.rsqrt(ms + EPS) * g


def _dot(a, b):
    return jnp.dot(a, b, preferred_element_type=F32)


def _dot_nt(a, b):
    return lax.dot_general(a, b, (((1,), (1,)), ((), ())), preferred_element_type=F32)


def _dot_tn(a, b):
    return lax.dot_general(a, b, (((0,), (0,)), ((), ())), preferred_element_type=F32)


def _stacked_spec(lead, block, index_map):
    nlead = len(lead)
    return pl.BlockSpec((None,) * nlead + block, lambda *g: tuple(lead) + tuple(index_map(*g)))


def _ffn_kernel(x_ref, pre_g_ref, wg_ref, wu_ref, wo_ref, post_g_ref, o_ref, h_ref):
    j = pl.program_id(1)

    @pl.when(j == 0)
    def _():
        h_ref[...] = _rmsnorm(x_ref[...], pre_g_ref[...]).astype(BF16)
        o_ref[...] = jnp.zeros_like(o_ref)

    h = h_ref[...]
    gate = _dot(h, wg_ref[...].astype(BF16))
    up = _dot(h, wu_ref[...].astype(BF16))
    act = (gate * jax.nn.sigmoid(gate) * up).astype(BF16)
    o_ref[...] += _dot(act, wo_ref[...].astype(BF16))

    @pl.when(j == pl.num_programs(1) - 1)
    def _():
        o_ref[...] = x_ref[...] + 0.5 * _rmsnorm(o_ref[...], post_g_ref[...])


def _ffn(x, pre_g, w_in, w_out, post_g, lead, *, tm=1024, tf=256):
    t, d = x.shape
    nf = D_FF // tf
    return pl.pallas_call(
        _ffn_kernel,
        grid=(t // tm, nf),
        in_specs=[
            pl.BlockSpec((tm, d), lambda i, j: (i, 0)),
            _stacked_spec(lead, (1, d), lambda i, j: (0, 0)),
            _stacked_spec(lead, (d, tf), lambda i, j: (0, j)),
            _stacked_spec(lead, (d, tf), lambda i, j: (0, j + nf)),
            _stacked_spec(lead, (tf, d), lambda i, j: (j, 0)),
            _stacked_spec(lead, (1, d), lambda i, j: (0, 0)),
        ],
        out_specs=pl.BlockSpec((tm, d), lambda i, j: (i, 0)),
        out_shape=jax.ShapeDtypeStruct((t, d), F32),
        scratch_shapes=[pltpu.VMEM((tm, d), BF16)],
        compiler_params=pltpu.CompilerParams(
            dimension_semantics=("parallel", "arbitrary"), vmem_limit_bytes=VMEM_LIMIT_FFN),
        name="ffn",
    )(x, pre_g, w_in, w_in, w_out, post_g)


def _norm_matmul_kernel(x_ref, g_ref, w_ref, o_ref, h_ref):
    @pl.when(pl.program_id(1) == 0)
    def _():
        h_ref[...] = _rmsnorm(x_ref[...], g_ref[...]).astype(BF16)

    o_ref[...] = _dot(h_ref[...], w_ref[...].astype(BF16)).astype(o_ref.dtype)


def _norm_matmul(x, g, g_lead, w, w_lead, *, tm, tn, name):
    t, d = x.shape
    n = w.shape[-1]
    return pl.pallas_call(
        _norm_matmul_kernel,
        grid=(t // tm, n // tn),
        in_specs=[
            pl.BlockSpec((tm, d), lambda i, j: (i, 0)),
            _stacked_spec(g_lead, (1, d), lambda i, j: (0, 0)),
            _stacked_spec(w_lead, (d, tn), lambda i, j: (0, j)),
        ],
        out_specs=pl.BlockSpec((tm, tn), lambda i, j: (i, j)),
        out_shape=jax.ShapeDtypeStruct((t, n), BF16),
        scratch_shapes=[pltpu.VMEM((tm, d), BF16)],
        compiler_params=pltpu.CompilerParams(
            dimension_semantics=("parallel", "arbitrary"), vmem_limit_bytes=VMEM_LIMIT),
        name=name,
    )(x, g, w)


def _shift_groups(a, n):
    return jnp.concatenate([a[n:], a[:n]], axis=0)


def _block_first_group(a, n):
    return jnp.concatenate([a[g - g % n:g - g % n + 1] for g in range(a.shape[0])], axis=0)


def _gla_kernel(v_ref, r_ref, q_ref, k_ref, gl_ref, w2_ref, b2_ref, hg_ref, o_ref, st_ref):
    c_len = GLA_CHUNK
    ng = c_len // SUBLANE
    nb = v_ref.shape[0]

    @pl.when(pl.program_id(0) == 0)
    def _():
        st_ref[...] = jnp.zeros_like(st_ref)

    r3 = lax.broadcasted_iota(jnp.int32, (ng, SUBLANE, GLA_DKP), 1)
    ti = lax.broadcasted_iota(jnp.int32, (c_len, c_len), 0)
    si = lax.broadcasted_iota(jnp.int32, (c_len, c_len), 1)
    txs = jnp.where(ti > si, ti ^ si, 0)
    to3 = lambda a: a.reshape(ng, SUBLANE, GLA_DKP)
    to2 = lambda a: a.reshape(c_len, GLA_DKP)

    streams = [divmod(s, GLA_HEADS) for s in range(nb * GLA_HEADS)]
    kcols = [pl.ds(hd * GLA_DKP, GLA_DKP) for _, hd in streams]
    vcols = [pl.ds(hd * GLA_DV, GLA_DV) for _, hd in streams]
    each = lambda f, *lists: [f(*args) for args in zip(*lists)]

    z = [_dot(gl_ref[bi], w2_ref[hd]) + b2_ref[hd] for bi, hd in streams]
    q = [to3(q_ref[bi, :, c].astype(F32) * (GLA_DK ** -0.5)) for (bi, _), c in zip(streams, kcols)]
    k = [to3(k_ref[bi, :, c].astype(F32)) for (bi, _), c in zip(streams, kcols)]

    def log2_decay(zs):
        soft = jnp.log2(1.0 + jnp.exp2(jnp.abs(zs) * -LOG2E))
        return to3((jnp.minimum(zs, 0.0) * LOG2E - soft) * (1.0 / GLA_TAU))

    cum = each(log2_decay, z)
    d = 1
    while d < SUBLANE:
        cum = each(lambda a: a + jnp.where(r3 >= d, pltpu.roll(a, d, axis=1), 0.0), cum)
        d *= 2
    tot = each(lambda a: a[:, SUBLANE - 1:SUBLANE, :], cum)
    offs = [[jnp.zeros_like(t[0:1])] for t in tot]
    for g in range(1, ng):
        for o_s, t in zip(offs, tot):
            o_s.append(o_s[-1] + t[g - 1:g])
    cum = each(lambda a, o_s: a + jnp.concatenate(o_s, axis=0), cum, offs)

    nt = lambda a, b: _dot_nt(to2(a).astype(BF16), to2(b).astype(BF16))
    attn = each(lambda a, b: jnp.where(ti == si, nt(a, b), 0.0), q, k)
    g_h = cum
    h = 1
    while h < c_len:
        if h < SUBLANE:
            if h > 1:
                g_h = each(lambda a: jnp.where((r3 & (h // 2)) != 0,
                                               pltpu.roll(a, h // 2, axis=1), a), g_h)
            g_next = each(lambda a: pltpu.roll(a, SUBLANE - h, axis=1), g_h)
        else:
            n = h // SUBLANE
            g_h = each(lambda a: jnp.broadcast_to(a[:, 0:1, :], a.shape), cum)
            if n > 1:
                g_h = each(lambda a: _block_first_group(a, n), g_h)
            g_next = each(lambda a: _shift_groups(a, n), g_h)
        q_h = q if h == 1 else each(lambda a, c, g: a * jnp.exp2(c - g), q, cum, g_h)
        k_h = each(lambda a, c, g: a * jnp.exp2(g - c), k, cum, g_next)
        a_h = each(nt, q_h, k_h)
        attn = each(lambda new, old: jnp.where((txs >= h) & (txs < 2 * h), new, old), a_h, attn)
        h *= 2

    last = each(lambda a: a[ng - 1, SUBLANE - 1:SUBLANE, :], cum)
    q_in = each(lambda a, c: to2(a * jnp.exp2(c)).astype(BF16), q, cum)
    k_out = each(lambda a, c, l: to2(a * jnp.exp2(l - c)).astype(BF16), k, cum, last)
    v = [v_ref[bi, :, c] for (bi, _), c in zip(streams, vcols)]
    st = [st_ref[s] for s in range(len(streams))]
    o = each(lambda qi, s_, a, v_: _dot_nt(qi, s_.astype(BF16)) + _dot(a.astype(BF16), v_),
             q_in, st, attn, v)
    for s, (s_, l, v_, ko) in enumerate(zip(st, last, v, k_out)):
        st_ref[s] = s_ * jnp.exp2(l) + _dot_tn(v_, ko)

    o = each(lambda a: _rmsnorm(a, hg_ref[...]), o)
    for (bi, _), c, a in zip(streams, vcols, o):
        r = r_ref[bi, :, c].astype(F32)
        o_ref[bi, :, c] = (a * (r * jax.nn.sigmoid(r))).astype(BF16)


def _gla(p, w2, b2, head_g, *, batch, seq):
    t = p.shape[0]
    p = p.reshape(batch, seq, GLA_NP)
    c_len = GLA_CHUNK
    qk_w = GLA_HEADS * GLA_DKP
    n_q = 2 * TOK_W // qk_w
    n_g = (GLA_NP - GLA_RANKP) // GLA_RANKP
    out = pl.pallas_call(
        _gla_kernel,
        grid=(seq // c_len,),
        in_specs=[
            pl.BlockSpec((batch, c_len, TOK_W), lambda i: (0, i, 0)),
            pl.BlockSpec((batch, c_len, TOK_W), lambda i: (0, i, 1)),
            pl.BlockSpec((batch, c_len, qk_w), lambda i: (0, i, n_q)),
            pl.BlockSpec((batch, c_len, qk_w), lambda i: (0, i, n_q + 1)),
            pl.BlockSpec((batch, c_len, GLA_RANKP), lambda i: (0, i, n_g)),
            pl.BlockSpec((GLA_HEADS, GLA_RANKP, GLA_DKP), lambda i: (0, 0, 0)),
            pl.BlockSpec((GLA_HEADS, 1, GLA_DKP), lambda i: (0, 0, 0)),
            pl.BlockSpec((1, GLA_DV), lambda i: (0, 0)),
        ],
        out_specs=pl.BlockSpec((batch, c_len, TOK_W), lambda i: (0, i, 0)),
        out_shape=jax.ShapeDtypeStruct((batch, seq, TOK_W), BF16),
        scratch_shapes=[pltpu.VMEM((batch * GLA_HEADS, GLA_DV, GLA_DKP), F32)],
        compiler_params=pltpu.CompilerParams(
            dimension_semantics=("arbitrary",), vmem_limit_bytes=VMEM_LIMIT),
        name="gla",
    )(p, p, p, p, p, w2, b2, head_g)
    return out.reshape(t, TOK_W)


def _conv_kernel(b_ref, c_ref, xt_ref, cp_ref, xp_ref, w_ref, bias_ref, o_ref, *, tiles_per_seq):
    i = pl.program_id(0)
    u = c_ref[...].astype(F32) * xt_ref[...].astype(F32)
    keep = jnp.where(i % tiles_per_seq != 0, 1.0, 0.0)
    u_prev = cp_ref[...].astype(F32) * xp_ref[...].astype(F32) * keep
    row = lax.broadcasted_iota(jnp.int32, u.shape, 0)
    u1 = jnp.where(row == 0, u_prev[BF16_ROWS - 1:BF16_ROWS, :], pltpu.roll(u, 1, axis=0))
    u2 = pltpu.roll(u, 2, axis=0)
    u2 = jnp.where(row == 0, u_prev[BF16_ROWS - 2:BF16_ROWS - 1, :], u2)
    u2 = jnp.where(row == 1, u_prev[BF16_ROWS - 1:BF16_ROWS, :], u2)
    w = w_ref[...]
    y = u2 * w[0:1, :] + u1 * w[1:2, :] + u * w[2:3, :] + bias_ref[...]
    o_ref[...] = (b_ref[...].astype(F32) * y).astype(BF16)


def _conv(p, conv_w, conv_b, lead, *, seq, ts=512):
    t = p.shape[0]
    prev = lambda i: jnp.maximum(i * (ts // BF16_ROWS) - 1, 0)
    return pl.pallas_call(
        functools.partial(_conv_kernel, tiles_per_seq=seq // ts),
        grid=(t // ts,),
        in_specs=[
            pl.BlockSpec((ts, CONV_W), lambda i: (i, 0)),
            pl.BlockSpec((ts, CONV_W), lambda i: (i, 1)),
            pl.BlockSpec((ts, CONV_W), lambda i: (i, 2)),
            pl.BlockSpec((BF16_ROWS, CONV_W), lambda i: (prev(i), 1)),
            pl.BlockSpec((BF16_ROWS, CONV_W), lambda i: (prev(i), 2)),
            _stacked_spec(lead, (CONV_K, CONV_W), lambda i: (0, 0)),
            _stacked_spec(lead, (1, CONV_W), lambda i: (0, 0)),
        ],
        out_specs=pl.BlockSpec((ts, CONV_W), lambda i: (i, 0)),
        out_shape=jax.ShapeDtypeStruct((t, CONV_W), BF16),
        compiler_params=pltpu.CompilerParams(
            dimension_semantics=("parallel",), vmem_limit_bytes=VMEM_LIMIT),
        name="conv",
    )(p, p, p, p, p, conv_w, conv_b)


def _mixout_kernel(tok_ref, xq_ref, kv_ref, wm_ref, x_ref, g_ref, o_ref, cat_ref):
    cat_ref[:, :TOK_W] = tok_ref[...]
    for h in range(X_HEADS):
        lo = h * X_HEAD_DIM
        qh = xq_ref[:, lo:lo + X_HEAD_DIM]
        kh = kv_ref[:, lo:lo + X_HEAD_DIM]
        vh = kv_ref[:, X_W + lo:X_W + lo + X_HEAD_DIM]
        s = _dot_nt(qh, kh) * (X_HEAD_DIM ** -0.5)
        e = jnp.exp(s - jnp.max(s, axis=-1, keepdims=True))
        pr = (e / jnp.sum(e, axis=-1, keepdims=True)).astype(BF16)
        cat_ref[:, TOK_W + lo:TOK_W + lo + X_HEAD_DIM] = _dot(pr, vh).astype(BF16)
    mixed = _dot(cat_ref[...], wm_ref[...])
    o_ref[...] = x_ref[...] + _rmsnorm(mixed, g_ref[...])


def _mixout(tok, p, xq_block, kv, wm, x, g, lead, *, seq, tm=512):
    t, d = x.shape
    per_seq = seq // tm
    return pl.pallas_call(
        _mixout_kernel,
        grid=(t // tm,),
        in_specs=[
            pl.BlockSpec((tm, TOK_W), lambda i: (i, 0)),
            pl.BlockSpec((tm, X_W), lambda i: (i, xq_block)),
            pl.BlockSpec((MEM_LEN, 2 * X_W), lambda i: (i // per_seq, 0)),
            _stacked_spec(lead, (d, d), lambda i: (0, 0)),
            pl.BlockSpec((tm, d), lambda i: (i, 0)),
            _stacked_spec(lead, (1, d), lambda i: (0, 0)),
        ],
        out_specs=pl.BlockSpec((tm, d), lambda i: (i, 0)),
        out_shape=jax.ShapeDtypeStruct((t, d), F32),
        scratch_shapes=[pltpu.VMEM((tm, d), BF16)],
        compiler_params=pltpu.CompilerParams(
            dimension_semantics=("parallel",), vmem_limit_bytes=VMEM_LIMIT),
        name="mixout",
    )(tok, p, kv, wm, x, g)


def _pad_heads(w, width, padded):
    lead = w.shape[:-1]
    w = w.reshape(lead + (GLA_HEADS, width))
    w = jnp.pad(w, [(0, 0)] * len(lead) + [(0, 0), (0, padded - width)])
    return w.reshape(lead + (GLA_HEADS * padded,))


def _gla_weights(w_in, gate_w2, gate_b):
    wq = _pad_heads(w_in[:, :GLA_KW], GLA_DK, GLA_DKP)
    wk = _pad_heads(w_in[:, GLA_KW:2 * GLA_KW], GLA_DK, GLA_DKP)
    wv = w_in[:, 2 * GLA_KW:2 * GLA_KW + TOK_W]
    wr = w_in[:, 2 * GLA_KW + TOK_W:2 * GLA_KW + 2 * TOK_W]
    o_g = 2 * GLA_KW + 2 * TOK_W
    wg = jnp.pad(w_in[:, o_g:o_g + GLA_RANK], ((0, 0), (0, GLA_RANKP - GLA_RANK)))
    wxq = w_in[:, o_g + GLA_RANK:]
    w_all = jnp.concatenate([wv, wr, wq, wk, wxq, wg], axis=1)
    w2 = jnp.pad(_pad_heads(gate_w2, GLA_DK, GLA_DKP), ((0, GLA_RANKP - GLA_RANK), (0, 0)))
    w2 = w2.reshape(GLA_RANKP, GLA_HEADS, GLA_DKP).transpose(1, 0, 2).astype(BF16)
    b2 = _pad_heads(gate_b, GLA_DK, GLA_DKP).reshape(GLA_HEADS, 1, GLA_DKP)
    return w_all, w2, b2


def kernel(x, mem, ffn_pre_g, ffn_w_in, ffn_w_out, ffn_post_g, mix_pre_g, mix_post_g, mem_g,
           w_mem_kv, w_mix_out, gla_w_in, gla_gate_w2, gla_gate_b, gla_head_g,
           conv_w_in, conv_w, conv_b):
    batch, seq, d = x.shape
    x = x.reshape(batch * seq, d)
    mem2 = mem.reshape(batch * MEM_LEN, d)
    ffn_pre_g = ffn_pre_g[:, :, None, :]
    ffn_post_g = ffn_post_g[:, :, None, :]
    mix_pre_g = mix_pre_g[:, None, :]
    mix_post_g = mix_post_g[:, None, :]
    mem_g = mem_g[:, None, :]
    conv_b = conv_b[:, None, :]
    w_mix_out = w_mix_out.astype(BF16)
    w_mem_kv = w_mem_kv.astype(BF16)

    def ffn(x, i, k):
        return _ffn(x, ffn_pre_g, ffn_w_in, ffn_w_out, ffn_post_g, (i, k))

    for i in range(DEPTH):
        x = ffn(x, i, 0)
        j = i // 2
        kv = _norm_matmul(mem2, mem_g, (i,), w_mem_kv, (i,),
                          tm=batch * MEM_LEN, tn=2 * X_W, name="mem_kv")
        if i % 2 == 0:
            w_all, w2, b2 = _gla_weights(gla_w_in[j], gla_gate_w2[j], gla_gate_b[j])
            p = _norm_matmul(x, mix_pre_g, (i,), w_all, (), tm=1024, tn=GLA_NP // 5, name="gla_proj")
            tok = _gla(p, w2, b2, gla_head_g[j][None], batch=batch, seq=seq)
            xq_block = (2 * TOK_W + 2 * GLA_HEADS * GLA_DKP) // X_W
        else:
            p = _norm_matmul(x, mix_pre_g, (i,), conv_w_in, (j,), tm=1024, tn=CONV_NP // 5,
                             name="conv_proj")
            tok = _conv(p, conv_w, conv_b, (j,), seq=seq)
            xq_block = 3 * CONV_W // X_W
        x = _mixout(tok, p, xq_block, kv, w_mix_out, x, mix_post_g, (i,), seq=seq)
        x = ffn(x, i, 1)
    return x.reshape(batch, seq, d)
```

```python
import functools

import jax
import jax.numpy as jnp
from jax import lax
from jax.experimental import pallas as pl
from jax.experimental.pallas import tpu as pltpu

F32 = jnp.float32
BF16 = jnp.bfloat16

D_MODEL = 2048
DEPTH = 2
MEM_LEN = 256
TOK_W = 1536
X_HEADS = 4
X_HEAD_DIM = 128
X_W = 512
GLA_HEADS = 4
GLA_DV = 384
GLA_DK = 192
GLA_KW = 768
GLA_RANK = 16
GLA_TAU = 16.0
CONV_W = 1536
CONV_K = 3
D_FF = 5632
EPS = 1e-6
LOG2E = 1.4426950408889634

LANE = 128
SUBLANE = 8
BF16_ROWS = 16
MXU_DIM = 256
VMEM_LIMIT = 56 * 1024 * 1024
VMEM_LIMIT_FFN = 60 * 1024 * 1024

GLA_RANKP = LANE
GLA_DKP = MXU_DIM
GLA_CHUNK = 64
GLA_MAIN_N = 2 * GLA_KW + 2 * TOK_W
GLA_SIDE_N = X_W + GLA_RANKP
CONV_NP = 3 * CONV_W + X_W


def _rmsnorm(x, g):
    ms = jnp.mean(x * x, axis=-1, keepdims=True)
    return x * lax.rsqrt(ms + EPS) * g


def _dot(a, b):
    return jnp.dot(a, b, preferred_element_type=F32)


def _dot_nt(a, b):
    return lax.dot_general(a, b, (((1,), (1,)), ((), ())), preferred_element_type=F32)


def _dot_tn(a, b):
    return lax.dot_general(a, b, (((0,), (0,)), ((), ())), preferred_element_type=F32)


def _stacked_spec(lead, block, index_map):
    nlead = len(lead)
    return pl.BlockSpec((None,) * nlead + block, lambda *g: tuple(lead) + tuple(index_map(*g)))


def _ffn_kernel(x_ref, pre_g_ref, wg_ref, wu_ref, wo_ref, post_g_ref, o_ref, h_ref):
    j = pl.program_id(1)

    @pl.when(j == 0)
    def _():
        h_ref[...] = _rmsnorm(x_ref[...], pre_g_ref[...]).astype(BF16)
        o_ref[...] = jnp.zeros_like(o_ref)

    h = h_ref[...]
    gate = _dot(h, wg_ref[...].astype(BF16))
    up = _dot(h, wu_ref[...].astype(BF16))
    act = (gate * jax.nn.sigmoid(gate) * up).astype(BF16)
    o_ref[...] += _dot(act, wo_ref[...].astype(BF16))

    @pl.when(j == pl.num_programs(1) - 1)
    def _():
        o_ref[...] = x_ref[...] + 0.5 * _rmsnorm(o_ref[...], post_g_ref[...])


def _ffn(x, pre_g, w_in, w_out, post_g, lead, *, tm=1024, tf=256):
    t, d = x.shape
    nf = D_FF // tf
    return pl.pallas_call(
        _ffn_kernel,
        grid=(t // tm, nf),
        in_specs=[
            pl.BlockSpec((tm, d), lambda i, j: (i, 0)),
            _stacked_spec(lead, (1, d), lambda i, j: (0, 0)),
            _stacked_spec(lead, (d, tf), lambda i, j: (0, j)),
            _stacked_spec(lead, (d, tf), lambda i, j: (0, j + nf)),
            _stacked_spec(lead, (tf, d), lambda i, j: (j, 0)),
            _stacked_spec(lead, (1, d), lambda i, j: (0, 0)),
        ],
        out_specs=pl.BlockSpec((tm, d), lambda i, j: (i, 0)),
        out_shape=jax.ShapeDtypeStruct((t, d), F32),
        scratch_shapes=[pltpu.VMEM((tm, d), BF16)],
        compiler_params=pltpu.CompilerParams(
            dimension_semantics=("parallel", "arbitrary"), vmem_limit_bytes=VMEM_LIMIT_FFN),
        name="ffn",
    )(x, pre_g, w_in, w_in, w_out, post_g)


def _norm_matmul_kernel(x_ref, g_ref, w_ref, o_ref, h_ref):
    @pl.when(pl.program_id(1) == 0)
    def _():
        h_ref[...] = _rmsnorm(x_ref[...], g_ref[...]).astype(BF16)

    o_ref[...] = _dot(h_ref[...], w_ref[...].astype(BF16)).astype(o_ref.dtype)


def _norm_matmul(x, g, g_lead, w, w_lead, *, tm, tn, name):
    t, d = x.shape
    n = w.shape[-1]
    return pl.pallas_call(
        _norm_matmul_kernel,
        grid=(t // tm, n // tn),
        in_specs=[
            pl.BlockSpec((tm, d), lambda i, j: (i, 0)),
            _stacked_spec(g_lead, (1, d), lambda i, j: (0, 0)),
            _stacked_spec(w_lead, (d, tn), lambda i, j: (0, j)),
        ],
        out_specs=pl.BlockSpec((tm, tn), lambda i, j: (i, j)),
        out_shape=jax.ShapeDtypeStruct((t, n), BF16),
        scratch_shapes=[pltpu.VMEM((tm, d), BF16)],
        compiler_params=pltpu.CompilerParams(
            dimension_semantics=("parallel", "arbitrary"), vmem_limit_bytes=VMEM_LIMIT),
        name=name,
    )(x, g, w)


def _gla_proj_kernel(x_ref, g_ref, wm_ref, ws_ref, om_ref, os_ref, h_ref, *, n_main):
    j = pl.program_id(1)

    @pl.when(j == 0)
    def _():
        h_ref[...] = _rmsnorm(x_ref[...], g_ref[...]).astype(BF16)

    @pl.when(j < n_main)
    def _():
        om_ref[...] = _dot(h_ref[...], wm_ref[...].astype(BF16)).astype(BF16)

    @pl.when(j == n_main)
    def _():
        os_ref[...] = _dot(h_ref[...], ws_ref[...].astype(BF16)).astype(BF16)


def _gla_proj(x, g, g_lead, w, w_lead, w_side, *, tm=1024, tn=768):
    t, d = x.shape
    n_main = GLA_MAIN_N // tn
    return pl.pallas_call(
        functools.partial(_gla_proj_kernel, n_main=n_main),
        grid=(t // tm, n_main + 1),
        in_specs=[
            pl.BlockSpec((tm, d), lambda i, j: (i, 0)),
            _stacked_spec(g_lead, (1, d), lambda i, j: (0, 0)),
            _stacked_spec(w_lead, (d, tn), lambda i, j: (0, jnp.minimum(j, n_main - 1))),
            pl.BlockSpec((d, GLA_SIDE_N), lambda i, j: (0, 0)),
        ],
        out_specs=[
            pl.BlockSpec((tm, tn), lambda i, j: (i, jnp.minimum(j, n_main - 1))),
            pl.BlockSpec((tm, GLA_SIDE_N), lambda i, j: (i, 0)),
        ],
        out_shape=[jax.ShapeDtypeStruct((t, GLA_MAIN_N), BF16),
                   jax.ShapeDtypeStruct((t, GLA_SIDE_N), BF16)],
        scratch_shapes=[pltpu.VMEM((tm, d), BF16)],
        compiler_params=pltpu.CompilerParams(
            dimension_semantics=("parallel", "arbitrary"), vmem_limit_bytes=VMEM_LIMIT),
        name="gla_proj",
    )(x, g, w, w_side)


def _shift_groups(a, n):
    return jnp.concatenate([a[n:], a[:n]], axis=0)


def _block_first_group(a, n):
    return jnp.concatenate([a[g - g % n:g - g % n + 1] for g in range(a.shape[0])], axis=0)


def _gla_kernel(qkv_ref, r_ref, gl_ref, w2_ref, b2_ref, hg_ref, o_ref, st_ref):
    c_len = GLA_CHUNK
    ng = c_len // SUBLANE
    nb = qkv_ref.shape[0]

    @pl.when(pl.program_id(0) == 0)
    def _():
        st_ref[...] = jnp.zeros_like(st_ref)

    r3 = lax.broadcasted_iota(jnp.int32, (ng, SUBLANE, GLA_DKP), 1)
    ti = lax.broadcasted_iota(jnp.int32, (c_len, c_len), 0)
    si = lax.broadcasted_iota(jnp.int32, (c_len, c_len), 1)
    txs = jnp.where(ti > si, ti ^ si, 0)
    to3 = lambda a: a.reshape(ng, SUBLANE, GLA_DKP)
    to2 = lambda a: a.reshape(c_len, GLA_DKP)

    streams = [divmod(s, GLA_HEADS) for s in range(nb * GLA_HEADS)]
    vcols = [pl.ds(hd * GLA_DV, GLA_DV) for _, hd in streams]
    each = lambda f, *lists: [f(*args) for args in zip(*lists)]

    z = [_dot(gl_ref[bi], w2_ref[hd]) + b2_ref[hd] for bi, hd in streams]
    qkv = [qkv_ref[bi, :, :2 * GLA_KW + GLA_DKP].astype(F32) for bi in range(nb)]
    lane = lax.broadcasted_iota(jnp.int32, (c_len, GLA_DKP), 1)
    head = lambda a, lo: to3(jnp.where(lane < GLA_DK, a[:, lo:lo + GLA_DKP], 0.0))
    q = [head(qkv[bi], hd * GLA_DK) * (GLA_DK ** -0.5) for bi, hd in streams]
    k = [head(qkv[bi], GLA_KW + hd * GLA_DK) for bi, hd in streams]

    def log2_decay(zs):
        soft = jnp.log2(1.0 + jnp.exp2(jnp.abs(zs) * -LOG2E))
        return to3((jnp.minimum(zs, 0.0) * LOG2E - soft) * (1.0 / GLA_TAU))

    cum = each(log2_decay, z)
    d = 1
    while d < SUBLANE:
        cum = each(lambda a: a + jnp.where(r3 >= d, pltpu.roll(a, d, axis=1), 0.0), cum)
        d *= 2
    tot = each(lambda a: a[:, SUBLANE - 1:SUBLANE, :], cum)
    offs = [[jnp.zeros_like(t[0:1])] for t in tot]
    for g in range(1, ng):
        for o_s, t in zip(offs, tot):
            o_s.append(o_s[-1] + t[g - 1:g])
    cum = each(lambda a, o_s: a + jnp.concatenate(o_s, axis=0), cum, offs)

    nt = lambda a, b: _dot_nt(to2(a).astype(BF16), to2(b).astype(BF16))
    attn = each(lambda a, b: jnp.where(ti == si, nt(a, b), 0.0), q, k)
    g_h = cum
    h = 1
    while h < c_len:
        if h < SUBLANE:
            if h > 1:
                g_h = each(lambda a: jnp.where((r3 & (h // 2)) != 0,
                                               pltpu.roll(a, h // 2, axis=1), a), g_h)
            g_next = each(lambda a: pltpu.roll(a, SUBLANE - h, axis=1), g_h)
        else:
            n = h // SUBLANE
            g_h = each(lambda a: jnp.broadcast_to(a[:, 0:1, :], a.shape), cum)
            if n > 1:
                g_h = each(lambda a: _block_first_group(a, n), g_h)
            g_next = each(lambda a: _shift_groups(a, n), g_h)
        q_h = q if h == 1 else each(lambda a, c, g: a * jnp.exp2(c - g), q, cum, g_h)
        k_h = each(lambda a, c, g: a * jnp.exp2(g - c), k, cum, g_next)
        a_h = each(nt, q_h, k_h)
        attn = each(lambda new, old: jnp.where((txs >= h) & (txs < 2 * h), new, old), a_h, attn)
        h *= 2

    last = each(lambda a: a[ng - 1, SUBLANE - 1:SUBLANE, :], cum)
    q_in = each(lambda a, c: to2(a * jnp.exp2(c)).astype(BF16), q, cum)
    k_out = each(lambda a, c, l: to2(a * jnp.exp2(l - c)).astype(BF16), k, cum, last)
    v = [qkv_ref[bi, :, pl.ds(2 * GLA_KW + hd * GLA_DV, GLA_DV)] for bi, hd in streams]
    st = [st_ref[s] for s in range(len(streams))]
    o = each(lambda qi, s_, a, v_: _dot_nt(qi, s_.astype(BF16)) + _dot(a.astype(BF16), v_),
             q_in, st, attn, v)
    for s, (s_, l, v_, ko) in enumerate(zip(st, last, v, k_out)):
        st_ref[s] = s_ * jnp.exp2(l) + _dot_tn(v_, ko)

    o = each(lambda a: _rmsnorm(a, hg_ref[...]), o)
    for (bi, _), c, a in zip(streams, vcols, o):
        r = r_ref[bi, :, c].astype(F32)
        o_ref[bi, :, c] = (a * (r * jax.nn.sigmoid(r))).astype(BF16)


def _gla(p_main, p_side, w2, b2, head_g, *, batch, seq):
    t = p_main.shape[0]
    p_main = p_main.reshape(batch, seq, GLA_MAIN_N)
    p_side = p_side.reshape(batch, seq, GLA_SIDE_N)
    c_len = GLA_CHUNK
    assert 2 * GLA_KW == TOK_W
    out = pl.pallas_call(
        _gla_kernel,
        grid=(seq // c_len,),
        in_specs=[
            pl.BlockSpec((batch, c_len, 2 * TOK_W), lambda i: (0, i, 0)),
            pl.BlockSpec((batch, c_len, TOK_W), lambda i: (0, i, 2)),
            pl.BlockSpec((batch, c_len, GLA_RANKP), lambda i: (0, i, X_W // GLA_RANKP)),
            pl.BlockSpec((GLA_HEADS, GLA_RANKP, GLA_DKP), lambda i: (0, 0, 0)),
            pl.BlockSpec((GLA_HEADS, 1, GLA_DKP), lambda i: (0, 0, 0)),
            pl.BlockSpec((1, GLA_DV), lambda i: (0, 0)),
        ],
        out_specs=pl.BlockSpec((batch, c_len, TOK_W), lambda i: (0, i, 0)),
        out_shape=jax.ShapeDtypeStruct((batch, seq, TOK_W), BF16),
        scratch_shapes=[pltpu.VMEM((batch * GLA_HEADS, GLA_DV, GLA_DKP), F32)],
        compiler_params=pltpu.CompilerParams(
            dimension_semantics=("arbitrary",), vmem_limit_bytes=VMEM_LIMIT),
        name="gla",
    )(p_main, p_main, p_side, w2, b2, head_g)
    return out.reshape(t, TOK_W)


def _conv_kernel(b_ref, c_ref, xt_ref, cp_ref, xp_ref, w_ref, bias_ref, o_ref, *, tiles_per_seq):
    i = pl.program_id(0)
    u = c_ref[...].astype(F32) * xt_ref[...].astype(F32)
    keep = jnp.where(i % tiles_per_seq != 0, 1.0, 0.0)
    u_prev = cp_ref[...].astype(F32) * xp_ref[...].astype(F32) * keep
    row = lax.broadcasted_iota(jnp.int32, u.shape, 0)
    u1 = jnp.where(row == 0, u_prev[BF16_ROWS - 1:BF16_ROWS, :], pltpu.roll(u, 1, axis=0))
    u2 = pltpu.roll(u, 2, axis=0)
    u2 = jnp.where(row == 0, u_prev[BF16_ROWS - 2:BF16_ROWS - 1, :], u2)
    u2 = jnp.where(row == 1, u_prev[BF16_ROWS - 1:BF16_ROWS, :], u2)
    w = w_ref[...]
    y = u2 * w[0:1, :] + u1 * w[1:2, :] + u * w[2:3, :] + bias_ref[...]
    o_ref[...] = (b_ref[...].astype(F32) * y).astype(BF16)


def _conv(p, conv_w, conv_b, lead, *, seq, ts=512):
    t = p.shape[0]
    prev = lambda i: jnp.maximum(i * (ts // BF16_ROWS) - 1, 0)
    return pl.pallas_call(
        functools.partial(_conv_kernel, tiles_per_seq=seq // ts),
        grid=(t // ts,),
        in_specs=[
            pl.BlockSpec((ts, CONV_W), lambda i: (i, 0)),
            pl.BlockSpec((ts, CONV_W), lambda i: (i, 1)),
            pl.BlockSpec((ts, CONV_W), lambda i: (i, 2)),
            pl.BlockSpec((BF16_ROWS, CONV_W), lambda i: (prev(i), 1)),
            pl.BlockSpec((BF16_ROWS, CONV_W), lambda i: (prev(i), 2)),
            _stacked_spec(lead, (CONV_K, CONV_W), lambda i: (0, 0)),
            _stacked_spec(lead, (1, CONV_W), lambda i: (0, 0)),
        ],
        out_specs=pl.BlockSpec((ts, CONV_W), lambda i: (i, 0)),
        out_shape=jax.ShapeDtypeStruct((t, CONV_W), BF16),
        compiler_params=pltpu.CompilerParams(
            dimension_semantics=("parallel",), vmem_limit_bytes=VMEM_LIMIT),
        name="conv",
    )(p, p, p, p, p, conv_w, conv_b)


def _mixout_kernel(tok_ref, xq_ref, kv_ref, wm_ref, x_ref, g_ref, o_ref):
    mixed = _dot(tok_ref[...], wm_ref[:TOK_W, :])
    xo = []
    for h in range(X_HEADS):
        lo = h * X_HEAD_DIM
        qh = xq_ref[:, lo:lo + X_HEAD_DIM]
        kh = kv_ref[:, lo:lo + X_HEAD_DIM]
        vh = kv_ref[:, X_W + lo:X_W + lo + X_HEAD_DIM]
        s = _dot_nt(qh, kh) * (X_HEAD_DIM ** -0.5)
        e = jnp.exp(s - jnp.max(s, axis=-1, keepdims=True))
        pr = (e / jnp.sum(e, axis=-1, keepdims=True)).astype(BF16)
        xo.append(_dot(pr, vh).astype(BF16))
    mixed = mixed + _dot(jnp.concatenate(xo, axis=1), wm_ref[TOK_W:, :])
    o_ref[...] = x_ref[...] + _rmsnorm(mixed, g_ref[...])


def _mixout(tok, p, xq_block, kv, wm, x, g, lead, *, seq, tm=512):
    t, d = x.shape
    per_seq = seq // tm
    return pl.pallas_call(
        _mixout_kernel,
        grid=(t // tm,),
        in_specs=[
            pl.BlockSpec((tm, TOK_W), lambda i: (i, 0)),
            pl.BlockSpec((tm, X_W), lambda i: (i, xq_block)),
            pl.BlockSpec((MEM_LEN, 2 * X_W), lambda i: (i // per_seq, 0)),
            _stacked_spec(lead, (d, d), lambda i: (0, 0)),
            pl.BlockSpec((tm, d), lambda i: (i, 0)),
            _stacked_spec(lead, (1, d), lambda i: (0, 0)),
        ],
        out_specs=pl.BlockSpec((tm, d), lambda i: (i, 0)),
        out_shape=jax.ShapeDtypeStruct((t, d), F32),
        compiler_params=pltpu.CompilerParams(
            dimension_semantics=("parallel",), vmem_limit_bytes=VMEM_LIMIT),
        name="mixout",
    )(tok, p, kv, wm, x, g)


def _gla_side_weights(w_in, gate_w2, gate_b):
    w_g = w_in[:, GLA_MAIN_N:GLA_MAIN_N + GLA_RANK]
    w_xq = w_in[:, GLA_MAIN_N + GLA_RANK:]
    pad = jnp.zeros((w_in.shape[0], GLA_RANKP - GLA_RANK), w_in.dtype)
    w_side = jnp.concatenate([w_xq, w_g, pad], axis=1)
    w2 = gate_w2.reshape(GLA_RANK, GLA_HEADS, GLA_DK).transpose(1, 0, 2)
    w2 = jnp.pad(w2, ((0, 0), (0, GLA_RANKP - GLA_RANK), (0, GLA_DKP - GLA_DK))).astype(BF16)
    b2 = jnp.pad(gate_b.reshape(GLA_HEADS, 1, GLA_DK), ((0, 0), (0, 0), (0, GLA_DKP - GLA_DK)))
    return w_side, w2, b2


def kernel(x, mem, ffn_pre_g, ffn_w_in, ffn_w_out, ffn_post_g, mix_pre_g, mix_post_g, mem_g,
           w_mem_kv, w_mix_out, gla_w_in, gla_gate_w2, gla_gate_b, gla_head_g,
           conv_w_in, conv_w, conv_b):
    batch, seq, d = x.shape
    x = x.reshape(batch * seq, d)
    mem2 = mem.reshape(batch * MEM_LEN, d)
    ffn_pre_g = ffn_pre_g[:, :, None, :]
    ffn_post_g = ffn_post_g[:, :, None, :]
    mix_pre_g = mix_pre_g[:, None, :]
    mix_post_g = mix_post_g[:, None, :]
    mem_g = mem_g[:, None, :]
    conv_b = conv_b[:, None, :]
    w_mix_out = w_mix_out.astype(BF16)

    def ffn(x, i, k):
        return _ffn(x, ffn_pre_g, ffn_w_in, ffn_w_out, ffn_post_g, (i, k))

    for i in range(DEPTH):
        x = ffn(x, i, 0)
        j = i // 2
        kv = _norm_matmul(mem2, mem_g, (i,), w_mem_kv, (i,),
                          tm=batch * MEM_LEN, tn=2 * X_W, name="mem_kv")
        if i % 2 == 0:
            w_side, w2, b2 = _gla_side_weights(gla_w_in[j], gla_gate_w2[j], gla_gate_b[j])
            p, p_xq = _gla_proj(x, mix_pre_g, (i,), gla_w_in, (j,), w_side)
            tok = _gla(p, p_xq, w2, b2, gla_head_g[j][None], batch=batch, seq=seq)
            xq_block = 0
        else:
            p = _norm_matmul(x, mix_pre_g, (i,), conv_w_in, (j,), tm=1024, tn=CONV_NP // 5,
                             name="conv_proj")
            tok = _conv(p, conv_w, conv_b, (j,), seq=seq)
            p_xq, xq_block = p, 3 * CONV_W // X_W
        x = _mixout(tok, p_xq, xq_block, kv, w_mix_out, x, mix_post_g, (i,), seq=seq)
        x = ffn(x, i, 1)
    return x.reshape(batch, seq, d)
```

```python
import functools

import jax
import jax.numpy as jnp
from jax import lax
from jax.experimental import pallas as pl
from jax.experimental.pallas import tpu as pltpu

F32 = jnp.float32
BF16 = jnp.bfloat16

D_MODEL = 2048
DEPTH = 2
MEM_LEN = 256
TOK_W = 1536
X_HEADS = 4
X_HEAD_DIM = 128
X_W = 512
GLA_HEADS = 4
GLA_DV = 384
GLA_DK = 192
GLA_KW = 768
GLA_RANK = 16
GLA_TAU = 16.0
CONV_W = 1536
CONV_K = 3
D_FF = 5632
EPS = 1e-6
LOG2E = 1.4426950408889634

LANE = 128
SUBLANE = 8
BF16_ROWS = 16
MXU_DIM = 256
VMEM_LIMIT = 56 * 1024 * 1024
VMEM_LIMIT_FFN = 60 * 1024 * 1024
NORM_ROWS = 256

GLA_RANKP = LANE
GLA_DKP = MXU_DIM
GLA_CHUNK = 64
GLA_MAIN_N = 2 * GLA_KW + 2 * TOK_W
GLA_SIDE_N = X_W + GLA_RANKP
CONV_NP = 3 * CONV_W + X_W


def _rmsnorm(x, g):
    ms = jnp.mean(x * x, axis=-1, keepdims=True)
    return x * lax.rsqrt(ms + EPS) * g


def _dot(a, b):
    return jnp.dot(a, b, preferred_element_type=F32)


def _dot_nt(a, b):
    return lax.dot_general(a, b, (((1,), (1,)), ((), ())), preferred_element_type=F32)


def _dot_tn(a, b):
    return lax.dot_general(a, b, (((0,), (0,)), ((), ())), preferred_element_type=F32)


def _stacked_spec(lead, block, index_map):
    nlead = len(lead)
    return pl.BlockSpec((None,) * nlead + block, lambda *g: tuple(lead) + tuple(index_map(*g)))


def _for_row_chunks(n_rows, body):
    def step(c, carry):
        body(pl.ds(pl.multiple_of(c * NORM_ROWS, NORM_ROWS), NORM_ROWS))
        return carry

    lax.fori_loop(0, n_rows // NORM_ROWS, step, 0, unroll=2)


def _row_rsqrt_ms(src_ref, rs_ref):
    def rows_rs(rows):
        v = src_ref[rows, :]
        ms = jnp.mean(v * v, axis=-1, keepdims=True)
        rs_ref[rows, :] = jnp.broadcast_to(lax.rsqrt(ms + EPS), (NORM_ROWS, LANE))

    _for_row_chunks(src_ref.shape[0], rows_rs)


def _ffn_kernel(x_ref, pre_g_ref, wg_ref, wu_ref, wo_ref, post_g_ref, o_ref, h_ref, rs_ref):
    j = pl.program_id(1)
    tm = x_ref.shape[0]

    @pl.when(j == 0)
    def _():
        def rows_in(rows):
            h_ref[rows, :] = _rmsnorm(x_ref[rows, :], pre_g_ref[...]).astype(BF16)
            o_ref[rows, :] = jnp.zeros((NORM_ROWS, o_ref.shape[1]), F32)

        _for_row_chunks(tm, rows_in)

    h = h_ref[...]
    gate = _dot(h, wg_ref[...].astype(BF16))
    up = _dot(h, wu_ref[...].astype(BF16))
    act = (gate * jax.nn.sigmoid(gate) * up).astype(BF16)
    o_ref[...] += _dot(act, wo_ref[...].astype(BF16))

    @pl.when(j == pl.num_programs(1) - 1)
    def _():
        _row_rsqrt_ms(o_ref, rs_ref)
        half_g = 0.5 * post_g_ref[...]

        def rows_out(rows):
            o_ref[rows, :] = x_ref[rows, :] + o_ref[rows, :] * rs_ref[rows, 0:1] * half_g

        _for_row_chunks(tm, rows_out)


def _ffn(x, pre_g, w_in, w_out, post_g, lead, *, tm=1024, tf=256):
    t, d = x.shape
    nf = D_FF // tf
    return pl.pallas_call(
        _ffn_kernel,
        grid=(t // tm, nf),
        in_specs=[
            pl.BlockSpec((tm, d), lambda i, j: (i, 0)),
            _stacked_spec(lead, (1, d), lambda i, j: (0, 0)),
            _stacked_spec(lead, (d, tf), lambda i, j: (0, j)),
            _stacked_spec(lead, (d, tf), lambda i, j: (0, j + nf)),
            _stacked_spec(lead, (tf, d), lambda i, j: (j, 0)),
            _stacked_spec(lead, (1, d), lambda i, j: (0, 0)),
        ],
        out_specs=pl.BlockSpec((tm, d), lambda i, j: (i, 0)),
        out_shape=jax.ShapeDtypeStruct((t, d), F32),
        scratch_shapes=[pltpu.VMEM((tm, d), BF16), pltpu.VMEM((tm, LANE), F32)],
        compiler_params=pltpu.CompilerParams(
            dimension_semantics=("parallel", "arbitrary"), vmem_limit_bytes=VMEM_LIMIT_FFN),
        name="ffn",
    )(x, pre_g, w_in, w_in, w_out, post_g)


def _norm_matmul_kernel(x_ref, g_ref, w_ref, o_ref, h_ref):
    @pl.when(pl.program_id(1) == 0)
    def _():
        h_ref[...] = _rmsnorm(x_ref[...], g_ref[...]).astype(BF16)

    o_ref[...] = _dot(h_ref[...], w_ref[...].astype(BF16)).astype(o_ref.dtype)


def _norm_matmul(x, g, g_lead, w, w_lead, *, tm, tn, name):
    t, d = x.shape
    n = w.shape[-1]
    return pl.pallas_call(
        _norm_matmul_kernel,
        grid=(t // tm, n // tn),
        in_specs=[
            pl.BlockSpec((tm, d), lambda i, j: (i, 0)),
            _stacked_spec(g_lead, (1, d), lambda i, j: (0, 0)),
            _stacked_spec(w_lead, (d, tn), lambda i, j: (0, j)),
        ],
        out_specs=pl.BlockSpec((tm, tn), lambda i, j: (i, j)),
        out_shape=jax.ShapeDtypeStruct((t, n), BF16),
        scratch_shapes=[pltpu.VMEM((tm, d), BF16)],
        compiler_params=pltpu.CompilerParams(
            dimension_semantics=("parallel", "arbitrary"), vmem_limit_bytes=VMEM_LIMIT),
        name=name,
    )(x, g, w)


def _gla_proj_kernel(x_ref, g_ref, wm_ref, ws_ref, om_ref, os_ref, h_ref, *, n_main):
    j = pl.program_id(1)

    @pl.when(j == 0)
    def _():
        h_ref[...] = _rmsnorm(x_ref[...], g_ref[...]).astype(BF16)

    @pl.when(j < n_main)
    def _():
        om_ref[...] = _dot(h_ref[...], wm_ref[...].astype(BF16)).astype(BF16)

    @pl.when(j == n_main)
    def _():
        os_ref[...] = _dot(h_ref[...], ws_ref[...].astype(BF16)).astype(BF16)


def _gla_proj(x, g, g_lead, w, w_lead, w_side, *, tm=1024, tn=768):
    t, d = x.shape
    n_main = GLA_MAIN_N // tn
    return pl.pallas_call(
        functools.partial(_gla_proj_kernel, n_main=n_main),
        grid=(t // tm, n_main + 1),
        in_specs=[
            pl.BlockSpec((tm, d), lambda i, j: (i, 0)),
            _stacked_spec(g_lead, (1, d), lambda i, j: (0, 0)),
            _stacked_spec(w_lead, (d, tn), lambda i, j: (0, jnp.minimum(j, n_main - 1))),
            pl.BlockSpec((d, GLA_SIDE_N), lambda i, j: (0, 0)),
        ],
        out_specs=[
            pl.BlockSpec((tm, tn), lambda i, j: (i, jnp.minimum(j, n_main - 1))),
            pl.BlockSpec((tm, GLA_SIDE_N), lambda i, j: (i, 0)),
        ],
        out_shape=[jax.ShapeDtypeStruct((t, GLA_MAIN_N), BF16),
                   jax.ShapeDtypeStruct((t, GLA_SIDE_N), BF16)],
        scratch_shapes=[pltpu.VMEM((tm, d), BF16)],
        compiler_params=pltpu.CompilerParams(
            dimension_semantics=("parallel", "arbitrary"), vmem_limit_bytes=VMEM_LIMIT),
        name="gla_proj",
    )(x, g, w, w_side)


def _shift_groups(a, n):
    return jnp.concatenate([a[n:], a[:n]], axis=0)


def _block_first_group(a, n):
    return jnp.concatenate([a[g - g % n:g - g % n + 1] for g in range(a.shape[0])], axis=0)


def _gla_kernel(qkv_ref, r_ref, gl_ref, w2_ref, b2_ref, hg_ref, o_ref, st_ref):
    c_len = GLA_CHUNK
    ng = c_len // SUBLANE
    nb = qkv_ref.shape[0]

    @pl.when(pl.program_id(0) == 0)
    def _():
        st_ref[...] = jnp.zeros_like(st_ref)

    r3 = lax.broadcasted_iota(jnp.int32, (ng, SUBLANE, GLA_DKP), 1)
    ti = lax.broadcasted_iota(jnp.int32, (c_len, c_len), 0)
    si = lax.broadcasted_iota(jnp.int32, (c_len, c_len), 1)
    txs = jnp.where(ti > si, ti ^ si, 0)
    to3 = lambda a: a.reshape(ng, SUBLANE, GLA_DKP)
    to2 = lambda a: a.reshape(c_len, GLA_DKP)

    streams = [divmod(s, GLA_HEADS) for s in range(nb * GLA_HEADS)]
    vcols = [pl.ds(hd * GLA_DV, GLA_DV) for _, hd in streams]
    each = lambda f, *lists: [f(*args) for args in zip(*lists)]

    z = [_dot(gl_ref[bi], w2_ref[hd]) + b2_ref[hd] for bi, hd in streams]
    qkv = [qkv_ref[bi, :, :2 * GLA_KW + GLA_DKP].astype(F32) for bi in range(nb)]
    lane = lax.broadcasted_iota(jnp.int32, (c_len, GLA_DKP), 1)
    head = lambda a, lo: to3(jnp.where(lane < GLA_DK, a[:, lo:lo + GLA_DKP], 0.0))
    q = [head(qkv[bi], hd * GLA_DK) * (GLA_DK ** -0.5) for bi, hd in streams]
    k = [head(qkv[bi], GLA_KW + hd * GLA_DK) for bi, hd in streams]

    def log2_decay(zs):
        soft = jnp.log2(1.0 + jnp.exp2(jnp.abs(zs) * -LOG2E))
        return to3((jnp.minimum(zs, 0.0) * LOG2E - soft) * (1.0 / GLA_TAU))

    cum = each(log2_decay, z)
    d = 1
    while d < SUBLANE:
        cum = each(lambda a: a + jnp.where(r3 >= d, pltpu.roll(a, d, axis=1), 0.0), cum)
        d *= 2
    tot = each(lambda a: a[:, SUBLANE - 1:SUBLANE, :], cum)
    offs = [[jnp.zeros_like(t[0:1])] for t in tot]
    for g in range(1, ng):
        for o_s, t in zip(offs, tot):
            o_s.append(o_s[-1] + t[g - 1:g])
    cum = each(lambda a, o_s: a + jnp.concatenate(o_s, axis=0), cum, offs)

    nt = lambda a, b: _dot_nt(to2(a).astype(BF16), to2(b).astype(BF16))
    attn = each(lambda a, b: jnp.where(ti == si, nt(a, b), 0.0), q, k)
    g_h = cum
    h = 1
    while h < c_len:
        if h < SUBLANE:
            if h > 1:
                g_h = each(lambda a: jnp.where((r3 & (h // 2)) != 0,
                                               pltpu.roll(a, h // 2, axis=1), a), g_h)
            g_next = each(lambda a: pltpu.roll(a, SUBLANE - h, axis=1), g_h)
        else:
            n = h // SUBLANE
            g_h = each(lambda a: jnp.broadcast_to(a[:, 0:1, :], a.shape), cum)
            if n > 1:
                g_h = each(lambda a: _block_first_group(a, n), g_h)
            g_next = each(lambda a: _shift_groups(a, n), g_h)
        q_h = q if h == 1 else each(lambda a, c, g: a * jnp.exp2(c - g), q, cum, g_h)
        k_h = each(lambda a, c, g: a * jnp.exp2(g - c), k, cum, g_next)
        a_h = each(nt, q_h, k_h)
        attn = each(lambda new, old: jnp.where((txs >= h) & (txs < 2 * h), new, old), a_h, attn)
        h *= 2

    last = each(lambda a: a[ng - 1, SUBLANE - 1:SUBLANE, :], cum)
    q_in = each(lambda a, c: to2(a * jnp.exp2(c)).astype(BF16), q, cum)
    k_out = each(lambda a, c, l: to2(a * jnp.exp2(l - c)).astype(BF16), k, cum, last)
    v = [qkv_ref[bi, :, pl.ds(2 * GLA_KW + hd * GLA_DV, GLA_DV)] for bi, hd in streams]
    st = [st_ref[s] for s in range(len(streams))]
    o = each(lambda qi, s_, a, v_: _dot_nt(qi, s_.astype(BF16)) + _dot(a.astype(BF16), v_),
             q_in, st, attn, v)
    for s, (s_, l, v_, ko) in enumerate(zip(st, last, v, k_out)):
        st_ref[s] = s_ * jnp.exp2(l) + _dot_tn(v_, ko)

    o = each(lambda a: _rmsnorm(a, hg_ref[...]), o)
    for (bi, _), c, a in zip(streams, vcols, o):
        r = r_ref[bi, :, c].astype(F32)
        o_ref[bi, :, c] = (a * (r * jax.nn.sigmoid(r))).astype(BF16)


def _gla(p_main, p_side, w2, b2, head_g, *, batch, seq):
    t = p_main.shape[0]
    p_main = p_main.reshape(batch, seq, GLA_MAIN_N)
    p_side = p_side.reshape(batch, seq, GLA_SIDE_N)
    c_len = GLA_CHUNK
    assert 2 * GLA_KW == TOK_W
    out = pl.pallas_call(
        _gla_kernel,
        grid=(seq // c_len,),
        in_specs=[
            pl.BlockSpec((batch, c_len, 2 * TOK_W), lambda i: (0, i, 0)),
            pl.BlockSpec((batch, c_len, TOK_W), lambda i: (0, i, 2)),
            pl.BlockSpec((batch, c_len, GLA_RANKP), lambda i: (0, i, 0)),
            pl.BlockSpec((GLA_HEADS, GLA_RANKP, GLA_DKP), lambda i: (0, 0, 0)),
            pl.BlockSpec((GLA_HEADS, 1, GLA_DKP), lambda i: (0, 0, 0)),
            pl.BlockSpec((1, GLA_DV), lambda i: (0, 0)),
        ],
        out_specs=pl.BlockSpec((batch, c_len, TOK_W), lambda i: (0, i, 0)),
        out_shape=jax.ShapeDtypeStruct((batch, seq, TOK_W), BF16),
        scratch_shapes=[pltpu.VMEM((batch * GLA_HEADS, GLA_DV, GLA_DKP), F32)],
        compiler_params=pltpu.CompilerParams(
            dimension_semantics=("arbitrary",), vmem_limit_bytes=VMEM_LIMIT),
        name="gla",
    )(p_main, p_main, p_side, w2, b2, head_g)
    return out.reshape(t, TOK_W)


def _conv_kernel(b_ref, c_ref, xt_ref, cp_ref, xp_ref, w_ref, bias_ref, o_ref, *, tiles_per_seq):
    i = pl.program_id(0)
    u = c_ref[...].astype(F32) * xt_ref[...].astype(F32)
    keep = jnp.where(i % tiles_per_seq != 0, 1.0, 0.0)
    u_prev = cp_ref[...].astype(F32) * xp_ref[...].astype(F32) * keep
    row = lax.broadcasted_iota(jnp.int32, u.shape, 0)
    u1 = jnp.where(row == 0, u_prev[BF16_ROWS - 1:BF16_ROWS, :], pltpu.roll(u, 1, axis=0))
    u2 = pltpu.roll(u, 2, axis=0)
    u2 = jnp.where(row == 0, u_prev[BF16_ROWS - 2:BF16_ROWS - 1, :], u2)
    u2 = jnp.where(row == 1, u_prev[BF16_ROWS - 1:BF16_ROWS, :], u2)
    w = w_ref[...]
    y = u2 * w[0:1, :] + u1 * w[1:2, :] + u * w[2:3, :] + bias_ref[...]
    o_ref[...] = (b_ref[...].astype(F32) * y).astype(BF16)


def _conv(p, conv_w, conv_b, lead, *, seq, ts=512):
    t = p.shape[0]
    prev = lambda i: jnp.maximum(i * (ts // BF16_ROWS) - 1, 0)
    return pl.pallas_call(
        functools.partial(_conv_kernel, tiles_per_seq=seq // ts),
        grid=(t // ts,),
        in_specs=[
            pl.BlockSpec((ts, CONV_W), lambda i: (i, 0)),
            pl.BlockSpec((ts, CONV_W), lambda i: (i, 1)),
            pl.BlockSpec((ts, CONV_W), lambda i: (i, 2)),
            pl.BlockSpec((BF16_ROWS, CONV_W), lambda i: (prev(i), 1)),
            pl.BlockSpec((BF16_ROWS, CONV_W), lambda i: (prev(i), 2)),
            _stacked_spec(lead, (CONV_K, CONV_W), lambda i: (0, 0)),
            _stacked_spec(lead, (1, CONV_W), lambda i: (0, 0)),
        ],
        out_specs=pl.BlockSpec((ts, CONV_W), lambda i: (i, 0)),
        out_shape=jax.ShapeDtypeStruct((t, CONV_W), BF16),
        compiler_params=pltpu.CompilerParams(
            dimension_semantics=("parallel",), vmem_limit_bytes=VMEM_LIMIT),
        name="conv",
    )(p, p, p, p, p, conv_w, conv_b)


def _mixout_kernel(tok_ref, xq_ref, kv_ref, wm_ref, x_ref, g_ref, o_ref, rs_ref, *, q_off):
    mixed = _dot(tok_ref[...], wm_ref[:TOK_W, :].astype(BF16))
    xo = []
    for h in range(X_HEADS):
        lo = h * X_HEAD_DIM
        kh = kv_ref[:, lo:lo + X_HEAD_DIM]
        vh = kv_ref[:, X_W + lo:X_W + lo + X_HEAD_DIM]
        if q_off == 0:
            qh = xq_ref[:, lo:lo + X_HEAD_DIM]
        else:
            qh = xq_ref[:, lo:lo + 2 * X_HEAD_DIM]
            kh = jnp.concatenate([kh.astype(F32), jnp.zeros(kh.shape, F32)], axis=1)
            kh = pltpu.roll(kh, q_off, axis=1).astype(BF16)
        s = _dot_nt(qh, kh) * (X_HEAD_DIM ** -0.5)
        e = jnp.exp(s - jnp.max(s, axis=-1, keepdims=True))
        pr = (e / jnp.sum(e, axis=-1, keepdims=True)).astype(BF16)
        xo.append(_dot(pr, vh).astype(BF16))
    o_ref[...] = mixed + _dot(jnp.concatenate(xo, axis=1), wm_ref[TOK_W:, :].astype(BF16))

    _row_rsqrt_ms(o_ref, rs_ref)

    def rows_out(rows):
        o_ref[rows, :] = x_ref[rows, :] + o_ref[rows, :] * rs_ref[rows, 0:1] * g_ref[...]

    _for_row_chunks(o_ref.shape[0], rows_out)


def _mixout(tok, p, xq_block, q_off, kv, wm, x, g, lead, *, seq, tm=512):
    t, d = x.shape
    per_seq = seq // tm
    xq_w = X_W if q_off == 0 else X_W + LANE
    return pl.pallas_call(
        functools.partial(_mixout_kernel, q_off=q_off),
        grid=(t // tm,),
        in_specs=[
            pl.BlockSpec((tm, TOK_W), lambda i: (i, 0)),
            pl.BlockSpec((tm, xq_w), lambda i: (i, xq_block)),
            pl.BlockSpec((MEM_LEN, 2 * X_W), lambda i: (i // per_seq, 0)),
            pl.BlockSpec((None,) * len(lead) + (d, d), lambda i: tuple(lead) + (0, 0),
                         pipeline_mode=pl.Buffered(1)),
            pl.BlockSpec((tm, d), lambda i: (i, 0)),
            _stacked_spec(lead, (1, d), lambda i: (0, 0)),
        ],
        out_specs=pl.BlockSpec((tm, d), lambda i: (i, 0)),
        out_shape=jax.ShapeDtypeStruct((t, d), F32),
        scratch_shapes=[pltpu.VMEM((tm, LANE), F32)],
        compiler_params=pltpu.CompilerParams(
            dimension_semantics=("parallel",), vmem_limit_bytes=VMEM_LIMIT),
        name="mixout",
    )(tok, p, kv, wm, x, g)


def _gla_side_weights(w_in, gate_w2, gate_b):
    tail = w_in[:, GLA_MAIN_N:]
    w_side = jnp.pad(tail, ((0, 0), (0, GLA_SIDE_N - tail.shape[1])))
    w2 = gate_w2.reshape(GLA_RANK, GLA_HEADS, GLA_DK).transpose(1, 0, 2)
    w2 = jnp.pad(w2, ((0, 0), (0, GLA_RANKP - GLA_RANK), (0, GLA_DKP - GLA_DK))).astype(BF16)
    b2 = jnp.pad(gate_b.reshape(GLA_HEADS, 1, GLA_DK), ((0, 0), (0, 0), (0, GLA_DKP - GLA_DK)))
    return w_side, w2, b2


def kernel(x, mem, ffn_pre_g, ffn_w_in, ffn_w_out, ffn_post_g, mix_pre_g, mix_post_g, mem_g,
           w_mem_kv, w_mix_out, gla_w_in, gla_gate_w2, gla_gate_b, gla_head_g,
           conv_w_in, conv_w, conv_b):
    batch, seq, d = x.shape
    x = x.reshape(batch * seq, d)
    mem2 = mem.reshape(batch * MEM_LEN, d)
    ffn_pre_g = ffn_pre_g[:, :, None, :]
    ffn_post_g = ffn_post_g[:, :, None, :]
    mix_pre_g = mix_pre_g[:, None, :]
    mix_post_g = mix_post_g[:, None, :]
    mem_g = mem_g[:, None, :]
    conv_b = conv_b[:, None, :]

    def ffn(x, i, k):
        return _ffn(x, ffn_pre_g, ffn_w_in, ffn_w_out, ffn_post_g, (i, k))

    for i in range(DEPTH):
        x = ffn(x, i, 0)
        j = i // 2
        kv = _norm_matmul(mem2, mem_g, (i,), w_mem_kv, (i,),
                          tm=batch * MEM_LEN, tn=2 * X_W, name="mem_kv")
        if i % 2 == 0:
            w_side, w2, b2 = _gla_side_weights(gla_w_in[j], gla_gate_w2[j], gla_gate_b[j])
            p, p_xq = _gla_proj(x, mix_pre_g, (i,), gla_w_in, (j,), w_side)
            tok = _gla(p, p_xq, w2, b2, gla_head_g[j][None], batch=batch, seq=seq)
            xq_block, q_off = 0, GLA_RANK
        else:
            p = _norm_matmul(x, mix_pre_g, (i,), conv_w_in, (j,), tm=1024, tn=CONV_NP // 5,
                             name="conv_proj")
            tok = _conv(p, conv_w, conv_b, (j,), seq=seq)
            p_xq, xq_block, q_off = p, 3 * CONV_W // X_W, 0
        x = _mixout(tok, p_xq, xq_block, q_off, kv, w_mix_out, x, mix_post_g, (i,), seq=seq)
        x = ffn(x, i, 1)
    return x.reshape(batch, seq, d)
```

```python
import functools

import jax
import jax.numpy as jnp
from jax import lax
from jax.experimental import pallas as pl
from jax.experimental.pallas import tpu as pltpu

F32 = jnp.float32
BF16 = jnp.bfloat16

D_MODEL = 2048
DEPTH = 2
MEM_LEN = 256
TOK_W = 1536
X_HEADS = 4
X_HEAD_DIM = 128
X_W = 512
GLA_HEADS = 4
GLA_DV = 384
GLA_DK = 192
GLA_KW = 768
GLA_RANK = 16
GLA_TAU = 16.0
CONV_W = 1536
CONV_K = 3
D_FF = 5632
EPS = 1e-6
LOG2E = 1.4426950408889634

LANE = 128
SUBLANE = 8
BF16_ROWS = 16
MXU_DIM = 256
VMEM_LIMIT = 56 * 1024 * 1024
VMEM_LIMIT_FFN = 60 * 1024 * 1024
NORM_ROWS = 256
FFN_TF_F32 = 256
FFN_TF_BF16 = 512

GLA_RANKP = LANE
GLA_DKP = MXU_DIM
GLA_CHUNK = 64
GLA_MAIN_N = 2 * GLA_KW + 2 * TOK_W
GLA_SIDE_N = X_W + GLA_RANKP
CONV_NP = 3 * CONV_W + X_W


def _rmsnorm(x, g):
    ms = jnp.mean(x * x, axis=-1, keepdims=True)
    return x * lax.rsqrt(ms + EPS) * g


def _dot(a, b):
    return jnp.dot(a, b, preferred_element_type=F32)


def _dot_nt(a, b):
    return lax.dot_general(a, b, (((1,), (1,)), ((), ())), preferred_element_type=F32)


def _dot_tn(a, b):
    return lax.dot_general(a, b, (((0,), (0,)), ((), ())), preferred_element_type=F32)


def _stacked_spec(lead, block, index_map):
    nlead = len(lead)
    return pl.BlockSpec((None,) * nlead + block, lambda *g: tuple(lead) + tuple(index_map(*g)))


def _for_row_chunks(n_rows, body):
    def step(c, carry):
        body(pl.ds(pl.multiple_of(c * NORM_ROWS, NORM_ROWS), NORM_ROWS))
        return carry

    lax.fori_loop(0, n_rows // NORM_ROWS, step, 0, unroll=2)


def _row_rsqrt_ms(src_ref, rs_ref):
    def rows_rs(rows):
        v = src_ref[rows, :]
        ms = jnp.mean(v * v, axis=-1, keepdims=True)
        rs_ref[rows, :] = jnp.broadcast_to(lax.rsqrt(ms + EPS), (NORM_ROWS, LANE))

    _for_row_chunks(src_ref.shape[0], rows_rs)


def _cast_spec_pair(w, lead, rows, cols, index_map):
    shape = w.shape[len(lead):]
    return (_stacked_spec(lead, (rows, cols), index_map),
            pl.BlockSpec((rows, cols), index_map),
            jax.ShapeDtypeStruct(shape, BF16))


def _ffn_kernel(*refs, cast_next):
    if cast_next:
        (x_ref, pre_g_ref, wg_ref, wu_ref, wo_ref, post_g_ref, nwi_ref, nwo_ref,
         o_ref, cwi_ref, cwo_ref, h_ref, rs_ref) = refs
        cwi_ref[...] = nwi_ref[...].astype(BF16)
        cwo_ref[...] = nwo_ref[...].astype(BF16)
    else:
        x_ref, pre_g_ref, wg_ref, wu_ref, wo_ref, post_g_ref, o_ref, h_ref, rs_ref = refs
    j = pl.program_id(1)
    tm = x_ref.shape[0]

    @pl.when(j == 0)
    def _():
        def rows_in(rows):
            h_ref[rows, :] = _rmsnorm(x_ref[rows, :], pre_g_ref[...]).astype(BF16)
            o_ref[rows, :] = jnp.zeros((NORM_ROWS, o_ref.shape[1]), F32)

        _for_row_chunks(tm, rows_in)

    h = h_ref[...]
    gate = _dot(h, wg_ref[...].astype(BF16))
    up = _dot(h, wu_ref[...].astype(BF16))
    act = (gate * jax.nn.sigmoid(gate) * up).astype(BF16)
    o_ref[...] += _dot(act, wo_ref[...].astype(BF16))

    @pl.when(j == pl.num_programs(1) - 1)
    def _():
        _row_rsqrt_ms(o_ref, rs_ref)
        half_g = 0.5 * post_g_ref[...]

        def rows_out(rows):
            o_ref[rows, :] = x_ref[rows, :] + o_ref[rows, :] * rs_ref[rows, 0:1] * half_g

        _for_row_chunks(tm, rows_out)


def _ffn(x, gains, g_lead, weights, w_lead, *, tf, tm=1024, cast_next=None):
    t, d = x.shape
    nt, nf = t // tm, D_FF // tf
    pre_g, post_g = gains
    w_in, w_out = weights
    in_specs = [
        pl.BlockSpec((tm, d), lambda i, j: (i, 0)),
        _stacked_spec(g_lead, (1, d), lambda i, j: (0, 0)),
        _stacked_spec(w_lead, (d, tf), lambda i, j: (0, j)),
        _stacked_spec(w_lead, (d, tf), lambda i, j: (0, j + nf)),
        _stacked_spec(w_lead, (tf, d), lambda i, j: (j, 0)),
        _stacked_spec(g_lead, (1, d), lambda i, j: (0, 0)),
    ]
    out_specs = [pl.BlockSpec((tm, d), lambda i, j: (i, 0))]
    out_shape = [jax.ShapeDtypeStruct((t, d), F32)]
    args = [x, pre_g, w_in, w_in, w_out, post_g]
    if cast_next is not None:
        n_in, n_out, n_lead = cast_next
        pairs = [_cast_spec_pair(n_in, n_lead, d // nt, 2 * D_FF // nf, lambda i, j: (i, j)),
                 _cast_spec_pair(n_out, n_lead, D_FF // nf, d // nt, lambda i, j: (j, i))]
        in_specs += [p[0] for p in pairs]
        out_specs += [p[1] for p in pairs]
        out_shape += [p[2] for p in pairs]
        args += [n_in, n_out]
    res = pl.pallas_call(
        functools.partial(_ffn_kernel, cast_next=cast_next is not None),
        grid=(nt, nf),
        in_specs=in_specs,
        out_specs=out_specs,
        out_shape=out_shape,
        scratch_shapes=[pltpu.VMEM((tm, d), BF16), pltpu.VMEM((tm, LANE), F32)],
        compiler_params=pltpu.CompilerParams(
            dimension_semantics=("parallel", "arbitrary"), vmem_limit_bytes=VMEM_LIMIT_FFN),
        name="ffn",
    )(*args)
    return res[0], tuple(res[1:])


def _norm_matmul_kernel(x_ref, g_ref, w_ref, o_ref, h_ref):
    @pl.when(pl.program_id(1) == 0)
    def _():
        h_ref[...] = _rmsnorm(x_ref[...], g_ref[...]).astype(BF16)

    o_ref[...] = _dot(h_ref[...], w_ref[...].astype(BF16)).astype(o_ref.dtype)


def _norm_matmul(x, g, g_lead, w, w_lead, *, tm, tn, name):
    t, d = x.shape
    n = w.shape[-1]
    return pl.pallas_call(
        _norm_matmul_kernel,
        grid=(t // tm, n // tn),
        in_specs=[
            pl.BlockSpec((tm, d), lambda i, j: (i, 0)),
            _stacked_spec(g_lead, (1, d), lambda i, j: (0, 0)),
            _stacked_spec(w_lead, (d, tn), lambda i, j: (0, j)),
        ],
        out_specs=pl.BlockSpec((tm, tn), lambda i, j: (i, j)),
        out_shape=jax.ShapeDtypeStruct((t, n), BF16),
        scratch_shapes=[pltpu.VMEM((tm, d), BF16)],
        compiler_params=pltpu.CompilerParams(
            dimension_semantics=("parallel", "arbitrary"), vmem_limit_bytes=VMEM_LIMIT),
        name=name,
    )(x, g, w)


def _gla_proj_kernel(x_ref, g_ref, wm_ref, ws_ref, om_ref, os_ref, h_ref, *, n_main):
    j = pl.program_id(1)

    @pl.when(j == 0)
    def _():
        h_ref[...] = _rmsnorm(x_ref[...], g_ref[...]).astype(BF16)

    @pl.when(j < n_main)
    def _():
        om_ref[...] = _dot(h_ref[...], wm_ref[...].astype(BF16)).astype(BF16)

    @pl.when(j == n_main)
    def _():
        os_ref[...] = _dot(h_ref[...], ws_ref[...].astype(BF16)).astype(BF16)


def _gla_proj(x, g, g_lead, w, w_lead, w_side, *, tm=1024, tn=768):
    t, d = x.shape
    n_main = GLA_MAIN_N // tn
    return pl.pallas_call(
        functools.partial(_gla_proj_kernel, n_main=n_main),
        grid=(t // tm, n_main + 1),
        in_specs=[
            pl.BlockSpec((tm, d), lambda i, j: (i, 0)),
            _stacked_spec(g_lead, (1, d), lambda i, j: (0, 0)),
            _stacked_spec(w_lead, (d, tn), lambda i, j: (0, jnp.minimum(j, n_main - 1))),
            pl.BlockSpec((d, GLA_SIDE_N), lambda i, j: (0, 0)),
        ],
        out_specs=[
            pl.BlockSpec((tm, tn), lambda i, j: (i, jnp.minimum(j, n_main - 1))),
            pl.BlockSpec((tm, GLA_SIDE_N), lambda i, j: (i, 0)),
        ],
        out_shape=[jax.ShapeDtypeStruct((t, GLA_MAIN_N), BF16),
                   jax.ShapeDtypeStruct((t, GLA_SIDE_N), BF16)],
        scratch_shapes=[pltpu.VMEM((tm, d), BF16)],
        compiler_params=pltpu.CompilerParams(
            dimension_semantics=("parallel", "arbitrary"), vmem_limit_bytes=VMEM_LIMIT),
        name="gla_proj",
    )(x, g, w, w_side)


def _shift_groups(a, n):
    return jnp.concatenate([a[n:], a[:n]], axis=0)


def _block_first_group(a, n):
    return jnp.concatenate([a[g - g % n:g - g % n + 1] for g in range(a.shape[0])], axis=0)


def _gla_kernel(qkv_ref, r_ref, gl_ref, w2_ref, b2_ref, hg_ref, nwi_ref, nwo_ref,
                o_ref, cwi_ref, cwo_ref, st_ref):
    cwi_ref[...] = nwi_ref[...].astype(BF16)
    cwo_ref[...] = nwo_ref[...].astype(BF16)
    c_len = GLA_CHUNK
    ng = c_len // SUBLANE
    nb = qkv_ref.shape[0]

    @pl.when(pl.program_id(0) == 0)
    def _():
        st_ref[...] = jnp.zeros_like(st_ref)

    r3 = lax.broadcasted_iota(jnp.int32, (ng, SUBLANE, GLA_DKP), 1)
    ti = lax.broadcasted_iota(jnp.int32, (c_len, c_len), 0)
    si = lax.broadcasted_iota(jnp.int32, (c_len, c_len), 1)
    txs = jnp.where(ti > si, ti ^ si, 0)
    to3 = lambda a: a.reshape(ng, SUBLANE, GLA_DKP)
    to2 = lambda a: a.reshape(c_len, GLA_DKP)

    streams = [divmod(s, GLA_HEADS) for s in range(nb * GLA_HEADS)]
    vcols = [pl.ds(hd * GLA_DV, GLA_DV) for _, hd in streams]
    each = lambda f, *lists: [f(*args) for args in zip(*lists)]

    z = [_dot(gl_ref[bi], w2_ref[hd]) + b2_ref[hd] for bi, hd in streams]
    qkv = [qkv_ref[bi, :, :2 * GLA_KW + GLA_DKP].astype(F32) for bi in range(nb)]
    lane = lax.broadcasted_iota(jnp.int32, (c_len, GLA_DKP), 1)
    head = lambda a, lo: to3(jnp.where(lane < GLA_DK, a[:, lo:lo + GLA_DKP], 0.0))
    q = [head(qkv[bi], hd * GLA_DK) * (GLA_DK ** -0.5) for bi, hd in streams]
    k = [head(qkv[bi], GLA_KW + hd * GLA_DK) for bi, hd in streams]

    def log2_decay(zs):
        soft = jnp.log2(1.0 + jnp.exp2(jnp.abs(zs) * -LOG2E))
        return to3((jnp.minimum(zs, 0.0) * LOG2E - soft) * (1.0 / GLA_TAU))

    cum = each(log2_decay, z)
    d = 1
    while d < SUBLANE:
        cum = each(lambda a: a + jnp.where(r3 >= d, pltpu.roll(a, d, axis=1), 0.0), cum)
        d *= 2
    tot = each(lambda a: a[:, SUBLANE - 1:SUBLANE, :], cum)
    offs = [[jnp.zeros_like(t[0:1])] for t in tot]
    for g in range(1, ng):
        for o_s, t in zip(offs, tot):
            o_s.append(o_s[-1] + t[g - 1:g])
    cum = each(lambda a, o_s: a + jnp.concatenate(o_s, axis=0), cum, offs)

    nt = lambda a, b: _dot_nt(to2(a).astype(BF16), to2(b).astype(BF16))
    attn = each(lambda a, b: jnp.where(ti == si, nt(a, b), 0.0), q, k)
    g_h = cum
    h = 1
    while h < c_len:
        if h < SUBLANE:
            if h > 1:
                g_h = each(lambda a: jnp.where((r3 & (h // 2)) != 0,
                                               pltpu.roll(a, h // 2, axis=1), a), g_h)
            g_next = each(lambda a: pltpu.roll(a, SUBLANE - h, axis=1), g_h)
        else:
            n = h // SUBLANE
            g_h = each(lambda a: jnp.broadcast_to(a[:, 0:1, :], a.shape), cum)
            if n > 1:
                g_h = each(lambda a: _block_first_group(a, n), g_h)
            g_next = each(lambda a: _shift_groups(a, n), g_h)
        q_h = q if h == 1 else each(lambda a, c, g: a * jnp.exp2(c - g), q, cum, g_h)
        k_h = each(lambda a, c, g: a * jnp.exp2(g - c), k, cum, g_next)
        a_h = each(nt, q_h, k_h)
        attn = each(lambda new, old: jnp.where((txs >= h) & (txs < 2 * h), new, old), a_h, attn)
        h *= 2

    last = each(lambda a: a[ng - 1, SUBLANE - 1:SUBLANE, :], cum)
    q_in = each(lambda a, c: to2(a * jnp.exp2(c)).astype(BF16), q, cum)
    k_out = each(lambda a, c, l: to2(a * jnp.exp2(l - c)).astype(BF16), k, cum, last)
    v = [qkv_ref[bi, :, pl.ds(2 * GLA_KW + hd * GLA_DV, GLA_DV)] for bi, hd in streams]
    st = [st_ref[s] for s in range(len(streams))]
    o = each(lambda qi, s_, a, v_: _dot_nt(qi, s_.astype(BF16)) + _dot(a.astype(BF16), v_),
             q_in, st, attn, v)
    for s, (s_, l, v_, ko) in enumerate(zip(st, last, v, k_out)):
        st_ref[s] = s_ * jnp.exp2(l) + _dot_tn(v_, ko)

    o = each(lambda a: _rmsnorm(a, hg_ref[...]), o)
    for (bi, _), c, a in zip(streams, vcols, o):
        r = r_ref[bi, :, c].astype(F32)
        o_ref[bi, :, c] = (a * (r * jax.nn.sigmoid(r))).astype(BF16)


def _gla(p_main, p_side, w2, b2, head_g, cast_next, *, batch, seq):
    t = p_main.shape[0]
    p_main = p_main.reshape(batch, seq, GLA_MAIN_N)
    p_side = p_side.reshape(batch, seq, GLA_SIDE_N)
    c_len = GLA_CHUNK
    steps = seq // c_len
    assert 2 * GLA_KW == TOK_W
    n_in, n_out, n_lead = cast_next
    split = 8
    assert split * split == steps
    by_step = lambda i: (i // split, i % split)
    pairs = [_cast_spec_pair(n_in, n_lead, n_in.shape[-2] // split, n_in.shape[-1] // split, by_step),
             _cast_spec_pair(n_out, n_lead, n_out.shape[-2] // split, n_out.shape[-1] // split, by_step)]
    res = pl.pallas_call(
        _gla_kernel,
        grid=(steps,),
        in_specs=[
            pl.BlockSpec((batch, c_len, 2 * TOK_W), lambda i: (0, i, 0)),
            pl.BlockSpec((batch, c_len, TOK_W), lambda i: (0, i, 2)),
            pl.BlockSpec((batch, c_len, GLA_RANKP), lambda i: (0, i, 0)),
            pl.BlockSpec((GLA_HEADS, GLA_RANKP, GLA_DKP), lambda i: (0, 0, 0)),
            pl.BlockSpec((GLA_HEADS, 1, GLA_DKP), lambda i: (0, 0, 0)),
            pl.BlockSpec((1, GLA_DV), lambda i: (0, 0)),
        ] + [p[0] for p in pairs],
        out_specs=[pl.BlockSpec((batch, c_len, TOK_W), lambda i: (0, i, 0))] + [p[1] for p in pairs],
        out_shape=[jax.ShapeDtypeStruct((batch, seq, TOK_W), BF16)] + [p[2] for p in pairs],
        scratch_shapes=[pltpu.VMEM((batch * GLA_HEADS, GLA_DV, GLA_DKP), F32)],
        compiler_params=pltpu.CompilerParams(
            dimension_semantics=("arbitrary",), vmem_limit_bytes=VMEM_LIMIT),
        name="gla",
    )(p_main, p_main, p_side, w2, b2, head_g, n_in, n_out)
    return res[0].reshape(t, TOK_W), tuple(res[1:])


def _conv_kernel(b_ref, c_ref, xt_ref, cp_ref, xp_ref, w_ref, bias_ref, o_ref, *, tiles_per_seq):
    i = pl.program_id(0)
    u = c_ref[...].astype(F32) * xt_ref[...].astype(F32)
    keep = jnp.where(i % tiles_per_seq != 0, 1.0, 0.0)
    u_prev = cp_ref[...].astype(F32) * xp_ref[...].astype(F32) * keep
    row = lax.broadcasted_iota(jnp.int32, u.shape, 0)
    u1 = jnp.where(row == 0, u_prev[BF16_ROWS - 1:BF16_ROWS, :], pltpu.roll(u, 1, axis=0))
    u2 = pltpu.roll(u, 2, axis=0)
    u2 = jnp.where(row == 0, u_prev[BF16_ROWS - 2:BF16_ROWS - 1, :], u2)
    u2 = jnp.where(row == 1, u_prev[BF16_ROWS - 1:BF16_ROWS, :], u2)
    w = w_ref[...]
    y = u2 * w[0:1, :] + u1 * w[1:2, :] + u * w[2:3, :] + bias_ref[...]
    o_ref[...] = (b_ref[...].astype(F32) * y).astype(BF16)


def _conv(p, conv_w, conv_b, lead, *, seq, ts=512):
    t = p.shape[0]
    prev = lambda i: jnp.maximum(i * (ts // BF16_ROWS) - 1, 0)
    return pl.pallas_call(
        functools.partial(_conv_kernel, tiles_per_seq=seq // ts),
        grid=(t // ts,),
        in_specs=[
            pl.BlockSpec((ts, CONV_W), lambda i: (i, 0)),
            pl.BlockSpec((ts, CONV_W), lambda i: (i, 1)),
            pl.BlockSpec((ts, CONV_W), lambda i: (i, 2)),
            pl.BlockSpec((BF16_ROWS, CONV_W), lambda i: (prev(i), 1)),
            pl.BlockSpec((BF16_ROWS, CONV_W), lambda i: (prev(i), 2)),
            _stacked_spec(lead, (CONV_K, CONV_W), lambda i: (0, 0)),
            _stacked_spec(lead, (1, CONV_W), lambda i: (0, 0)),
        ],
        out_specs=pl.BlockSpec((ts, CONV_W), lambda i: (i, 0)),
        out_shape=jax.ShapeDtypeStruct((t, CONV_W), BF16),
        compiler_params=pltpu.CompilerParams(
            dimension_semantics=("parallel",), vmem_limit_bytes=VMEM_LIMIT),
        name="conv",
    )(p, p, p, p, p, conv_w, conv_b)


def _mixout_kernel(tok_ref, xq_ref, kv_ref, wm_ref, x_ref, g_ref, o_ref, rs_ref, *, q_off):
    mixed = _dot(tok_ref[...], wm_ref[:TOK_W, :].astype(BF16))
    xo = []
    for h in range(X_HEADS):
        lo = h * X_HEAD_DIM
        kh = kv_ref[:, lo:lo + X_HEAD_DIM]
        vh = kv_ref[:, X_W + lo:X_W + lo + X_HEAD_DIM]
        if q_off == 0:
            qh = xq_ref[:, lo:lo + X_HEAD_DIM]
        else:
            qh = xq_ref[:, lo:lo + 2 * X_HEAD_DIM]
            kh = jnp.concatenate([kh.astype(F32), jnp.zeros(kh.shape, F32)], axis=1)
            kh = pltpu.roll(kh, q_off, axis=1).astype(BF16)
        s = _dot_nt(qh, kh) * (X_HEAD_DIM ** -0.5)
        e = jnp.exp(s - jnp.max(s, axis=-1, keepdims=True))
        pr = (e / jnp.sum(e, axis=-1, keepdims=True)).astype(BF16)
        xo.append(_dot(pr, vh).astype(BF16))
    o_ref[...] = mixed + _dot(jnp.concatenate(xo, axis=1), wm_ref[TOK_W:, :].astype(BF16))

    _row_rsqrt_ms(o_ref, rs_ref)

    def rows_out(rows):
        o_ref[rows, :] = x_ref[rows, :] + o_ref[rows, :] * rs_ref[rows, 0:1] * g_ref[...]

    _for_row_chunks(o_ref.shape[0], rows_out)


def _mixout(tok, p, xq_block, q_off, kv, wm, x, g, lead, *, seq, tm=512):
    t, d = x.shape
    per_seq = seq // tm
    xq_w = X_W if q_off == 0 else X_W + LANE
    return pl.pallas_call(
        functools.partial(_mixout_kernel, q_off=q_off),
        grid=(t // tm,),
        in_specs=[
            pl.BlockSpec((tm, TOK_W), lambda i: (i, 0)),
            pl.BlockSpec((tm, xq_w), lambda i: (i, xq_block)),
            pl.BlockSpec((MEM_LEN, 2 * X_W), lambda i: (i // per_seq, 0)),
            pl.BlockSpec((None,) * len(lead) + (d, d), lambda i: tuple(lead) + (0, 0),
                         pipeline_mode=pl.Buffered(1)),
            pl.BlockSpec((tm, d), lambda i: (i, 0)),
            _stacked_spec(lead, (1, d), lambda i: (0, 0)),
        ],
        out_specs=pl.BlockSpec((tm, d), lambda i: (i, 0)),
        out_shape=jax.ShapeDtypeStruct((t, d), F32),
        scratch_shapes=[pltpu.VMEM((tm, LANE), F32)],
        compiler_params=pltpu.CompilerParams(
            dimension_semantics=("parallel",), vmem_limit_bytes=VMEM_LIMIT),
        name="mixout",
    )(tok, p, kv, wm, x, g)


def _gla_side_weights(w_in, gate_w2, gate_b):
    tail = w_in[:, GLA_MAIN_N:]
    w_side = jnp.pad(tail, ((0, 0), (0, GLA_SIDE_N - tail.shape[1])))
    w2 = gate_w2.reshape(GLA_RANK, GLA_HEADS, GLA_DK).transpose(1, 0, 2)
    w2 = jnp.pad(w2, ((0, 0), (0, GLA_RANKP - GLA_RANK), (0, GLA_DKP - GLA_DK))).astype(BF16)
    b2 = jnp.pad(gate_b.reshape(GLA_HEADS, 1, GLA_DK), ((0, 0), (0, 0), (0, GLA_DKP - GLA_DK)))
    return w_side, w2, b2


def kernel(x, mem, ffn_pre_g, ffn_w_in, ffn_w_out, ffn_post_g, mix_pre_g, mix_post_g, mem_g,
           w_mem_kv, w_mix_out, gla_w_in, gla_gate_w2, gla_gate_b, gla_head_g,
           conv_w_in, conv_w, conv_b):
    batch, seq, d = x.shape
    x = x.reshape(batch * seq, d)
    mem2 = mem.reshape(batch * MEM_LEN, d)
    ffn_pre_g = ffn_pre_g[:, :, None, :]
    ffn_post_g = ffn_post_g[:, :, None, :]
    mix_pre_g = mix_pre_g[:, None, :]
    mix_post_g = mix_post_g[:, None, :]
    mem_g = mem_g[:, None, :]
    conv_b = conv_b[:, None, :]

    ffn_gains = (ffn_pre_g, ffn_post_g)
    ffn_stacks = (ffn_w_in, ffn_w_out)

    def ffn(x, i, k, ready, nxt):
        if ready is None:
            x, _ = _ffn(x, ffn_gains, (i, k), ffn_stacks, (i, k), tf=FFN_TF_F32)
            return x, None
        cast_next = None if nxt is None else ffn_stacks + (nxt,)
        x, cast = _ffn(x, ffn_gains, (i, k), ready, (), tf=FFN_TF_BF16, cast_next=cast_next)
        return x, (cast or None)

    ready = None
    for i in range(DEPTH):
        j = i // 2
        gla_layer = i % 2 == 0
        x, ready = ffn(x, i, 0, ready, None if gla_layer else (i, 1))
        kv = _norm_matmul(mem2, mem_g, (i,), w_mem_kv, (i,),
                          tm=batch * MEM_LEN, tn=2 * X_W, name="mem_kv")
        if gla_layer:
            w_side, w2, b2 = _gla_side_weights(gla_w_in[j], gla_gate_w2[j], gla_gate_b[j])
            p, p_xq = _gla_proj(x, mix_pre_g, (i,), gla_w_in, (j,), w_side)
            tok, ready = _gla(p, p_xq, w2, b2, gla_head_g[j][None], ffn_stacks + ((i, 1),),
                              batch=batch, seq=seq)
            xq_block, q_off = 0, GLA_RANK
        else:
            p = _norm_matmul(x, mix_pre_g, (i,), conv_w_in, (j,), tm=1024, tn=CONV_NP // 5,
                             name="conv_proj")
            tok = _conv(p, conv_w, conv_b, (j,), seq=seq)
            p_xq, xq_block, q_off = p, 3 * CONV_W // X_W, 0
        x = _mixout(tok, p_xq, xq_block, q_off, kv, w_mix_out, x, mix_post_g, (i,), seq=seq)
        x, ready = ffn(x, i, 1, ready, (i + 1, 0) if i + 1 < DEPTH else None)
    return x.reshape(batch, seq, d)
```

```python
import functools

import jax
import jax.numpy as jnp
from jax import lax
from jax.experimental import pallas as pl
from jax.experimental.pallas import tpu as pltpu

F32 = jnp.float32
BF16 = jnp.bfloat16

D_MODEL = 2048
DEPTH = 2
MEM_LEN = 256
TOK_W = 1536
X_HEADS = 4
X_HEAD_DIM = 128
X_W = 512
GLA_HEADS = 4
GLA_DV = 384
GLA_DK = 192
GLA_KW = 768
GLA_RANK = 16
GLA_TAU = 16.0
CONV_W = 1536
CONV_K = 3
D_FF = 5632
EPS = 1e-6
LOG2E = 1.4426950408889634

LANE = 128
SUBLANE = 8
BF16_ROWS = 16
MXU_DIM = 256
VMEM_LIMIT = 56 * 1024 * 1024
VMEM_LIMIT_FFN = 60 * 1024 * 1024
NORM_ROWS = 256
FFN_TF_F32 = 256
FFN_TF_BF16 = 512

GLA_RANKP = LANE
GLA_DKP = MXU_DIM
GLA_CHUNK = 64
GLA_MAIN_N = 2 * GLA_KW + 2 * TOK_W
GLA_SIDE_N = X_W + GLA_RANKP
CONV_NP = 3 * CONV_W + X_W


def _rmsnorm(x, g):
    ms = jnp.mean(x * x, axis=-1, keepdims=True)
    return x * lax.rsqrt(ms + EPS) * g


def _dot(a, b):
    return jnp.dot(a, b, preferred_element_type=F32)


def _dot_nt(a, b):
    return lax.dot_general(a, b, (((1,), (1,)), ((), ())), preferred_element_type=F32)


def _dot_tn(a, b):
    return lax.dot_general(a, b, (((0,), (0,)), ((), ())), preferred_element_type=F32)


def _stacked_spec(lead, block, index_map):
    nlead = len(lead)
    return pl.BlockSpec((None,) * nlead + block, lambda *g: tuple(lead) + tuple(index_map(*g)))


def _for_row_chunks(n_rows, body):
    def step(c, carry):
        body(pl.ds(pl.multiple_of(c * NORM_ROWS, NORM_ROWS), NORM_ROWS))
        return carry

    lax.fori_loop(0, n_rows // NORM_ROWS, step, 0, unroll=2)


def _row_rsqrt_ms(src_ref, rs_ref):
    def rows_rs(rows):
        v = src_ref[rows, :]
        ms = jnp.mean(v * v, axis=-1, keepdims=True)
        rs_ref[rows, :] = jnp.broadcast_to(lax.rsqrt(ms + EPS), (NORM_ROWS, LANE))

    _for_row_chunks(src_ref.shape[0], rows_rs)


def _cast_specs(jobs):
    in_specs = [_stacked_spec(lead, blk, imap) for _, lead, blk, imap in jobs]
    out_specs = [pl.BlockSpec(blk, imap) for _, _, blk, imap in jobs]
    out_shape = [jax.ShapeDtypeStruct(w.shape[len(lead):], BF16) for w, lead, _, _ in jobs]
    return in_specs, out_specs, out_shape, [w for w, _, _, _ in jobs]


def _split_cast_refs(refs, n_in, n_out, n_cast):
    ins, refs = refs[:n_in], refs[n_in:]
    cast_in, refs = refs[:n_cast], refs[n_cast:]
    outs, refs = refs[:n_out], refs[n_out:]
    cast_out, scratch = refs[:n_cast], refs[n_cast:]
    for src, dst in zip(cast_in, cast_out):
        dst[...] = src[...].astype(BF16)
    return ins, outs, scratch


def _ffn_kernel(*refs, n_cast):
    ins, (o_ref,), (h_ref, rs_ref) = _split_cast_refs(refs, 6, 1, n_cast)
    x_ref, pre_g_ref, wg_ref, wu_ref, wo_ref, post_g_ref = ins
    j = pl.program_id(1)
    tm = x_ref.shape[0]

    @pl.when(j == 0)
    def _():
        def rows_in(rows):
            h_ref[rows, :] = _rmsnorm(x_ref[rows, :], pre_g_ref[...]).astype(BF16)
            o_ref[rows, :] = jnp.zeros((NORM_ROWS, o_ref.shape[1]), F32)

        _for_row_chunks(tm, rows_in)

    h = h_ref[...]
    gate = _dot(h, wg_ref[...].astype(BF16))
    up = _dot(h, wu_ref[...].astype(BF16))
    act = (gate * jax.nn.sigmoid(gate) * up).astype(BF16)
    o_ref[...] += _dot(act, wo_ref[...].astype(BF16))

    @pl.when(j == pl.num_programs(1) - 1)
    def _():
        _row_rsqrt_ms(o_ref, rs_ref)
        half_g = 0.5 * post_g_ref[...]

        def rows_out(rows):
            o_ref[rows, :] = x_ref[rows, :] + o_ref[rows, :] * rs_ref[rows, 0:1] * half_g

        _for_row_chunks(tm, rows_out)


def _grid_cast_job(w, lead, n_i, n_j, rows_by_j=False):
    if rows_by_j:
        blk = (w.shape[-2] // n_j, w.shape[-1] // n_i)
        return (w, lead, blk, lambda i, j: (jnp.minimum(j, n_j - 1), i))
    blk = (w.shape[-2] // n_i, w.shape[-1] // n_j)
    return (w, lead, blk, lambda i, j: (i, jnp.minimum(j, n_j - 1)))


def _ffn(x, gains, g_lead, weights, w_lead, *, tf, tm=1024, cast_jobs=()):
    t, d = x.shape
    nt, nf = t // tm, D_FF // tf
    pre_g, post_g = gains
    w_in, w_out = weights
    c_in, c_out, c_shape, c_args = _cast_specs(cast_jobs)
    res = pl.pallas_call(
        functools.partial(_ffn_kernel, n_cast=len(cast_jobs)),
        grid=(nt, nf),
        in_specs=[
            pl.BlockSpec((tm, d), lambda i, j: (i, 0)),
            _stacked_spec(g_lead, (1, d), lambda i, j: (0, 0)),
            _stacked_spec(w_lead, (d, tf), lambda i, j: (0, j)),
            _stacked_spec(w_lead, (d, tf), lambda i, j: (0, j + nf)),
            _stacked_spec(w_lead, (tf, d), lambda i, j: (j, 0)),
            _stacked_spec(g_lead, (1, d), lambda i, j: (0, 0)),
        ] + c_in,
        out_specs=[pl.BlockSpec((tm, d), lambda i, j: (i, 0))] + c_out,
        out_shape=[jax.ShapeDtypeStruct((t, d), F32)] + c_shape,
        scratch_shapes=[pltpu.VMEM((tm, d), BF16), pltpu.VMEM((tm, LANE), F32)],
        compiler_params=pltpu.CompilerParams(
            dimension_semantics=("parallel", "arbitrary"), vmem_limit_bytes=VMEM_LIMIT_FFN),
        name="ffn",
    )(x, pre_g, w_in, w_in, w_out, post_g, *c_args)
    return res[0], tuple(res[1:])


def _norm_matmul_kernel(x_ref, g_ref, w_ref, o_ref, h_ref):
    @pl.when(pl.program_id(1) == 0)
    def _():
        h_ref[...] = _rmsnorm(x_ref[...], g_ref[...]).astype(BF16)

    o_ref[...] = _dot(h_ref[...], w_ref[...].astype(BF16)).astype(o_ref.dtype)


def _norm_matmul(x, g, g_lead, w, w_lead, *, tm, tn, name):
    t, d = x.shape
    n = w.shape[-1]
    return pl.pallas_call(
        _norm_matmul_kernel,
        grid=(t // tm, n // tn),
        in_specs=[
            pl.BlockSpec((tm, d), lambda i, j: (i, 0)),
            _stacked_spec(g_lead, (1, d), lambda i, j: (0, 0)),
            _stacked_spec(w_lead, (d, tn), lambda i, j: (0, j)),
        ],
        out_specs=pl.BlockSpec((tm, tn), lambda i, j: (i, j)),
        out_shape=jax.ShapeDtypeStruct((t, n), BF16),
        scratch_shapes=[pltpu.VMEM((tm, d), BF16)],
        compiler_params=pltpu.CompilerParams(
            dimension_semantics=("parallel", "arbitrary"), vmem_limit_bytes=VMEM_LIMIT),
        name=name,
    )(x, g, w)


def _gla_proj_kernel(x_ref, g_ref, wm_ref, ws_ref, om_ref, os_ref, h_ref, *, n_main):
    j = pl.program_id(1)

    @pl.when(j == 0)
    def _():
        h_ref[...] = _rmsnorm(x_ref[...], g_ref[...]).astype(BF16)

    @pl.when(j < n_main)
    def _():
        om_ref[...] = _dot(h_ref[...], wm_ref[...].astype(BF16)).astype(BF16)

    @pl.when(j == n_main)
    def _():
        os_ref[...] = _dot(h_ref[...], ws_ref[...].astype(BF16)).astype(BF16)


def _gla_proj(x, g, g_lead, w, w_lead, w_side, *, tm=1024, tn=768):
    t, d = x.shape
    n_main = GLA_MAIN_N // tn
    return pl.pallas_call(
        functools.partial(_gla_proj_kernel, n_main=n_main),
        grid=(t // tm, n_main + 1),
        in_specs=[
            pl.BlockSpec((tm, d), lambda i, j: (i, 0)),
            _stacked_spec(g_lead, (1, d), lambda i, j: (0, 0)),
            _stacked_spec(w_lead, (d, tn), lambda i, j: (0, jnp.minimum(j, n_main - 1))),
            pl.BlockSpec((d, GLA_SIDE_N), lambda i, j: (0, 0)),
        ],
        out_specs=[
            pl.BlockSpec((tm, tn), lambda i, j: (i, jnp.minimum(j, n_main - 1))),
            pl.BlockSpec((tm, GLA_SIDE_N), lambda i, j: (i, 0)),
        ],
        out_shape=[jax.ShapeDtypeStruct((t, GLA_MAIN_N), BF16),
                   jax.ShapeDtypeStruct((t, GLA_SIDE_N), BF16)],
        scratch_shapes=[pltpu.VMEM((tm, d), BF16)],
        compiler_params=pltpu.CompilerParams(
            dimension_semantics=("parallel", "arbitrary"), vmem_limit_bytes=VMEM_LIMIT),
        name="gla_proj",
    )(x, g, w, w_side)


def _shift_groups(a, n):
    return jnp.concatenate([a[n:], a[:n]], axis=0)


def _block_first_group(a, n):
    return jnp.concatenate([a[g - g % n:g - g % n + 1] for g in range(a.shape[0])], axis=0)


def _gla_kernel(*refs, n_cast):
    ins, (o_ref,), (st_ref,) = _split_cast_refs(refs, 6, 1, n_cast)
    qkv_ref, r_ref, gl_ref, w2_ref, b2_ref, hg_ref = ins
    c_len = GLA_CHUNK
    ng = c_len // SUBLANE
    nb = qkv_ref.shape[0]

    @pl.when(pl.program_id(0) == 0)
    def _():
        st_ref[...] = jnp.zeros_like(st_ref)

    r3 = lax.broadcasted_iota(jnp.int32, (ng, SUBLANE, GLA_DKP), 1)
    ti = lax.broadcasted_iota(jnp.int32, (c_len, c_len), 0)
    si = lax.broadcasted_iota(jnp.int32, (c_len, c_len), 1)
    txs = jnp.where(ti > si, ti ^ si, 0)
    to3 = lambda a: a.reshape(ng, SUBLANE, GLA_DKP)
    to2 = lambda a: a.reshape(c_len, GLA_DKP)

    streams = [divmod(s, GLA_HEADS) for s in range(nb * GLA_HEADS)]
    vcols = [pl.ds(hd * GLA_DV, GLA_DV) for _, hd in streams]
    each = lambda f, *lists: [f(*args) for args in zip(*lists)]

    z = [_dot(gl_ref[bi], w2_ref[hd]) + b2_ref[hd] for bi, hd in streams]
    qkv = [qkv_ref[bi, :, :2 * GLA_KW + GLA_DKP].astype(F32) for bi in range(nb)]
    lane = lax.broadcasted_iota(jnp.int32, (c_len, GLA_DKP), 1)
    head = lambda a, lo: to3(jnp.where(lane < GLA_DK, a[:, lo:lo + GLA_DKP], 0.0))
    q = [head(qkv[bi], hd * GLA_DK) * (GLA_DK ** -0.5) for bi, hd in streams]
    k = [head(qkv[bi], GLA_KW + hd * GLA_DK) for bi, hd in streams]

    def log2_decay(zs):
        soft = jnp.log2(1.0 + jnp.exp2(jnp.abs(zs) * -LOG2E))
        return to3((jnp.minimum(zs, 0.0) * LOG2E - soft) * (1.0 / GLA_TAU))

    cum = each(log2_decay, z)
    d = 1
    while d < SUBLANE:
        cum = each(lambda a: a + jnp.where(r3 >= d, pltpu.roll(a, d, axis=1), 0.0), cum)
        d *= 2
    tot = each(lambda a: a[:, SUBLANE - 1:SUBLANE, :], cum)
    offs = [[jnp.zeros_like(t[0:1])] for t in tot]
    for g in range(1, ng):
        for o_s, t in zip(offs, tot):
            o_s.append(o_s[-1] + t[g - 1:g])
    cum = each(lambda a, o_s: a + jnp.concatenate(o_s, axis=0), cum, offs)

    nt = lambda a, b: _dot_nt(to2(a).astype(BF16), to2(b).astype(BF16))
    attn = each(lambda a, b: jnp.where(ti == si, nt(a, b), 0.0), q, k)
    g_h = cum
    h = 1
    while h < c_len:
        if h < SUBLANE:
            if h > 1:
                g_h = each(lambda a: jnp.where((r3 & (h // 2)) != 0,
                                               pltpu.roll(a, h // 2, axis=1), a), g_h)
            g_next = each(lambda a: pltpu.roll(a, SUBLANE - h, axis=1), g_h)
        else:
            n = h // SUBLANE
            g_h = each(lambda a: jnp.broadcast_to(a[:, 0:1, :], a.shape), cum)
            if n > 1:
                g_h = each(lambda a: _block_first_group(a, n), g_h)
            g_next = each(lambda a: _shift_groups(a, n), g_h)
        q_h = q if h == 1 else each(lambda a, c, g: a * jnp.exp2(c - g), q, cum, g_h)
        k_h = each(lambda a, c, g: a * jnp.exp2(g - c), k, cum, g_next)
        a_h = each(nt, q_h, k_h)
        attn = each(lambda new, old: jnp.where((txs >= h) & (txs < 2 * h), new, old), a_h, attn)
        h *= 2

    last = each(lambda a: a[ng - 1, SUBLANE - 1:SUBLANE, :], cum)
    q_in = each(lambda a, c: to2(a * jnp.exp2(c)).astype(BF16), q, cum)
    k_out = each(lambda a, c, l: to2(a * jnp.exp2(l - c)).astype(BF16), k, cum, last)
    v = [qkv_ref[bi, :, pl.ds(2 * GLA_KW + hd * GLA_DV, GLA_DV)] for bi, hd in streams]
    st = [st_ref[s] for s in range(len(streams))]
    o = each(lambda qi, s_, a, v_: _dot_nt(qi, s_.astype(BF16)) + _dot(a.astype(BF16), v_),
             q_in, st, attn, v)
    for s, (s_, l, v_, ko) in enumerate(zip(st, last, v, k_out)):
        st_ref[s] = s_ * jnp.exp2(l) + _dot_tn(v_, ko)

    o = each(lambda a: _rmsnorm(a, hg_ref[...]), o)
    for (bi, _), c, a in zip(streams, vcols, o):
        r = r_ref[bi, :, c].astype(F32)
        o_ref[bi, :, c] = (a * (r * jax.nn.sigmoid(r))).astype(BF16)


def _gla(p_main, p_side, w2, b2, head_g, cast_weights, *, batch, seq):
    t = p_main.shape[0]
    p_main = p_main.reshape(batch, seq, GLA_MAIN_N)
    p_side = p_side.reshape(batch, seq, GLA_SIDE_N)
    c_len = GLA_CHUNK
    steps = seq // c_len
    assert 2 * GLA_KW == TOK_W
    split = 8
    assert split * split == steps
    by_step = lambda i: (i // split, i % split)
    jobs = [(w, lead, (w.shape[-2] // split, w.shape[-1] // split), by_step)
            for w, lead in cast_weights]
    c_in, c_out, c_shape, c_args = _cast_specs(jobs)
    res = pl.pallas_call(
        functools.partial(_gla_kernel, n_cast=len(jobs)),
        grid=(steps,),
        in_specs=[
            pl.BlockSpec((batch, c_len, 2 * TOK_W), lambda i: (0, i, 0)),
            pl.BlockSpec((batch, c_len, TOK_W), lambda i: (0, i, 2)),
            pl.BlockSpec((batch, c_len, GLA_RANKP), lambda i: (0, i, 0)),
            pl.BlockSpec((GLA_HEADS, GLA_RANKP, GLA_DKP), lambda i: (0, 0, 0)),
            pl.BlockSpec((GLA_HEADS, 1, GLA_DKP), lambda i: (0, 0, 0)),
            pl.BlockSpec((1, GLA_DV), lambda i: (0, 0)),
        ] + c_in,
        out_specs=[pl.BlockSpec((batch, c_len, TOK_W), lambda i: (0, i, 0))] + c_out,
        out_shape=[jax.ShapeDtypeStruct((batch, seq, TOK_W), BF16)] + c_shape,
        scratch_shapes=[pltpu.VMEM((batch * GLA_HEADS, GLA_DV, GLA_DKP), F32)],
        compiler_params=pltpu.CompilerParams(
            dimension_semantics=("arbitrary",), vmem_limit_bytes=VMEM_LIMIT),
        name="gla",
    )(p_main, p_main, p_side, w2, b2, head_g, *c_args)
    return res[0].reshape(t, TOK_W), tuple(res[1:])


def _conv_mixer_kernel(x_ref, g_ref, wb_ref, wc_ref, wx_ref, cw_ref, cb_ref, tok_ref, xq_ref,
                       h_ref, carry_ref, *, n_conv, tiles_per_seq):
    i = pl.program_id(0)
    j = pl.program_id(1)
    tm = x_ref.shape[0]

    @pl.when(j == 0)
    def _():
        h_ref[...] = _rmsnorm(x_ref[...], g_ref[...]).astype(BF16)

    @pl.when((j < n_conv) & (i % tiles_per_seq == 0))
    def _():
        carry_ref[j] = jnp.zeros(carry_ref.shape[1:], F32)

    @pl.when(j < n_conv)
    def _():
        h = h_ref[...]
        u = _dot(h, wc_ref[...]) * _dot(h, wx_ref[...])
        u_prev = carry_ref[j]
        carry_ref[j] = u[tm - SUBLANE:, :]
        row = lax.broadcasted_iota(jnp.int32, u.shape, 0)
        u1 = jnp.where(row == 0, u_prev[SUBLANE - 1:SUBLANE, :], pltpu.roll(u, 1, axis=0))
        u2 = pltpu.roll(u, 2, axis=0)
        u2 = jnp.where(row == 0, u_prev[SUBLANE - 2:SUBLANE - 1, :], u2)
        u2 = jnp.where(row == 1, u_prev[SUBLANE - 1:SUBLANE, :], u2)
        w = cw_ref[...]
        y = u2 * w[0:1, :] + u1 * w[1:2, :] + u * w[2:3, :] + cb_ref[...]
        tok_ref[...] = (_dot(h, wb_ref[...]) * y).astype(BF16)

    @pl.when(j == n_conv)
    def _():
        xq_ref[...] = _dot(h_ref[...], wb_ref[...]).astype(BF16)


def _conv_mixer(x, g, g_lead, w, conv_w, conv_b, lead, *, seq, tm=1024, tw=512):
    t, d = x.shape
    assert tw == X_W
    n_conv = CONV_W // tw
    col = lambda j: jnp.minimum(j, n_conv - 1)
    return pl.pallas_call(
        functools.partial(_conv_mixer_kernel, n_conv=n_conv, tiles_per_seq=seq // tm),
        grid=(t // tm, n_conv + 1),
        in_specs=[
            pl.BlockSpec((tm, d), lambda i, j: (i, 0)),
            _stacked_spec(g_lead, (1, d), lambda i, j: (0, 0)),
            pl.BlockSpec((d, tw), lambda i, j: (0, jnp.where(j < n_conv, j, 3 * n_conv))),
            pl.BlockSpec((d, tw), lambda i, j: (0, n_conv + col(j))),
            pl.BlockSpec((d, tw), lambda i, j: (0, 2 * n_conv + col(j))),
            _stacked_spec(lead, (CONV_K, tw), lambda i, j: (0, col(j))),
            _stacked_spec(lead, (1, tw), lambda i, j: (0, col(j))),
        ],
        out_specs=[
            pl.BlockSpec((tm, tw), lambda i, j: (i, col(j))),
            pl.BlockSpec((tm, tw), lambda i, j: (i, 0)),
        ],
        out_shape=[jax.ShapeDtypeStruct((t, CONV_W), BF16), jax.ShapeDtypeStruct((t, X_W), BF16)],
        scratch_shapes=[pltpu.VMEM((tm, d), BF16), pltpu.VMEM((n_conv, SUBLANE, tw), F32)],
        compiler_params=pltpu.CompilerParams(
            dimension_semantics=("arbitrary", "arbitrary"), vmem_limit_bytes=VMEM_LIMIT),
        name="conv_mixer",
    )(x, g, w, w, w, conv_w, conv_b)


def _mixout_kernel(tok_ref, xq_ref, kv_ref, wm_ref, x_ref, g_ref, o_ref, rs_ref, *, q_off):
    mixed = _dot(tok_ref[...], wm_ref[:TOK_W, :])
    xo = []
    for h in range(X_HEADS):
        lo = h * X_HEAD_DIM
        kh = kv_ref[:, lo:lo + X_HEAD_DIM]
        vh = kv_ref[:, X_W + lo:X_W + lo + X_HEAD_DIM]
        if q_off == 0:
            qh = xq_ref[:, lo:lo + X_HEAD_DIM]
        else:
            qh = xq_ref[:, lo:lo + 2 * X_HEAD_DIM]
            kh = jnp.concatenate([kh.astype(F32), jnp.zeros(kh.shape, F32)], axis=1)
            kh = pltpu.roll(kh, q_off, axis=1).astype(BF16)
        s = _dot_nt(qh, kh) * (X_HEAD_DIM ** -0.5)
        e = jnp.exp(s - jnp.max(s, axis=-1, keepdims=True))
        pr = (e / jnp.sum(e, axis=-1, keepdims=True)).astype(BF16)
        xo.append(_dot(pr, vh).astype(BF16))
    o_ref[...] = mixed + _dot(jnp.concatenate(xo, axis=1), wm_ref[TOK_W:, :])

    _row_rsqrt_ms(o_ref, rs_ref)

    def rows_out(rows):
        o_ref[rows, :] = x_ref[rows, :] + o_ref[rows, :] * rs_ref[rows, 0:1] * g_ref[...]

    _for_row_chunks(o_ref.shape[0], rows_out)


def _mixout(tok, p, xq_block, q_off, kv, wm, x, g, lead, *, seq, tm=512):
    t, d = x.shape
    per_seq = seq // tm
    xq_w = X_W if q_off == 0 else X_W + LANE
    return pl.pallas_call(
        functools.partial(_mixout_kernel, q_off=q_off),
        grid=(t // tm,),
        in_specs=[
            pl.BlockSpec((tm, TOK_W), lambda i: (i, 0)),
            pl.BlockSpec((tm, xq_w), lambda i: (i, xq_block)),
            pl.BlockSpec((MEM_LEN, 2 * X_W), lambda i: (i // per_seq, 0)),
            pl.BlockSpec((d, d), lambda i: (0, 0)),
            pl.BlockSpec((tm, d), lambda i: (i, 0)),
            _stacked_spec(lead, (1, d), lambda i: (0, 0)),
        ],
        out_specs=pl.BlockSpec((tm, d), lambda i: (i, 0)),
        out_shape=jax.ShapeDtypeStruct((t, d), F32),
        scratch_shapes=[pltpu.VMEM((tm, LANE), F32)],
        compiler_params=pltpu.CompilerParams(
            dimension_semantics=("parallel",), vmem_limit_bytes=VMEM_LIMIT),
        name="mixout",
    )(tok, p, kv, wm, x, g)


def _gla_side_weight_kernel(w_ref, o_ref):
    tail = w_ref[:, GLA_MAIN_N:]
    pad = jnp.zeros((tail.shape[0], GLA_SIDE_N - tail.shape[1]), F32)
    o_ref[...] = jnp.concatenate([tail, pad], axis=1).astype(BF16)


def _gla_side_weight(w, lead, *, rows=256):
    d, n = w.shape[-2:]
    return pl.pallas_call(
        _gla_side_weight_kernel,
        grid=(d // rows,),
        in_specs=[_stacked_spec(lead, (rows, n), lambda i: (i, 0))],
        out_specs=pl.BlockSpec((rows, GLA_SIDE_N), lambda i: (i, 0)),
        out_shape=jax.ShapeDtypeStruct((d, GLA_SIDE_N), BF16),
        compiler_params=pltpu.CompilerParams(
            dimension_semantics=("parallel",), vmem_limit_bytes=VMEM_LIMIT),
        name="gla_side_weight",
    )(w)


def _gla_gate_weights(gate_w2, gate_b):
    w2 = gate_w2.reshape(GLA_RANK, GLA_HEADS, GLA_DK).transpose(1, 0, 2)
    w2 = jnp.pad(w2, ((0, 0), (0, GLA_RANKP - GLA_RANK), (0, GLA_DKP - GLA_DK))).astype(BF16)
    b2 = jnp.pad(gate_b.reshape(GLA_HEADS, 1, GLA_DK), ((0, 0), (0, 0), (0, GLA_DKP - GLA_DK)))
    return w2, b2


def kernel(x, mem, ffn_pre_g, ffn_w_in, ffn_w_out, ffn_post_g, mix_pre_g, mix_post_g, mem_g,
           w_mem_kv, w_mix_out, gla_w_in, gla_gate_w2, gla_gate_b, gla_head_g,
           conv_w_in, conv_w, conv_b):
    batch, seq, d = x.shape
    x = x.reshape(batch * seq, d)
    mem2 = mem.reshape(batch * MEM_LEN, d)
    ffn_pre_g = ffn_pre_g[:, :, None, :]
    ffn_post_g = ffn_post_g[:, :, None, :]
    mix_pre_g = mix_pre_g[:, None, :]
    mix_post_g = mix_post_g[:, None, :]
    mem_g = mem_g[:, None, :]
    conv_b = conv_b[:, None, :]

    ffn_gains = (ffn_pre_g, ffn_post_g)
    ffn_stacks = (ffn_w_in, ffn_w_out)
    assert DEPTH == 2
    nt = batch * seq // 1024
    nf = D_FF // FFN_TF_BF16

    def mem_kv(i):
        return _norm_matmul(mem2, mem_g, (i,), w_mem_kv, (i,),
                            tm=batch * MEM_LEN, tn=2 * X_W, name="mem_kv")

    def ffn_pair_jobs(lead):
        return [_grid_cast_job(ffn_w_in, lead, nt, nf),
                _grid_cast_job(ffn_w_out, lead, nt, nf, rows_by_j=True)]


    x, _ = _ffn(x, ffn_gains, (0, 0), ffn_stacks, (0, 0), tf=FFN_TF_F32)
    w2, b2 = _gla_gate_weights(gla_gate_w2[0], gla_gate_b[0])
    p, p_side = _gla_proj(x, mix_pre_g, (0,), gla_w_in, (0,), _gla_side_weight(gla_w_in, (0,)))
    tok, (w01_in, w01_out, wm0, wconv) = _gla(
        p, p_side, w2, b2, gla_head_g[0][None],
        [(ffn_w_in, (0, 1)), (ffn_w_out, (0, 1)), (w_mix_out, (0,)), (conv_w_in, (0,))],
        batch=batch, seq=seq)
    x = _mixout(tok, p_side, 0, GLA_RANK, mem_kv(0), wm0, x, mix_post_g, (0,), seq=seq)
    x, (w10_in, w10_out, wm1) = _ffn(
        x, ffn_gains, (0, 1), (w01_in, w01_out), (), tf=FFN_TF_BF16,
        cast_jobs=ffn_pair_jobs((1, 0)) + [_grid_cast_job(w_mix_out, (1,), nt, nt)])

    x, (w11_in, w11_out) = _ffn(x, ffn_gains, (1, 0), (w10_in, w10_out), (), tf=FFN_TF_BF16,
                                cast_jobs=ffn_pair_jobs((1, 1)))
    tok, xq = _conv_mixer(x, mix_pre_g, (1,), wconv, conv_w, conv_b, (0,), seq=seq)
    x = _mixout(tok, xq, 0, 0, mem_kv(1), wm1, x, mix_post_g, (1,), seq=seq)
    x, _ = _ffn(x, ffn_gains, (1, 1), (w11_in, w11_out), (), tf=FFN_TF_BF16)
    return x.reshape(batch, seq, d)
```

```python
import functools

import jax
import jax.numpy as jnp
from jax import lax
from jax.experimental import pallas as pl
from jax.experimental.pallas import tpu as pltpu

F32 = jnp.float32
BF16 = jnp.bfloat16

D_MODEL = 2048
DEPTH = 2
MEM_LEN = 256
TOK_W = 1536
X_HEADS = 4
X_HEAD_DIM = 128
X_W = 512
GLA_HEADS = 4
GLA_DV = 384
GLA_DK = 192
GLA_KW = 768
GLA_RANK = 16
GLA_TAU = 16.0
CONV_W = 1536
CONV_K = 3
D_FF = 5632
EPS = 1e-6
LOG2E = 1.4426950408889634

LANE = 128
SUBLANE = 8
MXU_DIM = 256
VMEM_LIMIT = 56 * 1024 * 1024
VMEM_LIMIT_FFN = 60 * 1024 * 1024
NORM_ROWS = 256
FFN_TF_F32 = 256
FFN_TF_BF16 = 512

GLA_RANKP = LANE
GLA_DKP = MXU_DIM
GLA_CHUNK = 64
GLA_MAIN_N = 2 * GLA_KW + 2 * TOK_W
GLA_SIDE_N = X_W + GLA_RANKP
CONV_NP = 3 * CONV_W + X_W


def _rmsnorm(x, g):
    ms = jnp.mean(x * x, axis=-1, keepdims=True)
    return x * lax.rsqrt(ms + EPS) * g


def _dot(a, b):
    return jnp.dot(a, b, preferred_element_type=F32)


def _dot_nt(a, b):
    return lax.dot_general(a, b, (((1,), (1,)), ((), ())), preferred_element_type=F32)


def _dot_tn(a, b):
    return lax.dot_general(a, b, (((0,), (0,)), ((), ())), preferred_element_type=F32)


def _stacked_spec(lead, block, index_map):
    nlead = len(lead)
    return pl.BlockSpec((None,) * nlead + block, lambda *g: tuple(lead) + tuple(index_map(*g)))


def _for_row_chunks(n_rows, body):
    def step(c, carry):
        body(pl.ds(pl.multiple_of(c * NORM_ROWS, NORM_ROWS), NORM_ROWS))
        return carry

    lax.fori_loop(0, n_rows // NORM_ROWS, step, 0, unroll=2)


def _row_rsqrt_ms(src_ref, rs_ref):
    def rows_rs(rows):
        v = src_ref[rows, :]
        ms = jnp.mean(v * v, axis=-1, keepdims=True)
        rs_ref[rows, :] = jnp.broadcast_to(lax.rsqrt(ms + EPS), (NORM_ROWS, LANE))

    _for_row_chunks(src_ref.shape[0], rows_rs)


def _cast_specs(jobs):
    in_specs = [_stacked_spec(lead, blk, imap) for _, lead, blk, imap in jobs]
    out_specs = [pl.BlockSpec(blk, imap) for _, _, blk, imap in jobs]
    out_shape = [jax.ShapeDtypeStruct(w.shape[len(lead):], BF16) for w, lead, _, _ in jobs]
    return in_specs, out_specs, out_shape, [w for w, _, _, _ in jobs]


def _split_cast_refs(refs, n_in, n_out, n_cast):
    ins, refs = refs[:n_in], refs[n_in:]
    cast_in, refs = refs[:n_cast], refs[n_cast:]
    outs, refs = refs[:n_out], refs[n_out:]
    cast_out, scratch = refs[:n_cast], refs[n_cast:]
    for src, dst in zip(cast_in, cast_out):
        dst[...] = src[...].astype(BF16)
    return ins, outs, scratch


def _ffn_kernel(*refs, n_cast):
    ins, (o_ref,), (h_ref, rs_ref) = _split_cast_refs(refs, 6, 1, n_cast)
    x_ref, pre_g_ref, wg_ref, wu_ref, wo_ref, post_g_ref = ins
    j = pl.program_id(1)
    tm = x_ref.shape[0]

    @pl.when(j == 0)
    def _():
        def rows_in(rows):
            h_ref[rows, :] = _rmsnorm(x_ref[rows, :], pre_g_ref[...]).astype(BF16)
            o_ref[rows, :] = jnp.zeros((NORM_ROWS, o_ref.shape[1]), F32)

        _for_row_chunks(tm, rows_in)

    h = h_ref[...]
    gate = _dot(h, wg_ref[...].astype(BF16))
    up = _dot(h, wu_ref[...].astype(BF16))
    act = (gate * jax.nn.sigmoid(gate) * up).astype(BF16)
    o_ref[...] += _dot(act, wo_ref[...].astype(BF16))

    @pl.when(j == pl.num_programs(1) - 1)
    def _():
        _row_rsqrt_ms(o_ref, rs_ref)
        half_g = 0.5 * post_g_ref[...]

        def rows_out(rows):
            o_ref[rows, :] = x_ref[rows, :] + o_ref[rows, :] * rs_ref[rows, 0:1] * half_g

        _for_row_chunks(tm, rows_out)


def _grid_cast_job(w, lead, n_i, n_j, rows_by_j=False):
    if rows_by_j:
        blk = (w.shape[-2] // n_j, w.shape[-1] // n_i)
        return (w, lead, blk, lambda i, j: (jnp.minimum(j, n_j - 1), i))
    blk = (w.shape[-2] // n_i, w.shape[-1] // n_j)
    return (w, lead, blk, lambda i, j: (i, jnp.minimum(j, n_j - 1)))


def _ffn(x, gains, g_lead, weights, w_lead, *, tf, tm=1024, cast_jobs=()):
    t, d = x.shape
    nt, nf = t // tm, D_FF // tf
    pre_g, post_g = gains
    w_in, w_out = weights
    c_in, c_out, c_shape, c_args = _cast_specs(cast_jobs)
    res = pl.pallas_call(
        functools.partial(_ffn_kernel, n_cast=len(cast_jobs)),
        grid=(nt, nf),
        in_specs=[
            pl.BlockSpec((tm, d), lambda i, j: (i, 0)),
            _stacked_spec(g_lead, (1, d), lambda i, j: (0, 0)),
            _stacked_spec(w_lead, (d, tf), lambda i, j: (0, j)),
            _stacked_spec(w_lead, (d, tf), lambda i, j: (0, j + nf)),
            _stacked_spec(w_lead, (tf, d), lambda i, j: (j, 0)),
            _stacked_spec(g_lead, (1, d), lambda i, j: (0, 0)),
        ] + c_in,
        out_specs=[pl.BlockSpec((tm, d), lambda i, j: (i, 0))] + c_out,
        out_shape=[jax.ShapeDtypeStruct((t, d), F32)] + c_shape,
        scratch_shapes=[pltpu.VMEM((tm, d), BF16), pltpu.VMEM((tm, LANE), F32)],
        compiler_params=pltpu.CompilerParams(
            dimension_semantics=("parallel", "arbitrary"), vmem_limit_bytes=VMEM_LIMIT_FFN),
        name="ffn",
    )(x, pre_g, w_in, w_in, w_out, post_g, *c_args)
    return res[0], tuple(res[1:])


def _norm_matmul_kernel(x_ref, g_ref, w_ref, o_ref, h_ref):
    @pl.when(pl.program_id(1) == 0)
    def _():
        h_ref[...] = _rmsnorm(x_ref[...], g_ref[...]).astype(BF16)

    o_ref[...] = _dot(h_ref[...], w_ref[...].astype(BF16)).astype(o_ref.dtype)


def _norm_matmul(x, g, g_lead, w, w_lead, *, tm, tn, name):
    t, d = x.shape
    n = w.shape[-1]
    return pl.pallas_call(
        _norm_matmul_kernel,
        grid=(t // tm, n // tn),
        in_specs=[
            pl.BlockSpec((tm, d), lambda i, j: (i, 0)),
            _stacked_spec(g_lead, (1, d), lambda i, j: (0, 0)),
            _stacked_spec(w_lead, (d, tn), lambda i, j: (0, j)),
        ],
        out_specs=pl.BlockSpec((tm, tn), lambda i, j: (i, j)),
        out_shape=jax.ShapeDtypeStruct((t, n), BF16),
        scratch_shapes=[pltpu.VMEM((tm, d), BF16)],
        compiler_params=pltpu.CompilerParams(
            dimension_semantics=("parallel", "arbitrary"), vmem_limit_bytes=VMEM_LIMIT),
        name=name,
    )(x, g, w)


def _gla_proj_kernel(x_ref, g_ref, wm_ref, ws_ref, om_ref, os_ref, h_ref, *, n_main):
    j = pl.program_id(1)

    @pl.when(j == 0)
    def _():
        h_ref[...] = _rmsnorm(x_ref[...], g_ref[...]).astype(BF16)

    @pl.when(j < n_main)
    def _():
        om_ref[...] = _dot_nt(h_ref[...], wm_ref[...].astype(BF16)).astype(BF16)

    @pl.when(j == n_main)
    def _():
        os_ref[...] = _dot_nt(h_ref[...], ws_ref[...].astype(BF16)).astype(BF16)


def _gla_proj(x, g, g_lead, wt, wt_side, *, tm=1024, tn=768):
    t, d = x.shape
    n_main = GLA_MAIN_N // tn
    return pl.pallas_call(
        functools.partial(_gla_proj_kernel, n_main=n_main),
        grid=(t // tm, n_main + 1),
        in_specs=[
            pl.BlockSpec((tm, d), lambda i, j: (i, 0)),
            _stacked_spec(g_lead, (1, d), lambda i, j: (0, 0)),
            pl.BlockSpec((tn, d), lambda i, j: (jnp.minimum(j, n_main - 1), 0)),
            pl.BlockSpec((GLA_SIDE_N, d), lambda i, j: (0, 0)),
        ],
        out_specs=[
            pl.BlockSpec((tm, tn), lambda i, j: (i, jnp.minimum(j, n_main - 1))),
            pl.BlockSpec((tm, GLA_SIDE_N), lambda i, j: (i, 0)),
        ],
        out_shape=[jax.ShapeDtypeStruct((t, GLA_MAIN_N), BF16),
                   jax.ShapeDtypeStruct((t, GLA_SIDE_N), BF16)],
        scratch_shapes=[pltpu.VMEM((tm, d), BF16)],
        compiler_params=pltpu.CompilerParams(
            dimension_semantics=("parallel", "arbitrary"), vmem_limit_bytes=VMEM_LIMIT),
        name="gla_proj",
    )(x, g, wt, wt_side)


def _shift_groups(a, n):
    return jnp.concatenate([a[n:], a[:n]], axis=0)


def _block_first_group(a, n):
    return jnp.concatenate([a[g - g % n:g - g % n + 1] for g in range(a.shape[0])], axis=0)


def _gla_kernel(*refs, n_cast):
    ins, (o_ref,), (st_ref,) = _split_cast_refs(refs, 6, 1, n_cast)
    qkv_ref, r_ref, gl_ref, w2_ref, b2_ref, hg_ref = ins
    c_len = GLA_CHUNK
    ng = c_len // SUBLANE
    nb = qkv_ref.shape[0]

    @pl.when(pl.program_id(0) == 0)
    def _():
        st_ref[...] = jnp.zeros_like(st_ref)

    r3 = lax.broadcasted_iota(jnp.int32, (ng, SUBLANE, GLA_DKP), 1)
    ti = lax.broadcasted_iota(jnp.int32, (c_len, c_len), 0)
    si = lax.broadcasted_iota(jnp.int32, (c_len, c_len), 1)
    txs = jnp.where(ti > si, ti ^ si, 0)
    to3 = lambda a: a.reshape(ng, SUBLANE, GLA_DKP)
    to2 = lambda a: a.reshape(c_len, GLA_DKP)

    streams = [divmod(s, GLA_HEADS) for s in range(nb * GLA_HEADS)]
    vcols = [pl.ds(hd * GLA_DV, GLA_DV) for _, hd in streams]
    each = lambda f, *lists: [f(*args) for args in zip(*lists)]

    z = [_dot(gl_ref[bi], w2_ref[hd]) + b2_ref[hd] for bi, hd in streams]
    qkv = [qkv_ref[bi, :, :2 * GLA_KW + GLA_DKP].astype(F32) for bi in range(nb)]
    lane = lax.broadcasted_iota(jnp.int32, (c_len, GLA_DKP), 1)
    head = lambda a, lo: to3(jnp.where(lane < GLA_DK, a[:, lo:lo + GLA_DKP], 0.0))
    q = [head(qkv[bi], hd * GLA_DK) * (GLA_DK ** -0.5) for bi, hd in streams]
    k = [head(qkv[bi], GLA_KW + hd * GLA_DK) for bi, hd in streams]

    def log2_decay(zs):
        soft = jnp.log2(1.0 + jnp.exp2(jnp.abs(zs) * -LOG2E))
        return to3((jnp.minimum(zs, 0.0) * LOG2E - soft) * (1.0 / GLA_TAU))

    cum = each(log2_decay, z)
    d = 1
    while d < SUBLANE:
        cum = each(lambda a: a + jnp.where(r3 >= d, pltpu.roll(a, d, axis=1), 0.0), cum)
        d *= 2
    tot = each(lambda a: a[:, SUBLANE - 1:SUBLANE, :], cum)
    offs = [[jnp.zeros_like(t[0:1])] for t in tot]
    for g in range(1, ng):
        for o_s, t in zip(offs, tot):
            o_s.append(o_s[-1] + t[g - 1:g])
    cum = each(lambda a, o_s: a + jnp.concatenate(o_s, axis=0), cum, offs)

    nt = lambda a, b: _dot_nt(to2(a).astype(BF16), to2(b).astype(BF16))
    attn = each(lambda a, b: jnp.where(ti == si, nt(a, b), 0.0), q, k)
    g_h = cum
    h = 1
    while h < c_len:
        if h < SUBLANE:
            if h > 1:
                g_h = each(lambda a: jnp.where((r3 & (h // 2)) != 0,
                                               pltpu.roll(a, h // 2, axis=1), a), g_h)
            g_next = each(lambda a: pltpu.roll(a, SUBLANE - h, axis=1), g_h)
        else:
            n = h // SUBLANE
            g_h = each(lambda a: jnp.broadcast_to(a[:, 0:1, :], a.shape), cum)
            if n > 1:
                g_h = each(lambda a: _block_first_group(a, n), g_h)
            g_next = each(lambda a: _shift_groups(a, n), g_h)
        q_h = q if h == 1 else each(lambda a, c, g: a * jnp.exp2(c - g), q, cum, g_h)
        k_h = each(lambda a, c, g: a * jnp.exp2(g - c), k, cum, g_next)
        a_h = each(nt, q_h, k_h)
        attn = each(lambda new, old: jnp.where((txs >= h) & (txs < 2 * h), new, old), a_h, attn)
        h *= 2

    last = each(lambda a: a[ng - 1, SUBLANE - 1:SUBLANE, :], cum)
    q_in = each(lambda a, c: to2(a * jnp.exp2(c)).astype(BF16), q, cum)
    k_out = each(lambda a, c, l: to2(a * jnp.exp2(l - c)).astype(BF16), k, cum, last)
    v = [qkv_ref[bi, :, pl.ds(2 * GLA_KW + hd * GLA_DV, GLA_DV)] for bi, hd in streams]
    st = [st_ref[s] for s in range(len(streams))]
    o = each(lambda qi, s_, a, v_: _dot_nt(qi, s_.astype(BF16)) + _dot(a.astype(BF16), v_),
             q_in, st, attn, v)
    for s, (s_, l, v_, ko) in enumerate(zip(st, last, v, k_out)):
        st_ref[s] = s_ * jnp.exp2(l) + _dot_tn(v_, ko)

    o = each(lambda a: _rmsnorm(a, hg_ref[...]), o)
    for (bi, _), c, a in zip(streams, vcols, o):
        r = r_ref[bi, :, c].astype(F32)
        o_ref[bi, :, c] = (a * (r * jax.nn.sigmoid(r))).astype(BF16)


def _gla(p_main, p_side, w2, b2, head_g, cast_weights, *, batch, seq):
    t = p_main.shape[0]
    p_main = p_main.reshape(batch, seq, GLA_MAIN_N)
    p_side = p_side.reshape(batch, seq, GLA_SIDE_N)
    c_len = GLA_CHUNK
    steps = seq // c_len
    assert 2 * GLA_KW == TOK_W
    split = 8
    assert split * split == steps
    by_step = lambda i: (i // split, i % split)
    jobs = [(w, lead, (w.shape[-2] // split, w.shape[-1] // split), by_step)
            for w, lead in cast_weights]
    c_in, c_out, c_shape, c_args = _cast_specs(jobs)
    res = pl.pallas_call(
        functools.partial(_gla_kernel, n_cast=len(jobs)),
        grid=(steps,),
        in_specs=[
            pl.BlockSpec((batch, c_len, 2 * TOK_W), lambda i: (0, i, 0)),
            pl.BlockSpec((batch, c_len, TOK_W), lambda i: (0, i, 2)),
            pl.BlockSpec((batch, c_len, GLA_RANKP), lambda i: (0, i, X_W // GLA_RANKP)),
            pl.BlockSpec((GLA_HEADS, GLA_RANKP, GLA_DKP), lambda i: (0, 0, 0)),
            pl.BlockSpec((GLA_HEADS, 1, GLA_DKP), lambda i: (0, 0, 0)),
            pl.BlockSpec((1, GLA_DV), lambda i: (0, 0)),
        ] + c_in,
        out_specs=[pl.BlockSpec((batch, c_len, TOK_W), lambda i: (0, i, 0))] + c_out,
        out_shape=[jax.ShapeDtypeStruct((batch, seq, TOK_W), BF16)] + c_shape,
        scratch_shapes=[pltpu.VMEM((batch * GLA_HEADS, GLA_DV, GLA_DKP), F32)],
        compiler_params=pltpu.CompilerParams(
            dimension_semantics=("arbitrary",), vmem_limit_bytes=VMEM_LIMIT),
        name="gla",
    )(p_main, p_main, p_side, w2, b2, head_g, *c_args)
    return res[0].reshape(t, TOK_W), tuple(res[1:])


def _conv_mixer_kernel(x_ref, g_ref, wb_ref, wc_ref, wx_ref, cw_ref, cb_ref, tok_ref, xq_ref,
                       h_ref, carry_ref, *, n_conv, tiles_per_seq):
    i = pl.program_id(0)
    j = pl.program_id(1)
    tm = x_ref.shape[0]

    @pl.when(j == 0)
    def _():
        h_ref[...] = _rmsnorm(x_ref[...], g_ref[...]).astype(BF16)

    @pl.when((j < n_conv) & (i % tiles_per_seq == 0))
    def _():
        carry_ref[j] = jnp.zeros(carry_ref.shape[1:], F32)

    @pl.when(j < n_conv)
    def _():
        h = h_ref[...]
        u = _dot(h, wc_ref[...]) * _dot(h, wx_ref[...])
        u_prev = carry_ref[j]
        carry_ref[j] = u[tm - SUBLANE:, :]
        row = lax.broadcasted_iota(jnp.int32, u.shape, 0)
        u1 = jnp.where(row == 0, u_prev[SUBLANE - 1:SUBLANE, :], pltpu.roll(u, 1, axis=0))
        u2 = pltpu.roll(u, 2, axis=0)
        u2 = jnp.where(row == 0, u_prev[SUBLANE - 2:SUBLANE - 1, :], u2)
        u2 = jnp.where(row == 1, u_prev[SUBLANE - 1:SUBLANE, :], u2)
        w = cw_ref[...]
        y = u2 * w[0:1, :] + u1 * w[1:2, :] + u * w[2:3, :] + cb_ref[...]
        tok_ref[...] = (_dot(h, wb_ref[...]) * y).astype(BF16)

    @pl.when(j == n_conv)
    def _():
        xq_ref[...] = _dot(h_ref[...], wb_ref[...]).astype(BF16)


def _conv_mixer(x, g, g_lead, w, conv_w, conv_b, lead, *, seq, tm=1024, tw=512):
    t, d = x.shape
    assert tw == X_W
    n_conv = CONV_W // tw
    col = lambda j: jnp.minimum(j, n_conv - 1)
    return pl.pallas_call(
        functools.partial(_conv_mixer_kernel, n_conv=n_conv, tiles_per_seq=seq // tm),
        grid=(t // tm, n_conv + 1),
        in_specs=[
            pl.BlockSpec((tm, d), lambda i, j: (i, 0)),
            _stacked_spec(g_lead, (1, d), lambda i, j: (0, 0)),
            pl.BlockSpec((d, tw), lambda i, j: (0, jnp.where(j < n_conv, j, 3 * n_conv))),
            pl.BlockSpec((d, tw), lambda i, j: (0, n_conv + col(j))),
            pl.BlockSpec((d, tw), lambda i, j: (0, 2 * n_conv + col(j))),
            _stacked_spec(lead, (CONV_K, tw), lambda i, j: (0, col(j))),
            _stacked_spec(lead, (1, tw), lambda i, j: (0, col(j))),
        ],
        out_specs=[
            pl.BlockSpec((tm, tw), lambda i, j: (i, col(j))),
            pl.BlockSpec((tm, tw), lambda i, j: (i, 0)),
        ],
        out_shape=[jax.ShapeDtypeStruct((t, CONV_W), BF16), jax.ShapeDtypeStruct((t, X_W), BF16)],
        scratch_shapes=[pltpu.VMEM((tm, d), BF16), pltpu.VMEM((n_conv, SUBLANE, tw), F32)],
        compiler_params=pltpu.CompilerParams(
            dimension_semantics=("arbitrary", "arbitrary"), vmem_limit_bytes=VMEM_LIMIT),
        name="conv_mixer",
    )(x, g, w, w, w, conv_w, conv_b)


def _mixout_kernel(tok_ref, xq_ref, kv_ref, wm_ref, x_ref, g_ref, o_ref):
    mixed = _dot(tok_ref[...], wm_ref[:TOK_W, :])
    xo = []
    for h in range(X_HEADS):
        lo = h * X_HEAD_DIM
        qh = xq_ref[:, lo:lo + X_HEAD_DIM]
        kh = kv_ref[:, lo:lo + X_HEAD_DIM]
        vh = kv_ref[:, X_W + lo:X_W + lo + X_HEAD_DIM]
        s = _dot_nt(qh, kh) * (X_HEAD_DIM ** -0.5)
        e = jnp.exp(s - jnp.max(s, axis=-1, keepdims=True))
        pr = (e / jnp.sum(e, axis=-1, keepdims=True)).astype(BF16)
        xo.append(_dot(pr, vh).astype(BF16))
    mixed = mixed + _dot(jnp.concatenate(xo, axis=1), wm_ref[TOK_W:, :])
    o_ref[...] = x_ref[...] + _rmsnorm(mixed, g_ref[...])


def _mixout(tok, p, xq_block, kv, wm, x, g, lead, *, seq, tm=512):
    t, d = x.shape
    per_seq = seq // tm
    return pl.pallas_call(
        _mixout_kernel,
        grid=(t // tm,),
        in_specs=[
            pl.BlockSpec((tm, TOK_W), lambda i: (i, 0)),
            pl.BlockSpec((tm, X_W), lambda i: (i, xq_block)),
            pl.BlockSpec((MEM_LEN, 2 * X_W), lambda i: (i // per_seq, 0)),
            pl.BlockSpec((d, d), lambda i: (0, 0)),
            pl.BlockSpec((tm, d), lambda i: (i, 0)),
            _stacked_spec(lead, (1, d), lambda i: (0, 0)),
        ],
        out_specs=pl.BlockSpec((tm, d), lambda i: (i, 0)),
        out_shape=jax.ShapeDtypeStruct((t, d), F32),
        compiler_params=pltpu.CompilerParams(
            dimension_semantics=("parallel",), vmem_limit_bytes=VMEM_LIMIT),
        name="mixout",
    )(tok, p, kv, wm, x, g)


def _gla_side_weight(wt):
    lo = GLA_MAIN_N
    pad = jnp.zeros((GLA_RANKP - GLA_RANK, wt.shape[1]), wt.dtype)
    return jnp.concatenate([wt[lo + GLA_RANK:], wt[lo:lo + GLA_RANK], pad], axis=0)


def _gla_gate_weights(gate_w2, gate_b):
    w2 = gate_w2.reshape(GLA_RANK, GLA_HEADS, GLA_DK).transpose(1, 0, 2)
    w2 = jnp.pad(w2, ((0, 0), (0, GLA_RANKP - GLA_RANK), (0, GLA_DKP - GLA_DK))).astype(BF16)
    b2 = jnp.pad(gate_b.reshape(GLA_HEADS, 1, GLA_DK), ((0, 0), (0, 0), (0, GLA_DKP - GLA_DK)))
    return w2, b2


def kernel(x, mem, ffn_pre_g, ffn_w_in, ffn_w_out, ffn_post_g, mix_pre_g, mix_post_g, mem_g,
           w_mem_kv, w_mix_out, gla_w_in, gla_gate_w2, gla_gate_b, gla_head_g,
           conv_w_in, conv_w, conv_b):
    batch, seq, d = x.shape
    x = x.reshape(batch * seq, d)
    mem2 = mem.reshape(batch * MEM_LEN, d)
    ffn_pre_g = ffn_pre_g[:, :, None, :]
    ffn_post_g = ffn_post_g[:, :, None, :]
    mix_pre_g = mix_pre_g[:, None, :]
    mix_post_g = mix_post_g[:, None, :]
    mem_g = mem_g[:, None, :]
    conv_b = conv_b[:, None, :]

    ffn_gains = (ffn_pre_g, ffn_post_g)
    ffn_stacks = (ffn_w_in, ffn_w_out)
    assert DEPTH == 2
    nt = batch * seq // 1024
    nf = D_FF // FFN_TF_BF16

    def mem_kv(i):
        return _norm_matmul(mem2, mem_g, (i,), w_mem_kv, (i,),
                            tm=batch * MEM_LEN, tn=2 * X_W, name="mem_kv")

    def ffn_pair_jobs(lead):
        return [_grid_cast_job(ffn_w_in, lead, nt, nf),
                _grid_cast_job(ffn_w_out, lead, nt, nf, rows_by_j=True)]


    x, _ = _ffn(x, ffn_gains, (0, 0), ffn_stacks, (0, 0), tf=FFN_TF_F32)
    w2, b2 = _gla_gate_weights(gla_gate_w2[0], gla_gate_b[0])
    wt = gla_w_in[0].T
    p, p_side = _gla_proj(x, mix_pre_g, (0,), wt, _gla_side_weight(wt))
    tok, (w01_in, w01_out, wm0, wconv) = _gla(
        p, p_side, w2, b2, gla_head_g[0][None],
        [(ffn_w_in, (0, 1)), (ffn_w_out, (0, 1)), (w_mix_out, (0,)), (conv_w_in, (0,))],
        batch=batch, seq=seq)
    x = _mixout(tok, p_side, 0, mem_kv(0), wm0, x, mix_post_g, (0,), seq=seq)
    x, (w10_in, w10_out, wm1) = _ffn(
        x, ffn_gains, (0, 1), (w01_in, w01_out), (), tf=FFN_TF_BF16,
        cast_jobs=ffn_pair_jobs((1, 0)) + [_grid_cast_job(w_mix_out, (1,), nt, nt)])

    x, (w11_in, w11_out) = _ffn(x, ffn_gains, (1, 0), (w10_in, w10_out), (), tf=FFN_TF_BF16,
                                cast_jobs=ffn_pair_jobs((1, 1)))
    tok, xq = _conv_mixer(x, mix_pre_g, (1,), wconv, conv_w, conv_b, (0,), seq=seq)
    x = _mixout(tok, xq, 0, mem_kv(1), wm1, x, mix_post_g, (1,), seq=seq)
    x, _ = _ffn(x, ffn_gains, (1, 1), (w11_in, w11_out), (), tf=FFN_TF_BF16)
    return x.reshape(batch, seq, d)
```

```python
import functools

import jax
import jax.numpy as jnp
from jax import lax
from jax.experimental import pallas as pl
from jax.experimental.pallas import tpu as pltpu

F32 = jnp.float32
BF16 = jnp.bfloat16

D_MODEL = 2048
DEPTH = 2
MEM_LEN = 256
TOK_W = 1536
X_HEADS = 4
X_HEAD_DIM = 128
X_W = 512
GLA_HEADS = 4
GLA_DV = 384
GLA_DK = 192
GLA_KW = 768
GLA_RANK = 16
GLA_TAU = 16.0
CONV_W = 1536
CONV_K = 3
D_FF = 5632
EPS = 1e-6
LOG2E = 1.4426950408889634

LANE = 128
SUBLANE = 8
MXU_DIM = 256
VMEM_LIMIT = 56 * 1024 * 1024
VMEM_LIMIT_FFN = 60 * 1024 * 1024
NORM_ROWS = 256
FFN_TF_F32 = 256
FFN_TF_BF16 = 512

GLA_RANKP = LANE
GLA_DKP = MXU_DIM
GLA_CHUNK = 64
GLA_MAIN_N = 2 * GLA_KW + 2 * TOK_W
GLA_SIDE_N = X_W + GLA_RANKP
CONV_NP = 3 * CONV_W + X_W


def _rmsnorm(x, g, eps=EPS):
    ms = jnp.mean(x * x, axis=-1, keepdims=True)
    return x * lax.rsqrt(ms + eps) * g


def _dot(a, b):
    return jnp.dot(a, b, preferred_element_type=F32)


def _dot_nt(a, b):
    return lax.dot_general(a, b, (((1,), (1,)), ((), ())), preferred_element_type=F32)


def _dot_tn(a, b):
    return lax.dot_general(a, b, (((0,), (0,)), ((), ())), preferred_element_type=F32)


def _stacked_spec(lead, block, index_map):
    nlead = len(lead)
    return pl.BlockSpec((None,) * nlead + block, lambda *g: tuple(lead) + tuple(index_map(*g)))


def _for_row_chunks(n_rows, body):
    def step(c, carry):
        body(pl.ds(pl.multiple_of(c * NORM_ROWS, NORM_ROWS), NORM_ROWS))
        return carry

    lax.fori_loop(0, n_rows // NORM_ROWS, step, 0, unroll=2)


def _row_rsqrt_ms(src_ref, rs_ref):
    def rows_rs(rows):
        v = src_ref[rows, :]
        ms = jnp.mean(v * v, axis=-1, keepdims=True)
        rs_ref[rows, :] = jnp.broadcast_to(lax.rsqrt(ms + EPS), (NORM_ROWS, LANE))

    _for_row_chunks(src_ref.shape[0], rows_rs)


def _cast_specs(jobs):
    in_specs = [_stacked_spec(lead, blk, imap) for _, lead, blk, imap in jobs]
    out_specs = [pl.BlockSpec(blk, imap) for _, _, blk, imap in jobs]
    out_shape = [jax.ShapeDtypeStruct(w.shape[len(lead):], BF16) for w, lead, _, _ in jobs]
    return in_specs, out_specs, out_shape, [w for w, _, _, _ in jobs]


def _split_cast_refs(refs, n_in, n_out, n_cast):
    ins, refs = refs[:n_in], refs[n_in:]
    cast_in, refs = refs[:n_cast], refs[n_cast:]
    outs, refs = refs[:n_out], refs[n_out:]
    cast_out, scratch = refs[:n_cast], refs[n_cast:]
    return ins, outs, scratch, list(zip(cast_in, cast_out))


def _run_casts(casts):
    for src, dst in casts:
        dst[...] = src[...].astype(BF16)


def _ffn_kernel(*refs, n_cast):
    ins, (o_ref,), (h_ref, rs_ref), casts = _split_cast_refs(refs, 6, 1, n_cast)
    x_ref, pre_g_ref, wg_ref, wu_ref, wo_ref, post_g_ref = ins
    j = pl.program_id(1)
    tm = x_ref.shape[0]

    @pl.when(j == 0)
    def _():
        def rows_in(rows):
            h_ref[rows, :] = _rmsnorm(x_ref[rows, :], pre_g_ref[...]).astype(BF16)
            o_ref[rows, :] = jnp.zeros((NORM_ROWS, o_ref.shape[1]), F32)

        _for_row_chunks(tm, rows_in)

    h = h_ref[...]
    gate = _dot(h, wg_ref[...].astype(BF16))
    up = _dot(h, wu_ref[...].astype(BF16))
    act = (gate * jax.nn.sigmoid(gate) * up).astype(BF16)
    _run_casts(casts)
    o_ref[...] += _dot(act, wo_ref[...].astype(BF16))

    @pl.when(j == pl.num_programs(1) - 1)
    def _():
        _row_rsqrt_ms(o_ref, rs_ref)
        half_g = 0.5 * post_g_ref[...]

        def rows_out(rows):
            o_ref[rows, :] = x_ref[rows, :] + o_ref[rows, :] * rs_ref[rows, 0:1] * half_g

        _for_row_chunks(tm, rows_out)


def _grid_cast_job(w, lead, n_i, n_j, rows_by_j=False):
    if rows_by_j:
        blk = (w.shape[-2] // n_j, w.shape[-1] // n_i)
        return (w, lead, blk, lambda i, j: (jnp.minimum(j, n_j - 1), i))
    blk = (w.shape[-2] // n_i, w.shape[-1] // n_j)
    return (w, lead, blk, lambda i, j: (i, jnp.minimum(j, n_j - 1)))


def _ffn(x, gains, g_lead, weights, w_lead, *, tf, tm=1024, cast_jobs=()):
    t, d = x.shape
    nt, nf = t // tm, D_FF // tf
    pre_g, post_g = gains
    w_in, w_out = weights
    c_in, c_out, c_shape, c_args = _cast_specs(cast_jobs)
    res = pl.pallas_call(
        functools.partial(_ffn_kernel, n_cast=len(cast_jobs)),
        grid=(nt, nf),
        in_specs=[
            pl.BlockSpec((tm, d), lambda i, j: (i, 0)),
            _stacked_spec(g_lead, (1, d), lambda i, j: (0, 0)),
            _stacked_spec(w_lead, (d, tf), lambda i, j: (0, j)),
            _stacked_spec(w_lead, (d, tf), lambda i, j: (0, j + nf)),
            _stacked_spec(w_lead, (tf, d), lambda i, j: (j, 0)),
            _stacked_spec(g_lead, (1, d), lambda i, j: (0, 0)),
        ] + c_in,
        out_specs=[pl.BlockSpec((tm, d), lambda i, j: (i, 0))] + c_out,
        out_shape=[jax.ShapeDtypeStruct((t, d), F32)] + c_shape,
        scratch_shapes=[pltpu.VMEM((tm, d), BF16), pltpu.VMEM((tm, LANE), F32)],
        compiler_params=pltpu.CompilerParams(
            dimension_semantics=("parallel", "arbitrary"), vmem_limit_bytes=VMEM_LIMIT_FFN),
        name="ffn",
    )(x, pre_g, w_in, w_in, w_out, post_g, *c_args)
    return res[0], tuple(res[1:])


def _norm_matmul_kernel(x_ref, g_ref, w_ref, o_ref, h_ref):
    @pl.when(pl.program_id(1) == 0)
    def _():
        h_ref[...] = _rmsnorm(x_ref[...], g_ref[...]).astype(BF16)

    o_ref[...] = _dot(h_ref[...], w_ref[...].astype(BF16)).astype(o_ref.dtype)


def _norm_matmul(x, g, g_lead, w, w_lead, *, tm, tn, name):
    t, d = x.shape
    n = w.shape[-1]
    return pl.pallas_call(
        _norm_matmul_kernel,
        grid=(t // tm, n // tn),
        in_specs=[
            pl.BlockSpec((tm, d), lambda i, j: (i, 0)),
            _stacked_spec(g_lead, (1, d), lambda i, j: (0, 0)),
            _stacked_spec(w_lead, (d, tn), lambda i, j: (0, j)),
        ],
        out_specs=pl.BlockSpec((tm, tn), lambda i, j: (i, j)),
        out_shape=jax.ShapeDtypeStruct((t, n), BF16),
        scratch_shapes=[pltpu.VMEM((tm, d), BF16)],
        compiler_params=pltpu.CompilerParams(
            dimension_semantics=("parallel", "arbitrary"), vmem_limit_bytes=VMEM_LIMIT),
        name=name,
    )(x, g, w)


def _gla_proj_kernel(x_ref, g_ref, wm_ref, ws_ref, om_ref, os_ref, h_ref, *, n_main):
    j = pl.program_id(1)

    @pl.when(j == 0)
    def _():
        h_ref[...] = _rmsnorm(x_ref[...], g_ref[...]).astype(BF16)

    @pl.when(j < n_main)
    def _():
        om_ref[...] = _dot_nt(h_ref[...], wm_ref[...].astype(BF16)).astype(BF16)

    @pl.when(j == n_main)
    def _():
        os_ref[...] = _dot_nt(h_ref[...], ws_ref[...].astype(BF16)).astype(BF16)


def _gla_proj(x, g, g_lead, wt, wt_side, *, tm=1024, tn=768):
    t, d = x.shape
    n_main = GLA_MAIN_N // tn
    return pl.pallas_call(
        functools.partial(_gla_proj_kernel, n_main=n_main),
        grid=(t // tm, n_main + 1),
        in_specs=[
            pl.BlockSpec((tm, d), lambda i, j: (i, 0)),
            _stacked_spec(g_lead, (1, d), lambda i, j: (0, 0)),
            pl.BlockSpec((tn, d), lambda i, j: (jnp.minimum(j, n_main - 1), 0)),
            pl.BlockSpec((GLA_SIDE_N, d), lambda i, j: (0, 0)),
        ],
        out_specs=[
            pl.BlockSpec((tm, tn), lambda i, j: (i, jnp.minimum(j, n_main - 1))),
            pl.BlockSpec((tm, GLA_SIDE_N), lambda i, j: (i, 0)),
        ],
        out_shape=[jax.ShapeDtypeStruct((t, GLA_MAIN_N), BF16),
                   jax.ShapeDtypeStruct((t, GLA_SIDE_N), BF16)],
        scratch_shapes=[pltpu.VMEM((tm, d), BF16)],
        compiler_params=pltpu.CompilerParams(
            dimension_semantics=("parallel", "arbitrary"), vmem_limit_bytes=VMEM_LIMIT),
        name="gla_proj",
    )(x, g, wt, wt_side)


def _shift_groups(a, n):
    return jnp.concatenate([a[n:], a[:n]], axis=0)


def _block_first_group(a, n):
    return jnp.concatenate([a[g - g % n:g - g % n + 1] for g in range(a.shape[0])], axis=0)


def _gla_kernel(*refs, n_cast):
    ins, (o_ref,), (st_ref,), casts = _split_cast_refs(refs, 6, 1, n_cast)
    qkv_ref, r_ref, gl_ref, w2_ref, b2_ref, hg_ref = ins
    c_len = GLA_CHUNK
    ng = c_len // SUBLANE
    nb = qkv_ref.shape[0]

    @pl.when(pl.program_id(0) == 0)
    def _():
        st_ref[...] = jnp.zeros_like(st_ref)

    r3 = lax.broadcasted_iota(jnp.int32, (ng, SUBLANE, GLA_DKP), 1)
    ti = lax.broadcasted_iota(jnp.int32, (c_len, c_len), 0)
    si = lax.broadcasted_iota(jnp.int32, (c_len, c_len), 1)
    txs = jnp.where(ti > si, ti ^ si, 0)
    to3 = lambda a: a.reshape(ng, SUBLANE, GLA_DKP)
    to2 = lambda a: a.reshape(c_len, GLA_DKP)

    streams = [divmod(s, GLA_HEADS) for s in range(nb * GLA_HEADS)]
    vcols = [pl.ds(hd * GLA_DV, GLA_DV) for _, hd in streams]
    each = lambda f, *lists: [f(*args) for args in zip(*lists)]

    z = [_dot(gl_ref[bi], w2_ref[hd]) + b2_ref[hd] for bi, hd in streams]
    _run_casts(casts)
    qkv = [qkv_ref[bi, :, :2 * GLA_KW + GLA_DKP].astype(F32) for bi in range(nb)]
    lane = lax.broadcasted_iota(jnp.int32, (c_len, GLA_DKP), 1)
    head = lambda a, lo: to3(jnp.where(lane < GLA_DK, a[:, lo:lo + GLA_DKP], 0.0))
    q = [head(qkv[bi], hd * GLA_DK) for bi, hd in streams]
    k = [head(qkv[bi], GLA_KW + hd * GLA_DK) for bi, hd in streams]

    def log2_decay(zs):
        soft = jnp.log2(1.0 + jnp.exp2(jnp.abs(zs) * -LOG2E))
        return to3((jnp.minimum(zs, 0.0) * LOG2E - soft) * (1.0 / GLA_TAU))

    cum = each(log2_decay, z)
    d = 1
    while d < SUBLANE:
        cum = each(lambda a: a + jnp.where(r3 >= d, pltpu.roll(a, d, axis=1), 0.0), cum)
        d *= 2
    tot = each(lambda a: a[:, SUBLANE - 1:SUBLANE, :], cum)
    offs = [[jnp.zeros_like(t[0:1])] for t in tot]
    for g in range(1, ng):
        for o_s, t in zip(offs, tot):
            o_s.append(o_s[-1] + t[g - 1:g])
    cum = each(lambda a, o_s: a + jnp.concatenate(o_s, axis=0), cum, offs)

    nt = lambda a, b: _dot_nt(to2(a).astype(BF16), to2(b).astype(BF16))
    attn = each(lambda a, b: jnp.where(ti == si, nt(a, b), 0.0), q, k)
    g_h = cum
    h = 1
    while h < c_len:
        if h < SUBLANE:
            if h > 1:
                g_h = each(lambda a: jnp.where((r3 & (h // 2)) != 0,
                                               pltpu.roll(a, h // 2, axis=1), a), g_h)
            g_next = each(lambda a: pltpu.roll(a, SUBLANE - h, axis=1), g_h)
        else:
            n = h // SUBLANE
            g_h = each(lambda a: jnp.broadcast_to(a[:, 0:1, :], a.shape), cum)
            if n > 1:
                g_h = each(lambda a: _block_first_group(a, n), g_h)
            g_next = each(lambda a: _shift_groups(a, n), g_h)
        q_h = q if h == 1 else each(lambda a, c, g: a * jnp.exp2(c - g), q, cum, g_h)
        k_h = each(lambda a, c, g: a * jnp.exp2(g - c), k, cum, g_next)
        a_h = each(nt, q_h, k_h)
        attn = each(lambda new, old: jnp.where((txs >= h) & (txs < 2 * h), new, old), a_h, attn)
        h *= 2

    last = each(lambda a: a[ng - 1, SUBLANE - 1:SUBLANE, :], cum)
    q_in = each(lambda a, c: to2(a * jnp.exp2(c)).astype(BF16), q, cum)
    k_out = each(lambda a, c, l: to2(a * jnp.exp2(l - c)).astype(BF16), k, cum, last)
    v = [qkv_ref[bi, :, pl.ds(2 * GLA_KW + hd * GLA_DV, GLA_DV)] for bi, hd in streams]
    st = [st_ref[s] for s in range(len(streams))]
    o = each(lambda qi, s_, a, v_: _dot_nt(qi, s_.astype(BF16)) + _dot(a.astype(BF16), v_),
             q_in, st, attn, v)
    for s, (s_, l, v_, ko) in enumerate(zip(st, last, v, k_out)):
        st_ref[s] = s_ * jnp.exp2(l) + _dot_tn(v_, ko)

    o = each(lambda a: _rmsnorm(a, hg_ref[...], EPS * GLA_DK), o)
    for (bi, _), c, a in zip(streams, vcols, o):
        r = r_ref[bi, :, c].astype(F32)
        o_ref[bi, :, c] = (a * (r * jax.nn.sigmoid(r))).astype(BF16)


def _gla(p_main, p_side, w2, b2, head_g, cast_weights, *, batch, seq):
    t = p_main.shape[0]
    p_main = p_main.reshape(batch, seq, GLA_MAIN_N)
    p_side = p_side.reshape(batch, seq, GLA_SIDE_N)
    c_len = GLA_CHUNK
    steps = seq // c_len
    assert 2 * GLA_KW == TOK_W
    split = 8
    assert split * split == steps
    by_step = lambda i: (i // split, i % split)
    jobs = [(w, lead, (w.shape[-2] // split, w.shape[-1] // split), by_step)
            for w, lead in cast_weights]
    c_in, c_out, c_shape, c_args = _cast_specs(jobs)
    res = pl.pallas_call(
        functools.partial(_gla_kernel, n_cast=len(jobs)),
        grid=(steps,),
        in_specs=[
            pl.BlockSpec((batch, c_len, 2 * TOK_W), lambda i: (0, i, 0)),
            pl.BlockSpec((batch, c_len, TOK_W), lambda i: (0, i, 2)),
            pl.BlockSpec((batch, c_len, GLA_RANKP), lambda i: (0, i, X_W // GLA_RANKP)),
            pl.BlockSpec((GLA_HEADS, GLA_RANKP, GLA_DKP), lambda i: (0, 0, 0)),
            pl.BlockSpec((GLA_HEADS, 1, GLA_DKP), lambda i: (0, 0, 0)),
            pl.BlockSpec((1, GLA_DV), lambda i: (0, 0)),
        ] + c_in,
        out_specs=[pl.BlockSpec((batch, c_len, TOK_W), lambda i: (0, i, 0))] + c_out,
        out_shape=[jax.ShapeDtypeStruct((batch, seq, TOK_W), BF16)] + c_shape,
        scratch_shapes=[pltpu.VMEM((batch * GLA_HEADS, GLA_DV, GLA_DKP), F32)],
        compiler_params=pltpu.CompilerParams(
            dimension_semantics=("arbitrary",), vmem_limit_bytes=VMEM_LIMIT),
        name="gla",
    )(p_main, p_main, p_side, w2, b2, head_g, *c_args)
    return res[0].reshape(t, TOK_W), tuple(res[1:])


def _conv_mixer_kernel(x_ref, g_ref, wb_ref, wc_ref, wx_ref, cw_ref, cb_ref, tok_ref, xq_ref,
                       h_ref, carry_ref, *, tiles_per_seq):
    i = pl.program_id(0)
    j = pl.program_id(1)
    c = jnp.maximum(j - 1, 0)
    tm = x_ref.shape[0]

    @pl.when(j == 0)
    def _():
        h_ref[...] = _rmsnorm(x_ref[...], g_ref[...]).astype(BF16)
        xq_ref[...] = _dot(h_ref[...], wb_ref[...]).astype(BF16)

    @pl.when((j > 0) & (i % tiles_per_seq == 0))
    def _():
        carry_ref[c] = jnp.zeros(carry_ref.shape[1:], F32)

    @pl.when(j > 0)
    def _():
        h = h_ref[...]
        u = _dot(h, wc_ref[...]) * _dot(h, wx_ref[...])
        u_prev = carry_ref[c]
        carry_ref[c] = u[tm - SUBLANE:, :]
        row = lax.broadcasted_iota(jnp.int32, u.shape, 0)
        u1 = jnp.where(row == 0, u_prev[SUBLANE - 1:SUBLANE, :], pltpu.roll(u, 1, axis=0))
        u2 = pltpu.roll(u, 2, axis=0)
        u2 = jnp.where(row == 0, u_prev[SUBLANE - 2:SUBLANE - 1, :], u2)
        u2 = jnp.where(row == 1, u_prev[SUBLANE - 1:SUBLANE, :], u2)
        w = cw_ref[...]
        y = u2 * w[0:1, :] + u1 * w[1:2, :] + u * w[2:3, :] + cb_ref[...]
        tok_ref[...] = (_dot(h, wb_ref[...]) * y).astype(BF16)


def _conv_mixer(x, g, g_lead, w, conv_w, conv_b, lead, *, seq, tm=1024, tw=512):
    t, d = x.shape
    assert tw == X_W
    n_conv = CONV_W // tw
    col = lambda j: jnp.maximum(j - 1, 0)
    return pl.pallas_call(
        functools.partial(_conv_mixer_kernel, tiles_per_seq=seq // tm),
        grid=(t // tm, n_conv + 1),
        in_specs=[
            pl.BlockSpec((tm, d), lambda i, j: (i, 0)),
            _stacked_spec(g_lead, (1, d), lambda i, j: (0, 0)),
            pl.BlockSpec((d, tw), lambda i, j: (0, jnp.where(j == 0, 3 * n_conv, j - 1))),
            pl.BlockSpec((d, tw), lambda i, j: (0, n_conv + col(j))),
            pl.BlockSpec((d, tw), lambda i, j: (0, 2 * n_conv + col(j))),
            _stacked_spec(lead, (CONV_K, tw), lambda i, j: (0, col(j))),
            _stacked_spec(lead, (1, tw), lambda i, j: (0, col(j))),
        ],
        out_specs=[
            pl.BlockSpec((tm, tw), lambda i, j: (i, col(j))),
            pl.BlockSpec((tm, tw), lambda i, j: (i, 0)),
        ],
        out_shape=[jax.ShapeDtypeStruct((t, CONV_W), BF16), jax.ShapeDtypeStruct((t, X_W), BF16)],
        scratch_shapes=[pltpu.VMEM((tm, d), BF16), pltpu.VMEM((n_conv, SUBLANE, tw), F32)],
        compiler_params=pltpu.CompilerParams(
            dimension_semantics=("arbitrary", "arbitrary"), vmem_limit_bytes=VMEM_LIMIT),
        name="conv_mixer",
    )(x, g, w, w, w, conv_w, conv_b)


def _mixout_kernel(tok_ref, xq_ref, kv_ref, wm_ref, x_ref, g_ref, o_ref):
    mixed = _dot(tok_ref[...], wm_ref[:TOK_W, :])
    xo = []
    for h in range(X_HEADS):
        lo = h * X_HEAD_DIM
        qh = xq_ref[:, lo:lo + X_HEAD_DIM]
        kh = kv_ref[:, lo:lo + X_HEAD_DIM]
        vh = kv_ref[:, X_W + lo:X_W + lo + X_HEAD_DIM]
        s = _dot_nt(qh, kh) * (X_HEAD_DIM ** -0.5)
        e = jnp.exp(s - jnp.max(s, axis=-1, keepdims=True))
        pr = (e / jnp.sum(e, axis=-1, keepdims=True)).astype(BF16)
        xo.append(_dot(pr, vh).astype(BF16))
    mixed = mixed + _dot(jnp.concatenate(xo, axis=1), wm_ref[TOK_W:, :])
    o_ref[...] = x_ref[...] + _rmsnorm(mixed, g_ref[...])


def _mixout(tok, p, xq_block, kv, wm, x, g, lead, *, seq, tm=512):
    t, d = x.shape
    per_seq = seq // tm
    return pl.pallas_call(
        _mixout_kernel,
        grid=(t // tm,),
        in_specs=[
            pl.BlockSpec((tm, TOK_W), lambda i: (i, 0)),
            pl.BlockSpec((tm, X_W), lambda i: (i, xq_block)),
            pl.BlockSpec((MEM_LEN, 2 * X_W), lambda i: (i // per_seq, 0)),
            pl.BlockSpec((d, d), lambda i: (0, 0)),
            pl.BlockSpec((tm, d), lambda i: (i, 0)),
            _stacked_spec(lead, (1, d), lambda i: (0, 0)),
        ],
        out_specs=pl.BlockSpec((tm, d), lambda i: (i, 0)),
        out_shape=jax.ShapeDtypeStruct((t, d), F32),
        compiler_params=pltpu.CompilerParams(
            dimension_semantics=("parallel",), vmem_limit_bytes=VMEM_LIMIT),
        name="mixout",
    )(tok, p, kv, wm, x, g)


def _gla_side_weight(wt):
    lo = GLA_MAIN_N
    pad = jnp.zeros((GLA_RANKP - GLA_RANK, wt.shape[1]), wt.dtype)
    return jnp.concatenate([wt[lo + GLA_RANK:], wt[lo:lo + GLA_RANK], pad], axis=0)


def _gla_gate_weights(gate_w2, gate_b):
    w2 = gate_w2.reshape(GLA_RANK, GLA_HEADS, GLA_DK).transpose(1, 0, 2)
    w2 = jnp.pad(w2, ((0, 0), (0, GLA_RANKP - GLA_RANK), (0, GLA_DKP - GLA_DK))).astype(BF16)
    b2 = jnp.pad(gate_b.reshape(GLA_HEADS, 1, GLA_DK), ((0, 0), (0, 0), (0, GLA_DKP - GLA_DK)))
    return w2, b2


def kernel(x, mem, ffn_pre_g, ffn_w_in, ffn_w_out, ffn_post_g, mix_pre_g, mix_post_g, mem_g,
           w_mem_kv, w_mix_out, gla_w_in, gla_gate_w2, gla_gate_b, gla_head_g,
           conv_w_in, conv_w, conv_b):
    batch, seq, d = x.shape
    x = x.reshape(batch * seq, d)
    mem2 = mem.reshape(batch * MEM_LEN, d)
    ffn_pre_g = ffn_pre_g[:, :, None, :]
    ffn_post_g = ffn_post_g[:, :, None, :]
    mix_pre_g = mix_pre_g[:, None, :]
    mix_post_g = mix_post_g[:, None, :]
    mem_g = mem_g[:, None, :]
    conv_b = conv_b[:, None, :]

    ffn_gains = (ffn_pre_g, ffn_post_g)
    ffn_stacks = (ffn_w_in, ffn_w_out)
    assert DEPTH == 2
    nt = batch * seq // 1024
    nf = D_FF // FFN_TF_BF16

    def mem_kv(i):
        return _norm_matmul(mem2, mem_g, (i,), w_mem_kv, (i,),
                            tm=batch * MEM_LEN, tn=2 * X_W, name="mem_kv")

    def ffn_pair_jobs(lead):
        return [_grid_cast_job(ffn_w_in, lead, nt, nf),
                _grid_cast_job(ffn_w_out, lead, nt, nf, rows_by_j=True)]


    x, _ = _ffn(x, ffn_gains, (0, 0), ffn_stacks, (0, 0), tf=FFN_TF_F32)
    w2, b2 = _gla_gate_weights(gla_gate_w2[0], gla_gate_b[0])
    wt = gla_w_in[0].T
    p, p_side = _gla_proj(x, mix_pre_g, (0,), wt, _gla_side_weight(wt))
    tok, (w01_in, w01_out, wm0, wconv) = _gla(
        p, p_side, w2, b2, gla_head_g[0][None],
        [(ffn_w_in, (0, 1)), (ffn_w_out, (0, 1)), (w_mix_out, (0,)), (conv_w_in, (0,))],
        batch=batch, seq=seq)
    x = _mixout(tok, p_side, 0, mem_kv(0), wm0, x, mix_post_g, (0,), seq=seq)
    x, (w10_in, w10_out, wm1) = _ffn(
        x, ffn_gains, (0, 1), (w01_in, w01_out), (), tf=FFN_TF_BF16,
        cast_jobs=ffn_pair_jobs((1, 0)) + [_grid_cast_job(w_mix_out, (1,), nt, nt)])

    x, (w11_in, w11_out) = _ffn(x, ffn_gains, (1, 0), (w10_in, w10_out), (), tf=FFN_TF_BF16,
                                cast_jobs=ffn_pair_jobs((1, 1)))
    tok, xq = _conv_mixer(x, mix_pre_g, (1,), wconv, conv_w, conv_b, (0,), seq=seq)
    x = _mixout(tok, xq, 0, mem_kv(1), wm1, x, mix_post_g, (1,), seq=seq)
    x, _ = _ffn(x, ffn_gains, (1, 1), (w11_in, w11_out), (), tf=FFN_TF_BF16)
    return x.reshape(batch, seq, d)
```

```python
import functools

import jax
import jax.numpy as jnp
from jax import lax
from jax.experimental import pallas as pl
from jax.experimental.pallas import tpu as pltpu

F32 = jnp.float32
BF16 = jnp.bfloat16

D_MODEL = 2048
DEPTH = 2
MEM_LEN = 256
TOK_W = 1536
X_HEADS = 4
X_HEAD_DIM = 128
X_W = 512
GLA_HEADS = 4
GLA_DV = 384
GLA_DK = 192
GLA_KW = 768
GLA_RANK = 16
GLA_TAU = 16.0
CONV_W = 1536
CONV_K = 3
D_FF = 5632
EPS = 1e-6
LOG2E = 1.4426950408889634

LANE = 128
SUBLANE = 8
MXU_DIM = 256
VMEM_LIMIT = 56 * 1024 * 1024
VMEM_LIMIT_FFN = 60 * 1024 * 1024
NORM_ROWS = 256
FFN_TF_F32 = 256
FFN_TF_BF16 = 512

GLA_RANKP = LANE
GLA_DKP = MXU_DIM
GLA_CHUNK = 64
GLA_MAIN_N = 2 * GLA_KW + 2 * TOK_W
GLA_SIDE_N = X_W + GLA_RANKP
CONV_NP = 3 * CONV_W + X_W


def _rmsnorm(x, g, eps=EPS):
    ms = jnp.mean(x * x, axis=-1, keepdims=True)
    return x * lax.rsqrt(ms + eps) * g


def _dot(a, b):
    return jnp.dot(a, b, preferred_element_type=F32)


def _dot_nt(a, b):
    return lax.dot_general(a, b, (((1,), (1,)), ((), ())), preferred_element_type=F32)


def _dot_tn(a, b):
    return lax.dot_general(a, b, (((0,), (0,)), ((), ())), preferred_element_type=F32)


def _stacked_spec(lead, block, index_map):
    nlead = len(lead)
    return pl.BlockSpec((None,) * nlead + block, lambda *g: tuple(lead) + tuple(index_map(*g)))


def _for_row_chunks(n_rows, body):
    def step(c, carry):
        body(pl.ds(pl.multiple_of(c * NORM_ROWS, NORM_ROWS), NORM_ROWS))
        return carry

    lax.fori_loop(0, n_rows // NORM_ROWS, step, 0, unroll=2)


def _row_rsqrt_ms(src_ref, rs_ref):
    def rows_rs(rows):
        v = src_ref[rows, :]
        ms = jnp.mean(v * v, axis=-1, keepdims=True)
        rs_ref[rows, :] = jnp.broadcast_to(lax.rsqrt(ms + EPS), (NORM_ROWS, LANE))

    _for_row_chunks(src_ref.shape[0], rows_rs)


def _cast_specs(jobs):
    in_specs = [_stacked_spec(lead, blk, imap) for _, lead, blk, imap in jobs]
    out_specs = [pl.BlockSpec(blk, imap) for _, _, blk, imap in jobs]
    out_shape = [jax.ShapeDtypeStruct(w.shape[len(lead):], BF16) for w, lead, _, _ in jobs]
    return in_specs, out_specs, out_shape, [w for w, _, _, _ in jobs]


def _split_cast_refs(refs, n_in, n_out, n_cast):
    ins, refs = refs[:n_in], refs[n_in:]
    cast_in, refs = refs[:n_cast], refs[n_cast:]
    outs, refs = refs[:n_out], refs[n_out:]
    cast_out, scratch = refs[:n_cast], refs[n_cast:]
    return ins, outs, scratch, list(zip(cast_in, cast_out))


def _run_casts(casts):
    for src, dst in casts:
        dst[...] = src[...].astype(BF16)


def _ffn_kernel(*refs, n_cast):
    ins, (o_ref,), (h_ref, rs_ref), casts = _split_cast_refs(refs, 6, 1, n_cast)
    x_ref, pre_g_ref, wg_ref, wu_ref, wo_ref, post_g_ref = ins
    j = pl.program_id(1)
    tm = x_ref.shape[0]

    @pl.when(j == 0)
    def _():
        def rows_in(rows):
            h_ref[rows, :] = _rmsnorm(x_ref[rows, :], pre_g_ref[...]).astype(BF16)
            o_ref[rows, :] = jnp.zeros((NORM_ROWS, o_ref.shape[1]), F32)

        _for_row_chunks(tm, rows_in)

    h = h_ref[...]
    gate = _dot(h, wg_ref[...].astype(BF16))
    up = _dot(h, wu_ref[...].astype(BF16))
    act = (gate * jax.nn.sigmoid(gate) * up).astype(BF16)
    _run_casts(casts)
    o_ref[...] += _dot(act, wo_ref[...].astype(BF16))

    @pl.when(j == pl.num_programs(1) - 1)
    def _():
        _row_rsqrt_ms(o_ref, rs_ref)
        half_g = 0.5 * post_g_ref[...]

        def rows_out(rows):
            o_ref[rows, :] = x_ref[rows, :] + o_ref[rows, :] * rs_ref[rows, 0:1] * half_g

        _for_row_chunks(tm, rows_out)


def _grid_cast_job(w, lead, n_i, n_j, rows_by_j=False):
    if rows_by_j:
        blk = (w.shape[-2] // n_j, w.shape[-1] // n_i)
        return (w, lead, blk, lambda i, j: (jnp.minimum(j, n_j - 1), i))
    blk = (w.shape[-2] // n_i, w.shape[-1] // n_j)
    return (w, lead, blk, lambda i, j: (i, jnp.minimum(j, n_j - 1)))


def _ffn(x, gains, g_lead, weights, w_lead, *, tf, tm=1024, cast_jobs=()):
    t, d = x.shape
    nt, nf = t // tm, D_FF // tf
    pre_g, post_g = gains
    w_in, w_out = weights
    c_in, c_out, c_shape, c_args = _cast_specs(cast_jobs)
    ff = lambda i, j: jnp.where(i % 2 == 0, j, nf - 1 - j)
    res = pl.pallas_call(
        functools.partial(_ffn_kernel, n_cast=len(cast_jobs)),
        grid=(nt, nf),
        in_specs=[
            pl.BlockSpec((tm, d), lambda i, j: (i, 0)),
            _stacked_spec(g_lead, (1, d), lambda i, j: (0, 0)),
            _stacked_spec(w_lead, (d, tf), lambda i, j: (0, ff(i, j))),
            _stacked_spec(w_lead, (d, tf), lambda i, j: (0, ff(i, j) + nf)),
            _stacked_spec(w_lead, (tf, d), lambda i, j: (ff(i, j), 0)),
            _stacked_spec(g_lead, (1, d), lambda i, j: (0, 0)),
        ] + c_in,
        out_specs=[pl.BlockSpec((tm, d), lambda i, j: (i, 0))] + c_out,
        out_shape=[jax.ShapeDtypeStruct((t, d), F32)] + c_shape,
        scratch_shapes=[pltpu.VMEM((tm, d), BF16), pltpu.VMEM((tm, LANE), F32)],
        compiler_params=pltpu.CompilerParams(
            dimension_semantics=("parallel", "arbitrary"), vmem_limit_bytes=VMEM_LIMIT_FFN),
        name="ffn",
    )(x, pre_g, w_in, w_in, w_out, post_g, *c_args)
    return res[0], tuple(res[1:])


def _norm_matmul_kernel(x_ref, g_ref, w_ref, o_ref, h_ref):
    @pl.when(pl.program_id(1) == 0)
    def _():
        h_ref[...] = _rmsnorm(x_ref[...], g_ref[...]).astype(BF16)

    o_ref[...] = _dot(h_ref[...], w_ref[...].astype(BF16)).astype(o_ref.dtype)


def _norm_matmul(x, g, g_lead, w, w_lead, *, tm, tn, name):
    t, d = x.shape
    n = w.shape[-1]
    return pl.pallas_call(
        _norm_matmul_kernel,
        grid=(t // tm, n // tn),
        in_specs=[
            pl.BlockSpec((tm, d), lambda i, j: (i, 0)),
            _stacked_spec(g_lead, (1, d), lambda i, j: (0, 0)),
            _stacked_spec(w_lead, (d, tn), lambda i, j: (0, j)),
        ],
        out_specs=pl.BlockSpec((tm, tn), lambda i, j: (i, j)),
        out_shape=jax.ShapeDtypeStruct((t, n), BF16),
        scratch_shapes=[pltpu.VMEM((tm, d), BF16)],
        compiler_params=pltpu.CompilerParams(
            dimension_semantics=("parallel", "arbitrary"), vmem_limit_bytes=VMEM_LIMIT),
        name=name,
    )(x, g, w)


def _gla_proj_kernel(x_ref, g_ref, wm_ref, ws_ref, om_ref, os_ref, h_ref):
    j = pl.program_id(1)

    @pl.when(j == 0)
    def _():
        h_ref[...] = _rmsnorm(x_ref[...], g_ref[...]).astype(BF16)
        os_ref[...] = _dot_nt(h_ref[...], ws_ref[...].astype(BF16)).astype(BF16)

    @pl.when(j > 0)
    def _():
        om_ref[...] = _dot_nt(h_ref[...], wm_ref[...].astype(BF16)).astype(BF16)


def _gla_proj(x, g, g_lead, wt, wt_side, *, tm=1024, tn=768):
    t, d = x.shape
    n_main = GLA_MAIN_N // tn
    blk = lambda j: jnp.maximum(j - 1, 0)
    w_blk = lambda j: jnp.where(j == 0, n_main - 1, j - 1)
    return pl.pallas_call(
        _gla_proj_kernel,
        grid=(t // tm, n_main + 1),
        in_specs=[
            pl.BlockSpec((tm, d), lambda i, j: (i, 0)),
            _stacked_spec(g_lead, (1, d), lambda i, j: (0, 0)),
            pl.BlockSpec((tn, d), lambda i, j: (w_blk(j), 0)),
            pl.BlockSpec((GLA_SIDE_N, d), lambda i, j: (0, 0)),
        ],
        out_specs=[
            pl.BlockSpec((tm, tn), lambda i, j: (i, blk(j))),
            pl.BlockSpec((tm, GLA_SIDE_N), lambda i, j: (i, 0)),
        ],
        out_shape=[jax.ShapeDtypeStruct((t, GLA_MAIN_N), BF16),
                   jax.ShapeDtypeStruct((t, GLA_SIDE_N), BF16)],
        scratch_shapes=[pltpu.VMEM((tm, d), BF16)],
        compiler_params=pltpu.CompilerParams(
            dimension_semantics=("parallel", "arbitrary"), vmem_limit_bytes=VMEM_LIMIT),
        name="gla_proj",
    )(x, g, wt, wt_side)


def _shift_groups(a, n):
    return jnp.concatenate([a[n:], a[:n]], axis=0)


def _block_first_group(a, n):
    return jnp.concatenate([a[g - g % n:g - g % n + 1] for g in range(a.shape[0])], axis=0)


def _gla_kernel(*refs, n_cast):
    ins, (o_ref,), (st_ref,), casts = _split_cast_refs(refs, 6, 1, n_cast)
    qkv_ref, r_ref, gl_ref, w2_ref, b2_ref, hg_ref = ins
    c_len = GLA_CHUNK
    ng = c_len // SUBLANE
    nb = qkv_ref.shape[0]

    @pl.when(pl.program_id(0) == 0)
    def _():
        st_ref[...] = jnp.zeros_like(st_ref)

    r3 = lax.broadcasted_iota(jnp.int32, (ng, SUBLANE, GLA_DKP), 1)
    ti = lax.broadcasted_iota(jnp.int32, (c_len, c_len), 0)
    si = lax.broadcasted_iota(jnp.int32, (c_len, c_len), 1)
    txs = jnp.where(ti > si, ti ^ si, 0)
    to3 = lambda a: a.reshape(ng, SUBLANE, GLA_DKP)
    to2 = lambda a: a.reshape(c_len, GLA_DKP)

    streams = [divmod(s, GLA_HEADS) for s in range(nb * GLA_HEADS)]
    vcols = [pl.ds(hd * GLA_DV, GLA_DV) for _, hd in streams]
    each = lambda f, *lists: [f(*args) for args in zip(*lists)]

    z = [_dot(gl_ref[bi], w2_ref[hd]) + b2_ref[hd] for bi, hd in streams]
    _run_casts(casts)
    qkv = [qkv_ref[bi, :, :2 * GLA_KW + GLA_DKP].astype(F32) for bi in range(nb)]
    lane = lax.broadcasted_iota(jnp.int32, (c_len, GLA_DKP), 1)
    head = lambda a, lo: to3(jnp.where(lane < GLA_DK, a[:, lo:lo + GLA_DKP], 0.0))
    q = [head(qkv[bi], hd * GLA_DK) for bi, hd in streams]
    k = [head(qkv[bi], GLA_KW + hd * GLA_DK) for bi, hd in streams]

    def log2_decay(zs):
        soft = jnp.log2(1.0 + jnp.exp2(jnp.abs(zs) * -LOG2E))
        return to3((jnp.minimum(zs, 0.0) * LOG2E - soft) * (1.0 / GLA_TAU))

    cum = each(log2_decay, z)
    d = 1
    while d < SUBLANE:
        cum = each(lambda a: a + jnp.where(r3 >= d, pltpu.roll(a, d, axis=1), 0.0), cum)
        d *= 2
    tot = each(lambda a: a[:, SUBLANE - 1:SUBLANE, :], cum)
    offs = [[jnp.zeros_like(t[0:1])] for t in tot]
    for g in range(1, ng):
        for o_s, t in zip(offs, tot):
            o_s.append(o_s[-1] + t[g - 1:g])
    cum = each(lambda a, o_s: a + jnp.concatenate(o_s, axis=0), cum, offs)

    nt = lambda a, b: _dot_nt(to2(a).astype(BF16), to2(b).astype(BF16))
    attn = each(lambda a, b: jnp.where(ti == si, nt(a, b), 0.0), q, k)
    g_h = cum
    h = 1
    while h < c_len:
        if h < SUBLANE:
            if h > 1:
                g_h = each(lambda a: jnp.where((r3 & (h // 2)) != 0,
                                               pltpu.roll(a, h // 2, axis=1), a), g_h)
            g_next = each(lambda a: pltpu.roll(a, SUBLANE - h, axis=1), g_h)
        else:
            n = h // SUBLANE
            g_h = each(lambda a: jnp.broadcast_to(a[:, 0:1, :], a.shape), cum)
            if n > 1:
                g_h = each(lambda a: _block_first_group(a, n), g_h)
            g_next = each(lambda a: _shift_groups(a, n), g_h)
        q_h = q if h == 1 else each(lambda a, c, g: a * jnp.exp2(c - g), q, cum, g_h)
        k_h = each(lambda a, c, g: a * jnp.exp2(g - c), k, cum, g_next)
        a_h = each(nt, q_h, k_h)
        attn = each(lambda new, old: jnp.where((txs >= h) & (txs < 2 * h), new, old), a_h, attn)
        h *= 2

    last = each(lambda a: a[ng - 1, SUBLANE - 1:SUBLANE, :], cum)
    q_in = each(lambda a, c: to2(a * jnp.exp2(c)).astype(BF16), q, cum)
    k_out = each(lambda a, c, l: to2(a * jnp.exp2(l - c)).astype(BF16), k, cum, last)
    v = [qkv_ref[bi, :, pl.ds(2 * GLA_KW + hd * GLA_DV, GLA_DV)] for bi, hd in streams]
    st = [st_ref[s] for s in range(len(streams))]
    o = each(lambda qi, s_, a, v_: _dot_nt(qi, s_.astype(BF16)) + _dot(a.astype(BF16), v_),
             q_in, st, attn, v)
    for s, (s_, l, v_, ko) in enumerate(zip(st, last, v, k_out)):
        st_ref[s] = s_ * jnp.exp2(l) + _dot_tn(v_, ko)

    o = each(lambda a: _rmsnorm(a, hg_ref[...], EPS * GLA_DK), o)
    for (bi, _), c, a in zip(streams, vcols, o):
        r = r_ref[bi, :, c].astype(F32)
        o_ref[bi, :, c] = (a * (r * jax.nn.sigmoid(r))).astype(BF16)


def _gla(p_main, p_side, w2, b2, head_g, cast_weights, *, batch, seq):
    t = p_main.shape[0]
    p_main = p_main.reshape(batch, seq, GLA_MAIN_N)
    p_side = p_side.reshape(batch, seq, GLA_SIDE_N)
    c_len = GLA_CHUNK
    steps = seq // c_len
    assert 2 * GLA_KW == TOK_W
    split = 8
    assert split * split == steps
    by_step = lambda i: (i // split, i % split)
    jobs = [(w, lead, (w.shape[-2] // split, w.shape[-1] // split), by_step)
            for w, lead in cast_weights]
    c_in, c_out, c_shape, c_args = _cast_specs(jobs)
    res = pl.pallas_call(
        functools.partial(_gla_kernel, n_cast=len(jobs)),
        grid=(steps,),
        in_specs=[
            pl.BlockSpec((batch, c_len, 2 * TOK_W), lambda i: (0, i, 0)),
            pl.BlockSpec((batch, c_len, TOK_W), lambda i: (0, i, 2)),
            pl.BlockSpec((batch, c_len, GLA_RANKP), lambda i: (0, i, X_W // GLA_RANKP)),
            pl.BlockSpec((GLA_HEADS, GLA_RANKP, GLA_DKP), lambda i: (0, 0, 0)),
            pl.BlockSpec((GLA_HEADS, 1, GLA_DKP), lambda i: (0, 0, 0)),
            pl.BlockSpec((1, GLA_DV), lambda i: (0, 0)),
        ] + c_in,
        out_specs=[pl.BlockSpec((batch, c_len, TOK_W), lambda i: (0, i, 0))] + c_out,
        out_shape=[jax.ShapeDtypeStruct((batch, seq, TOK_W), BF16)] + c_shape,
        scratch_shapes=[pltpu.VMEM((batch * GLA_HEADS, GLA_DV, GLA_DKP), F32)],
        compiler_params=pltpu.CompilerParams(
            dimension_semantics=("arbitrary",), vmem_limit_bytes=VMEM_LIMIT),
        name="gla",
    )(p_main, p_main, p_side, w2, b2, head_g, *c_args)
    return res[0].reshape(t, TOK_W), tuple(res[1:])


def _conv_mixer_kernel(x_ref, g_ref, wb_ref, wc_ref, wx_ref, cw_ref, cb_ref, tok_ref, xq_ref,
                       h_ref, carry_ref, *, tiles_per_seq):
    i = pl.program_id(0)
    j = pl.program_id(1)
    c = jnp.maximum(j - 1, 0)
    tm = x_ref.shape[0]

    @pl.when(j == 0)
    def _():
        h_ref[...] = _rmsnorm(x_ref[...], g_ref[...]).astype(BF16)
        xq_ref[...] = _dot(h_ref[...], wb_ref[...]).astype(BF16)

    @pl.when((j > 0) & (i % tiles_per_seq == 0))
    def _():
        carry_ref[c] = jnp.zeros(carry_ref.shape[1:], F32)

    @pl.when(j > 0)
    def _():
        h = h_ref[...]
        u = _dot(h, wc_ref[...]) * _dot(h, wx_ref[...])
        u_prev = carry_ref[c]
        carry_ref[c] = u[tm - SUBLANE:, :]
        row = lax.broadcasted_iota(jnp.int32, u.shape, 0)
        u1 = jnp.where(row == 0, u_prev[SUBLANE - 1:SUBLANE, :], pltpu.roll(u, 1, axis=0))
        u2 = pltpu.roll(u, 2, axis=0)
        u2 = jnp.where(row == 0, u_prev[SUBLANE - 2:SUBLANE - 1, :], u2)
        u2 = jnp.where(row == 1, u_prev[SUBLANE - 1:SUBLANE, :], u2)
        w = cw_ref[...]
        y = u2 * w[0:1, :] + u1 * w[1:2, :] + u * w[2:3, :] + cb_ref[...]
        tok_ref[...] = (_dot(h, wb_ref[...]) * y).astype(BF16)


def _conv_mixer(x, g, g_lead, w, conv_w, conv_b, lead, *, seq, tm=1024, tw=512):
    t, d = x.shape
    assert tw == X_W
    n_conv = CONV_W // tw
    col = lambda j: jnp.maximum(j - 1, 0)
    stay = lambda j: jnp.where(j == 0, n_conv - 1, j - 1)
    return pl.pallas_call(
        functools.partial(_conv_mixer_kernel, tiles_per_seq=seq // tm),
        grid=(t // tm, n_conv + 1),
        in_specs=[
            pl.BlockSpec((tm, d), lambda i, j: (i, 0)),
            _stacked_spec(g_lead, (1, d), lambda i, j: (0, 0)),
            pl.BlockSpec((d, tw), lambda i, j: (0, jnp.where(j == 0, 3 * n_conv, j - 1))),
            pl.BlockSpec((d, tw), lambda i, j: (0, n_conv + stay(j))),
            pl.BlockSpec((d, tw), lambda i, j: (0, 2 * n_conv + stay(j))),
            _stacked_spec(lead, (CONV_K, tw), lambda i, j: (0, col(j))),
            _stacked_spec(lead, (1, tw), lambda i, j: (0, col(j))),
        ],
        out_specs=[
            pl.BlockSpec((tm, tw), lambda i, j: (i, col(j))),
            pl.BlockSpec((tm, tw), lambda i, j: (i, 0)),
        ],
        out_shape=[jax.ShapeDtypeStruct((t, CONV_W), BF16), jax.ShapeDtypeStruct((t, X_W), BF16)],
        scratch_shapes=[pltpu.VMEM((tm, d), BF16), pltpu.VMEM((n_conv, SUBLANE, tw), F32)],
        compiler_params=pltpu.CompilerParams(
            dimension_semantics=("arbitrary", "arbitrary"), vmem_limit_bytes=VMEM_LIMIT),
        name="conv_mixer",
    )(x, g, w, w, w, conv_w, conv_b)


def _mixout_kernel(tok_ref, xq_ref, kv_ref, wm_ref, x_ref, g_ref, o_ref):
    mixed = _dot(tok_ref[...], wm_ref[:TOK_W, :])
    xo = []
    for h in range(X_HEADS):
        lo = h * X_HEAD_DIM
        qh = xq_ref[:, lo:lo + X_HEAD_DIM]
        kh = kv_ref[:, lo:lo + X_HEAD_DIM]
        vh = kv_ref[:, X_W + lo:X_W + lo + X_HEAD_DIM]
        s = _dot_nt(qh, kh) * (X_HEAD_DIM ** -0.5)
        e = jnp.exp(s - jnp.max(s, axis=-1, keepdims=True))
        pr = (e / jnp.sum(e, axis=-1, keepdims=True)).astype(BF16)
        xo.append(_dot(pr, vh).astype(BF16))
    mixed = mixed + _dot(jnp.concatenate(xo, axis=1), wm_ref[TOK_W:, :])
    o_ref[...] = x_ref[...] + _rmsnorm(mixed, g_ref[...])


def _mixout(tok, p, xq_block, kv, wm, x, g, lead, *, seq, tm=512):
    t, d = x.shape
    per_seq = seq // tm
    return pl.pallas_call(
        _mixout_kernel,
        grid=(t // tm,),
        in_specs=[
            pl.BlockSpec((tm, TOK_W), lambda i: (i, 0)),
            pl.BlockSpec((tm, X_W), lambda i: (i, xq_block)),
            pl.BlockSpec((MEM_LEN, 2 * X_W), lambda i: (i // per_seq, 0)),
            pl.BlockSpec((d, d), lambda i: (0, 0)),
            pl.BlockSpec((tm, d), lambda i: (i, 0)),
            _stacked_spec(lead, (1, d), lambda i: (0, 0)),
        ],
        out_specs=pl.BlockSpec((tm, d), lambda i: (i, 0)),
        out_shape=jax.ShapeDtypeStruct((t, d), F32),
        compiler_params=pltpu.CompilerParams(
            dimension_semantics=("parallel",), vmem_limit_bytes=VMEM_LIMIT),
        name="mixout",
    )(tok, p, kv, wm, x, g)


def _gla_side_weight(wt):
    lo = GLA_MAIN_N
    pad = jnp.zeros((GLA_RANKP - GLA_RANK, wt.shape[1]), wt.dtype)
    return jnp.concatenate([wt[lo + GLA_RANK:], wt[lo:lo + GLA_RANK], pad], axis=0)


def _gla_gate_weights(gate_w2, gate_b):
    w2 = gate_w2.reshape(GLA_RANK, GLA_HEADS, GLA_DK).transpose(1, 0, 2)
    w2 = jnp.pad(w2, ((0, 0), (0, GLA_RANKP - GLA_RANK), (0, GLA_DKP - GLA_DK))).astype(BF16)
    b2 = jnp.pad(gate_b.reshape(GLA_HEADS, 1, GLA_DK), ((0, 0), (0, 0), (0, GLA_DKP - GLA_DK)))
    return w2, b2


def kernel(x, mem, ffn_pre_g, ffn_w_in, ffn_w_out, ffn_post_g, mix_pre_g, mix_post_g, mem_g,
           w_mem_kv, w_mix_out, gla_w_in, gla_gate_w2, gla_gate_b, gla_head_g,
           conv_w_in, conv_w, conv_b):
    batch, seq, d = x.shape
    x = x.reshape(batch * seq, d)
    mem2 = mem.reshape(batch * MEM_LEN, d)
    ffn_pre_g = ffn_pre_g[:, :, None, :]
    ffn_post_g = ffn_post_g[:, :, None, :]
    mix_pre_g = mix_pre_g[:, None, :]
    mix_post_g = mix_post_g[:, None, :]
    mem_g = mem_g[:, None, :]
    conv_b = conv_b[:, None, :]

    ffn_gains = (ffn_pre_g, ffn_post_g)
    ffn_stacks = (ffn_w_in, ffn_w_out)
    assert DEPTH == 2
    nt = batch * seq // 1024
    nf = D_FF // FFN_TF_BF16

    def mem_kv(i):
        return _norm_matmul(mem2, mem_g, (i,), w_mem_kv, (i,),
                            tm=batch * MEM_LEN, tn=2 * X_W, name="mem_kv")

    def ffn_pair_jobs(lead):
        return [_grid_cast_job(ffn_w_in, lead, nt, nf),
                _grid_cast_job(ffn_w_out, lead, nt, nf, rows_by_j=True)]


    x, _ = _ffn(x, ffn_gains, (0, 0), ffn_stacks, (0, 0), tf=FFN_TF_F32)
    w2, b2 = _gla_gate_weights(gla_gate_w2[0], gla_gate_b[0])
    wt = gla_w_in[0].T
    p, p_side = _gla_proj(x, mix_pre_g, (0,), wt, _gla_side_weight(wt))
    tok, (w01_in, w01_out, wm0, wconv) = _gla(
        p, p_side, w2, b2, gla_head_g[0][None],
        [(ffn_w_in, (0, 1)), (ffn_w_out, (0, 1)), (w_mix_out, (0,)), (conv_w_in, (0,))],
        batch=batch, seq=seq)
    x = _mixout(tok, p_side, 0, mem_kv(0), wm0, x, mix_post_g, (0,), seq=seq)
    x, (w10_in, w10_out, wm1) = _ffn(
        x, ffn_gains, (0, 1), (w01_in, w01_out), (), tf=FFN_TF_BF16,
        cast_jobs=ffn_pair_jobs((1, 0)) + [_grid_cast_job(w_mix_out, (1,), nt, nt)])

    x, (w11_in, w11_out) = _ffn(x, ffn_gains, (1, 0), (w10_in, w10_out), (), tf=FFN_TF_BF16,
                                cast_jobs=ffn_pair_jobs((1, 1)))
    tok, xq = _conv_mixer(x, mix_pre_g, (1,), wconv, conv_w, conv_b, (0,), seq=seq)
    x = _mixout(tok, xq, 0, mem_kv(1), wm1, x, mix_post_g, (1,), seq=seq)
    x, _ = _ffn(x, ffn_gains, (1, 1), (w11_in, w11_out), (), tf=FFN_TF_BF16)
    return x.reshape(batch, seq, d)
```

```python
import functools

import jax
import jax.numpy as jnp
from jax import lax
from jax.experimental import pallas as pl
from jax.experimental.pallas import tpu as pltpu

F32 = jnp.float32
BF16 = jnp.bfloat16

D_MODEL = 2048
DEPTH = 2
MEM_LEN = 256
TOK_W = 1536
X_HEADS = 4
X_HEAD_DIM = 128
X_W = 512
GLA_HEADS = 4
GLA_DV = 384
GLA_DK = 192
GLA_KW = 768
GLA_RANK = 16
GLA_TAU = 16.0
CONV_W = 1536
CONV_K = 3
D_FF = 5632
EPS = 1e-6
LOG2E = 1.4426950408889634

LANE = 128
SUBLANE = 8
MXU_DIM = 256
VMEM_LIMIT = 56 * 1024 * 1024
VMEM_LIMIT_FFN = 60 * 1024 * 1024
NORM_ROWS = 256
FFN_TF_F32 = 256
FFN_TF_BF16 = 512

GLA_RANKP = LANE
GLA_DKP = MXU_DIM
GLA_CHUNK = 64
GLA_MAIN_N = 2 * GLA_KW + 2 * TOK_W
GLA_SIDE_N = X_W + GLA_RANKP
CONV_NP = 3 * CONV_W + X_W


def _rmsnorm(x, g, eps=EPS):
    ms = jnp.mean(x * x, axis=-1, keepdims=True)
    return x * lax.rsqrt(ms + eps) * g


def _dot(a, b):
    return jnp.dot(a, b, preferred_element_type=F32)


def _dot_nt(a, b):
    return lax.dot_general(a, b, (((1,), (1,)), ((), ())), preferred_element_type=F32)


def _dot_tn(a, b):
    return lax.dot_general(a, b, (((0,), (0,)), ((), ())), preferred_element_type=F32)


def _stacked_spec(lead, block, index_map):
    nlead = len(lead)
    return pl.BlockSpec((None,) * nlead + block, lambda *g: tuple(lead) + tuple(index_map(*g)))


def _for_row_chunks(n_rows, body):
    def step(c, carry):
        body(pl.ds(pl.multiple_of(c * NORM_ROWS, NORM_ROWS), NORM_ROWS))
        return carry

    lax.fori_loop(0, n_rows // NORM_ROWS, step, 0, unroll=2)


def _row_rsqrt_ms(src_ref, rs_ref):
    def rows_rs(rows):
        v = src_ref[rows, :]
        ms = jnp.mean(v * v, axis=-1, keepdims=True)
        rs_ref[rows, :] = jnp.broadcast_to(lax.rsqrt(ms + EPS), (NORM_ROWS, LANE))

    _for_row_chunks(src_ref.shape[0], rows_rs)


def _cast_specs(jobs):
    in_specs, out_specs, out_shape = [], [], []
    for w, lead, (rows, cols), imap, col_tile in jobs:
        n_rows, n_cols = w.shape[len(lead):]
        in_specs.append(_stacked_spec(lead, (rows, cols), imap))
        if col_tile is None:
            out_specs.append(pl.BlockSpec((rows, cols), imap))
            out_shape.append(jax.ShapeDtypeStruct((n_rows, n_cols), BF16))
        else:
            slabs = cols // col_tile
            out_specs.append(pl.BlockSpec((slabs, rows, col_tile),
                                          lambda *g, imap=imap: (imap(*g)[1], imap(*g)[0], 0)))
            out_shape.append(jax.ShapeDtypeStruct((n_cols // col_tile, n_rows, col_tile), BF16))
    return in_specs, out_specs, out_shape, [job[0] for job in jobs]


def _split_cast_refs(refs, n_in, n_out, n_cast):
    ins, refs = refs[:n_in], refs[n_in:]
    cast_in, refs = refs[:n_cast], refs[n_cast:]
    outs, refs = refs[:n_out], refs[n_out:]
    cast_out, scratch = refs[:n_cast], refs[n_cast:]
    return ins, outs, scratch, list(zip(cast_in, cast_out))


def _run_casts(casts):
    for src, dst in casts:
        if len(dst.shape) == 2:
            dst[...] = src[...].astype(BF16)
        else:
            tile = dst.shape[2]
            for s in range(dst.shape[0]):
                dst[s] = src[:, s * tile:(s + 1) * tile].astype(BF16)


def _ffn_kernel(*refs, n_cast):
    ins, (o_ref,), (h_ref, rs_ref), casts = _split_cast_refs(refs, 6, 1, n_cast)
    x_ref, pre_g_ref, wg_ref, wu_ref, wo_ref, post_g_ref = ins
    j = pl.program_id(1)
    tm = x_ref.shape[0]

    @pl.when(j == 0)
    def _():
        def rows_in(rows):
            h_ref[rows, :] = _rmsnorm(x_ref[rows, :], pre_g_ref[...]).astype(BF16)
            o_ref[rows, :] = jnp.zeros((NORM_ROWS, o_ref.shape[1]), F32)

        _for_row_chunks(tm, rows_in)

    h = h_ref[...]
    gate = _dot(h, wg_ref[...].astype(BF16))
    up = _dot(h, wu_ref[...].astype(BF16))
    act = (gate * jax.nn.sigmoid(gate) * up).astype(BF16)
    _run_casts(casts)
    o_ref[...] += _dot(act, wo_ref[...].astype(BF16))

    @pl.when(j == pl.num_programs(1) - 1)
    def _():
        _row_rsqrt_ms(o_ref, rs_ref)
        half_g = 0.5 * post_g_ref[...]

        def rows_out(rows):
            o_ref[rows, :] = x_ref[rows, :] + o_ref[rows, :] * rs_ref[rows, 0:1] * half_g

        _for_row_chunks(tm, rows_out)


def _grid_cast_job(w, lead, n_i, n_j, rows_by_j=False, col_tile=None):
    if rows_by_j:
        blk = (w.shape[-2] // n_j, w.shape[-1] // n_i)
        return (w, lead, blk, lambda i, j: (jnp.minimum(j, n_j - 1), i), col_tile)
    blk = (w.shape[-2] // n_i, w.shape[-1] // n_j)
    return (w, lead, blk, lambda i, j: (i, jnp.minimum(j, n_j - 1)), col_tile)


def _ffn(x, gains, g_lead, weights, w_lead, *, tf, tm=1024, cast_jobs=()):
    t, d = x.shape
    nt, nf = t // tm, D_FF // tf
    pre_g, post_g = gains
    w_in, w_out = weights
    c_in, c_out, c_shape, c_args = _cast_specs(cast_jobs)
    if w_in.ndim - len(w_lead) == 3:
        assert w_in.shape[-1] == tf
        gate_spec = _stacked_spec(w_lead, (None, d, tf), lambda i, j: (j, 0, 0))
        up_spec = _stacked_spec(w_lead, (None, d, tf), lambda i, j: (j + nf, 0, 0))
    else:
        gate_spec = _stacked_spec(w_lead, (d, tf), lambda i, j: (0, j))
        up_spec = _stacked_spec(w_lead, (d, tf), lambda i, j: (0, j + nf))
    res = pl.pallas_call(
        functools.partial(_ffn_kernel, n_cast=len(cast_jobs)),
        grid=(nt, nf),
        in_specs=[
            pl.BlockSpec((tm, d), lambda i, j: (i, 0)),
            _stacked_spec(g_lead, (1, d), lambda i, j: (0, 0)),
            gate_spec,
            up_spec,
            _stacked_spec(w_lead, (tf, d), lambda i, j: (j, 0)),
            _stacked_spec(g_lead, (1, d), lambda i, j: (0, 0)),
        ] + c_in,
        out_specs=[pl.BlockSpec((tm, d), lambda i, j: (i, 0))] + c_out,
        out_shape=[jax.ShapeDtypeStruct((t, d), F32)] + c_shape,
        scratch_shapes=[pltpu.VMEM((tm, d), BF16), pltpu.VMEM((tm, LANE), F32)],
        compiler_params=pltpu.CompilerParams(
            dimension_semantics=("parallel", "arbitrary"), vmem_limit_bytes=VMEM_LIMIT_FFN),
        name="ffn",
    )(x, pre_g, w_in, w_in, w_out, post_g, *c_args)
    return res[0], tuple(res[1:])


def _norm_matmul_kernel(x_ref, g_ref, w_ref, o_ref, h_ref):
    @pl.when(pl.program_id(1) == 0)
    def _():
        h_ref[...] = _rmsnorm(x_ref[...], g_ref[...]).astype(BF16)

    o_ref[...] = _dot(h_ref[...], w_ref[...].astype(BF16)).astype(o_ref.dtype)


def _norm_matmul(x, g, g_lead, w, w_lead, *, tm, tn, name):
    t, d = x.shape
    n = w.shape[-1]
    return pl.pallas_call(
        _norm_matmul_kernel,
        grid=(t // tm, n // tn),
        in_specs=[
            pl.BlockSpec((tm, d), lambda i, j: (i, 0)),
            _stacked_spec(g_lead, (1, d), lambda i, j: (0, 0)),
            _stacked_spec(w_lead, (d, tn), lambda i, j: (0, j)),
        ],
        out_specs=pl.BlockSpec((tm, tn), lambda i, j: (i, j)),
        out_shape=jax.ShapeDtypeStruct((t, n), BF16),
        scratch_shapes=[pltpu.VMEM((tm, d), BF16)],
        compiler_params=pltpu.CompilerParams(
            dimension_semantics=("parallel", "arbitrary"), vmem_limit_bytes=VMEM_LIMIT),
        name=name,
    )(x, g, w)


def _gla_proj_kernel(x_ref, g_ref, wm_ref, ws_ref, om_ref, os_ref, h_ref):
    j = pl.program_id(1)

    @pl.when(j == 0)
    def _():
        h_ref[...] = _rmsnorm(x_ref[...], g_ref[...]).astype(BF16)
        os_ref[...] = _dot_nt(h_ref[...], ws_ref[...].astype(BF16)).astype(BF16)

    @pl.when(j > 0)
    def _():
        om_ref[...] = _dot_nt(h_ref[...], wm_ref[...].astype(BF16)).astype(BF16)


def _gla_proj(x, g, g_lead, wt, wt_side, *, tm=1024, tn=768):
    t, d = x.shape
    n_main = GLA_MAIN_N // tn
    blk = lambda j: jnp.maximum(j - 1, 0)
    w_blk = lambda j: jnp.where(j == 0, n_main - 1, j - 1)
    return pl.pallas_call(
        _gla_proj_kernel,
        grid=(t // tm, n_main + 1),
        in_specs=[
            pl.BlockSpec((tm, d), lambda i, j: (i, 0)),
            _stacked_spec(g_lead, (1, d), lambda i, j: (0, 0)),
            pl.BlockSpec((tn, d), lambda i, j: (w_blk(j), 0)),
            pl.BlockSpec((GLA_SIDE_N, d), lambda i, j: (0, 0)),
        ],
        out_specs=[
            pl.BlockSpec((tm, tn), lambda i, j: (i, blk(j))),
            pl.BlockSpec((tm, GLA_SIDE_N), lambda i, j: (i, 0)),
        ],
        out_shape=[jax.ShapeDtypeStruct((t, GLA_MAIN_N), BF16),
                   jax.ShapeDtypeStruct((t, GLA_SIDE_N), BF16)],
        scratch_shapes=[pltpu.VMEM((tm, d), BF16)],
        compiler_params=pltpu.CompilerParams(
            dimension_semantics=("parallel", "arbitrary"), vmem_limit_bytes=VMEM_LIMIT),
        name="gla_proj",
    )(x, g, wt, wt_side)


def _shift_groups(a, n):
    return jnp.concatenate([a[n:], a[:n]], axis=0)


def _block_first_group(a, n):
    return jnp.concatenate([a[g - g % n:g - g % n + 1] for g in range(a.shape[0])], axis=0)


def _gla_kernel(*refs, n_cast):
    ins, (o_ref,), (st_ref,), casts = _split_cast_refs(refs, 6, 1, n_cast)
    qkv_ref, r_ref, gl_ref, w2_ref, b2_ref, hg_ref = ins
    c_len = GLA_CHUNK
    ng = c_len // SUBLANE
    nb = qkv_ref.shape[0]

    @pl.when(pl.program_id(0) == 0)
    def _():
        st_ref[...] = jnp.zeros_like(st_ref)

    r3 = lax.broadcasted_iota(jnp.int32, (ng, SUBLANE, GLA_DKP), 1)
    ti = lax.broadcasted_iota(jnp.int32, (c_len, c_len), 0)
    si = lax.broadcasted_iota(jnp.int32, (c_len, c_len), 1)
    txs = jnp.where(ti > si, ti ^ si, 0)
    to3 = lambda a: a.reshape(ng, SUBLANE, GLA_DKP)
    to2 = lambda a: a.reshape(c_len, GLA_DKP)

    streams = [divmod(s, GLA_HEADS) for s in range(nb * GLA_HEADS)]
    vcols = [pl.ds(hd * GLA_DV, GLA_DV) for _, hd in streams]
    each = lambda f, *lists: [f(*args) for args in zip(*lists)]

    z = [_dot(gl_ref[bi], w2_ref[hd]) + b2_ref[hd] for bi, hd in streams]
    _run_casts(casts)
    qkv = [qkv_ref[bi, :, :2 * GLA_KW + GLA_DKP].astype(F32) for bi in range(nb)]
    lane = lax.broadcasted_iota(jnp.int32, (c_len, GLA_DKP), 1)
    head = lambda a, lo: to3(jnp.where(lane < GLA_DK, a[:, lo:lo + GLA_DKP], 0.0))
    q = [head(qkv[bi], hd * GLA_DK) for bi, hd in streams]
    k = [head(qkv[bi], GLA_KW + hd * GLA_DK) for bi, hd in streams]

    def log2_decay(zs):
        soft = jnp.log2(1.0 + jnp.exp2(jnp.abs(zs) * -LOG2E))
        return to3((jnp.minimum(zs, 0.0) * LOG2E - soft) * (1.0 / GLA_TAU))

    cum = each(log2_decay, z)
    d = 1
    while d < SUBLANE:
        cum = each(lambda a: a + jnp.where(r3 >= d, pltpu.roll(a, d, axis=1), 0.0), cum)
        d *= 2
    tot = each(lambda a: a[:, SUBLANE - 1:SUBLANE, :], cum)
    offs = [[jnp.zeros_like(t[0:1])] for t in tot]
    for g in range(1, ng):
        for o_s, t in zip(offs, tot):
            o_s.append(o_s[-1] + t[g - 1:g])
    cum = each(lambda a, o_s: a + jnp.concatenate(o_s, axis=0), cum, offs)

    nt = lambda a, b: _dot_nt(to2(a).astype(BF16), to2(b).astype(BF16))
    attn = each(lambda a, b: jnp.where(ti == si, nt(a, b), 0.0), q, k)
    g_h = cum
    h = 1
    while h < c_len:
        if h < SUBLANE:
            if h > 1:
                g_h = each(lambda a: jnp.where((r3 & (h // 2)) != 0,
                                               pltpu.roll(a, h // 2, axis=1), a), g_h)
            g_next = each(lambda a: pltpu.roll(a, SUBLANE - h, axis=1), g_h)
        else:
            n = h // SUBLANE
            g_h = each(lambda a: jnp.broadcast_to(a[:, 0:1, :], a.shape), cum)
            if n > 1:
                g_h = each(lambda a: _block_first_group(a, n), g_h)
            g_next = each(lambda a: _shift_groups(a, n), g_h)
        q_h = q if h == 1 else each(lambda a, c, g: a * jnp.exp2(c - g), q, cum, g_h)
        k_h = each(lambda a, c, g: a * jnp.exp2(g - c), k, cum, g_next)
        a_h = each(nt, q_h, k_h)
        attn = each(lambda new, old: jnp.where((txs >= h) & (txs < 2 * h), new, old), a_h, attn)
        h *= 2

    last = each(lambda a: a[ng - 1, SUBLANE - 1:SUBLANE, :], cum)
    q_in = each(lambda a, c: to2(a * jnp.exp2(c)).astype(BF16), q, cum)
    k_out = each(lambda a, c, l: to2(a * jnp.exp2(l - c)).astype(BF16), k, cum, last)
    v = [qkv_ref[bi, :, pl.ds(2 * GLA_KW + hd * GLA_DV, GLA_DV)] for bi, hd in streams]
    st = [st_ref[s] for s in range(len(streams))]
    o = each(lambda qi, s_, a, v_: _dot_nt(qi, s_.astype(BF16)) + _dot(a.astype(BF16), v_),
             q_in, st, attn, v)
    for s, (s_, l, v_, ko) in enumerate(zip(st, last, v, k_out)):
        st_ref[s] = s_ * jnp.exp2(l) + _dot_tn(v_, ko)

    o = each(lambda a: _rmsnorm(a, hg_ref[...], EPS * GLA_DK), o)
    for (bi, _), c, a in zip(streams, vcols, o):
        r = r_ref[bi, :, c].astype(F32)
        o_ref[bi, :, c] = (a * (r * jax.nn.sigmoid(r))).astype(BF16)


def _gla(p_main, p_side, w2, b2, head_g, cast_weights, *, batch, seq):
    t = p_main.shape[0]
    p_main = p_main.reshape(batch, seq, GLA_MAIN_N)
    p_side = p_side.reshape(batch, seq, GLA_SIDE_N)
    c_len = GLA_CHUNK
    steps = seq // c_len
    assert 2 * GLA_KW == TOK_W
    split = 8
    assert split * split == steps
    by_step = lambda i: (i // split, i % split)
    jobs = []
    for w, lead, col_tile in cast_weights:
        if col_tile is None:
            jobs.append((w, lead, (w.shape[-2] // split, w.shape[-1] // split), by_step, None))
        else:
            jobs.append((w, lead, (2 * w.shape[-2] // steps, w.shape[-1] // 2),
                         lambda i: (i // 2, i % 2), col_tile))
    c_in, c_out, c_shape, c_args = _cast_specs(jobs)
    res = pl.pallas_call(
        functools.partial(_gla_kernel, n_cast=len(jobs)),
        grid=(steps,),
        in_specs=[
            pl.BlockSpec((batch, c_len, 2 * TOK_W), lambda i: (0, i, 0)),
            pl.BlockSpec((batch, c_len, TOK_W), lambda i: (0, i, 2)),
            pl.BlockSpec((batch, c_len, GLA_RANKP), lambda i: (0, i, X_W // GLA_RANKP)),
            pl.BlockSpec((GLA_HEADS, GLA_RANKP, GLA_DKP), lambda i: (0, 0, 0)),
            pl.BlockSpec((GLA_HEADS, 1, GLA_DKP), lambda i: (0, 0, 0)),
            pl.BlockSpec((1, GLA_DV), lambda i: (0, 0)),
        ] + c_in,
        out_specs=[pl.BlockSpec((batch, c_len, TOK_W), lambda i: (0, i, 0))] + c_out,
        out_shape=[jax.ShapeDtypeStruct((batch, seq, TOK_W), BF16)] + c_shape,
        scratch_shapes=[pltpu.VMEM((batch * GLA_HEADS, GLA_DV, GLA_DKP), F32)],
        compiler_params=pltpu.CompilerParams(
            dimension_semantics=("arbitrary",), vmem_limit_bytes=VMEM_LIMIT),
        name="gla",
    )(p_main, p_main, p_side, w2, b2, head_g, *c_args)
    return res[0].reshape(t, TOK_W), tuple(res[1:])


def _conv_mixer_kernel(x_ref, g_ref, wb_ref, wc_ref, wx_ref, cw_ref, cb_ref, tok_ref, xq_ref,
                       h_ref, carry_ref, *, tiles_per_seq):
    i = pl.program_id(0)
    j = pl.program_id(1)
    c = jnp.maximum(j - 1, 0)
    tm = x_ref.shape[0]

    @pl.when(j == 0)
    def _():
        h_ref[...] = _rmsnorm(x_ref[...], g_ref[...]).astype(BF16)
        xq_ref[...] = _dot(h_ref[...], wb_ref[...]).astype(BF16)

    @pl.when((j > 0) & (i % tiles_per_seq == 0))
    def _():
        carry_ref[c] = jnp.zeros(carry_ref.shape[1:], F32)

    @pl.when(j > 0)
    def _():
        h = h_ref[...]
        u = _dot(h, wc_ref[...]) * _dot(h, wx_ref[...])
        u_prev = carry_ref[c]
        carry_ref[c] = u[tm - SUBLANE:, :]
        row = lax.broadcasted_iota(jnp.int32, u.shape, 0)
        u1 = jnp.where(row == 0, u_prev[SUBLANE - 1:SUBLANE, :], pltpu.roll(u, 1, axis=0))
        u2 = pltpu.roll(u, 2, axis=0)
        u2 = jnp.where(row == 0, u_prev[SUBLANE - 2:SUBLANE - 1, :], u2)
        u2 = jnp.where(row == 1, u_prev[SUBLANE - 1:SUBLANE, :], u2)
        w = cw_ref[...]
        y = u2 * w[0:1, :] + u1 * w[1:2, :] + u * w[2:3, :] + cb_ref[...]
        tok_ref[...] = (_dot(h, wb_ref[...]) * y).astype(BF16)


def _conv_mixer(x, g, g_lead, w, conv_w, conv_b, lead, *, seq, tm=1024, tw=512):
    t, d = x.shape
    assert tw == X_W
    n_conv = CONV_W // tw
    col = lambda j: jnp.maximum(j - 1, 0)
    return pl.pallas_call(
        functools.partial(_conv_mixer_kernel, tiles_per_seq=seq // tm),
        grid=(t // tm, n_conv + 1),
        in_specs=[
            pl.BlockSpec((tm, d), lambda i, j: (i, 0)),
            _stacked_spec(g_lead, (1, d), lambda i, j: (0, 0)),
            pl.BlockSpec((d, tw), lambda i, j: (0, jnp.where(j == 0, 3 * n_conv, j - 1))),
            pl.BlockSpec((d, tw), lambda i, j: (0, n_conv + col(j))),
            pl.BlockSpec((d, tw), lambda i, j: (0, 2 * n_conv + col(j))),
            _stacked_spec(lead, (CONV_K, tw), lambda i, j: (0, col(j))),
            _stacked_spec(lead, (1, tw), lambda i, j: (0, col(j))),
        ],
        out_specs=[
            pl.BlockSpec((tm, tw), lambda i, j: (i, col(j))),
            pl.BlockSpec((tm, tw), lambda i, j: (i, 0)),
        ],
        out_shape=[jax.ShapeDtypeStruct((t, CONV_W), BF16), jax.ShapeDtypeStruct((t, X_W), BF16)],
        scratch_shapes=[pltpu.VMEM((tm, d), BF16), pltpu.VMEM((n_conv, SUBLANE, tw), F32)],
        compiler_params=pltpu.CompilerParams(
            dimension_semantics=("arbitrary", "arbitrary"), vmem_limit_bytes=VMEM_LIMIT),
        name="conv_mixer",
    )(x, g, w, w, w, conv_w, conv_b)


def _mixout_kernel(tok_ref, xq_ref, kv_ref, wm_ref, x_ref, g_ref, o_ref):
    mixed = _dot(tok_ref[...], wm_ref[:TOK_W, :])
    xo = []
    for h in range(X_HEADS):
        lo = h * X_HEAD_DIM
        qh = xq_ref[:, lo:lo + X_HEAD_DIM]
        kh = kv_ref[:, lo:lo + X_HEAD_DIM]
        vh = kv_ref[:, X_W + lo:X_W + lo + X_HEAD_DIM]
        s = _dot_nt(qh, kh) * (X_HEAD_DIM ** -0.5)
        e = jnp.exp(s - jnp.max(s, axis=-1, keepdims=True))
        pr = (e / jnp.sum(e, axis=-1, keepdims=True)).astype(BF16)
        xo.append(_dot(pr, vh).astype(BF16))
    mixed = mixed + _dot(jnp.concatenate(xo, axis=1), wm_ref[TOK_W:, :])
    o_ref[...] = x_ref[...] + _rmsnorm(mixed, g_ref[...])


def _mixout(tok, p, xq_block, kv, wm, x, g, lead, *, seq, tm=512):
    t, d = x.shape
    per_seq = seq // tm
    return pl.pallas_call(
        _mixout_kernel,
        grid=(t // tm,),
        in_specs=[
            pl.BlockSpec((tm, TOK_W), lambda i: (i, 0)),
            pl.BlockSpec((tm, X_W), lambda i: (i, xq_block)),
            pl.BlockSpec((MEM_LEN, 2 * X_W), lambda i: (i // per_seq, 0)),
            pl.BlockSpec((d, d), lambda i: (0, 0)),
            pl.BlockSpec((tm, d), lambda i: (i, 0)),
            _stacked_spec(lead, (1, d), lambda i: (0, 0)),
        ],
        out_specs=pl.BlockSpec((tm, d), lambda i: (i, 0)),
        out_shape=jax.ShapeDtypeStruct((t, d), F32),
        compiler_params=pltpu.CompilerParams(
            dimension_semantics=("parallel",), vmem_limit_bytes=VMEM_LIMIT),
        name="mixout",
    )(tok, p, kv, wm, x, g)


def _gla_side_weight(wt):
    lo = GLA_MAIN_N
    pad = jnp.zeros((GLA_RANKP - GLA_RANK, wt.shape[1]), wt.dtype)
    return jnp.concatenate([wt[lo + GLA_RANK:], wt[lo:lo + GLA_RANK], pad], axis=0)


def _gla_gate_weights(gate_w2, gate_b):
    w2 = gate_w2.reshape(GLA_RANK, GLA_HEADS, GLA_DK).transpose(1, 0, 2)
    w2 = jnp.pad(w2, ((0, 0), (0, GLA_RANKP - GLA_RANK), (0, GLA_DKP - GLA_DK))).astype(BF16)
    b2 = jnp.pad(gate_b.reshape(GLA_HEADS, 1, GLA_DK), ((0, 0), (0, 0), (0, GLA_DKP - GLA_DK)))
    return w2, b2


def kernel(x, mem, ffn_pre_g, ffn_w_in, ffn_w_out, ffn_post_g, mix_pre_g, mix_post_g, mem_g,
           w_mem_kv, w_mix_out, gla_w_in, gla_gate_w2, gla_gate_b, gla_head_g,
           conv_w_in, conv_w, conv_b):
    batch, seq, d = x.shape
    x = x.reshape(batch * seq, d)
    mem2 = mem.reshape(batch * MEM_LEN, d)
    ffn_pre_g = ffn_pre_g[:, :, None, :]
    ffn_post_g = ffn_post_g[:, :, None, :]
    mix_pre_g = mix_pre_g[:, None, :]
    mix_post_g = mix_post_g[:, None, :]
    mem_g = mem_g[:, None, :]
    conv_b = conv_b[:, None, :]

    ffn_gains = (ffn_pre_g, ffn_post_g)
    ffn_stacks = (ffn_w_in, ffn_w_out)
    assert DEPTH == 2
    nt = batch * seq // 1024
    nf = D_FF // FFN_TF_BF16

    def mem_kv(i):
        return _norm_matmul(mem2, mem_g, (i,), w_mem_kv, (i,),
                            tm=batch * MEM_LEN, tn=2 * X_W, name="mem_kv")

    def ffn_pair_jobs(lead):
        return [_grid_cast_job(ffn_w_in, lead, nt, nf, col_tile=FFN_TF_BF16),
                _grid_cast_job(ffn_w_out, lead, nt, nf, rows_by_j=True)]


    x, _ = _ffn(x, ffn_gains, (0, 0), ffn_stacks, (0, 0), tf=FFN_TF_F32)
    w2, b2 = _gla_gate_weights(gla_gate_w2[0], gla_gate_b[0])
    wt = gla_w_in[0].T
    p, p_side = _gla_proj(x, mix_pre_g, (0,), wt, _gla_side_weight(wt))
    tok, (w01_in, w01_out, wm0, wconv) = _gla(
        p, p_side, w2, b2, gla_head_g[0][None],
        [(ffn_w_in, (0, 1), FFN_TF_BF16), (ffn_w_out, (0, 1), None), (w_mix_out, (0,), None),
         (conv_w_in, (0,), None)],
        batch=batch, seq=seq)
    x = _mixout(tok, p_side, 0, mem_kv(0), wm0, x, mix_post_g, (0,), seq=seq)
    x, (w10_in, w10_out, wm1) = _ffn(
        x, ffn_gains, (0, 1), (w01_in, w01_out), (), tf=FFN_TF_BF16,
        cast_jobs=ffn_pair_jobs((1, 0)) + [_grid_cast_job(w_mix_out, (1,), nt, nt)])

    x, (w11_in, w11_out) = _ffn(x, ffn_gains, (1, 0), (w10_in, w10_out), (), tf=FFN_TF_BF16,
                                cast_jobs=ffn_pair_jobs((1, 1)))
    tok, xq = _conv_mixer(x, mix_pre_g, (1,), wconv, conv_w, conv_b, (0,), seq=seq)
    x = _mixout(tok, xq, 0, mem_kv(1), wm1, x, mix_post_g, (1,), seq=seq)
    x, _ = _ffn(x, ffn_gains, (1, 1), (w11_in, w11_out), (), tf=FFN_TF_BF16)
    return x.reshape(batch, seq, d)
```

```python
import functools

import jax
import jax.numpy as jnp
from jax import lax
from jax.experimental import pallas as pl
from jax.experimental.pallas import tpu as pltpu

F32 = jnp.float32
BF16 = jnp.bfloat16

D_MODEL = 2048
DEPTH = 2
MEM_LEN = 256
TOK_W = 1536
X_HEADS = 4
X_HEAD_DIM = 128
X_W = 512
GLA_HEADS = 4
GLA_DV = 384
GLA_DK = 192
GLA_KW = 768
GLA_RANK = 16
GLA_TAU = 16.0
CONV_W = 1536
CONV_K = 3
D_FF = 5632
EPS = 1e-6
LOG2E = 1.4426950408889634

LANE = 128
SUBLANE = 8
MXU_DIM = 256
VMEM_LIMIT = 56 * 1024 * 1024
VMEM_LIMIT_FFN = 60 * 1024 * 1024
NORM_ROWS = 256
FFN_TF_F32 = 256
FFN_TF_BF16 = 512

GLA_RANKP = LANE
GLA_DKP = MXU_DIM
GLA_CHUNK = 64
GLA_STEP_CHUNKS = 2
GLA_MAIN_N = 2 * GLA_KW + 2 * TOK_W
GLA_SIDE_N = X_W + GLA_RANKP
CONV_NP = 3 * CONV_W + X_W


def _rmsnorm(x, g, eps=EPS):
    ms = jnp.mean(x * x, axis=-1, keepdims=True)
    return x * lax.rsqrt(ms + eps) * g


def _dot(a, b):
    return jnp.dot(a, b, preferred_element_type=F32)


def _dot_nt(a, b):
    return lax.dot_general(a, b, (((1,), (1,)), ((), ())), preferred_element_type=F32)


def _dot_tn(a, b):
    return lax.dot_general(a, b, (((0,), (0,)), ((), ())), preferred_element_type=F32)


def _stacked_spec(lead, block, index_map):
    nlead = len(lead)
    return pl.BlockSpec((None,) * nlead + block, lambda *g: tuple(lead) + tuple(index_map(*g)))


def _for_row_chunks(n_rows, body):
    def step(c, carry):
        body(pl.ds(pl.multiple_of(c * NORM_ROWS, NORM_ROWS), NORM_ROWS))
        return carry

    lax.fori_loop(0, n_rows // NORM_ROWS, step, 0, unroll=2)


def _row_rsqrt_ms(src_ref, rs_ref):
    def rows_rs(rows):
        v = src_ref[rows, :]
        ms = jnp.mean(v * v, axis=-1, keepdims=True)
        rs_ref[rows, :] = jnp.broadcast_to(lax.rsqrt(ms + EPS), (NORM_ROWS, LANE))

    _for_row_chunks(src_ref.shape[0], rows_rs)


def _cast_specs(jobs):
    in_specs = [_stacked_spec(lead, blk, imap) for _, lead, blk, imap in jobs]
    out_specs = [pl.BlockSpec(blk, imap) for _, _, blk, imap in jobs]
    out_shape = [jax.ShapeDtypeStruct(w.shape[len(lead):], BF16) for w, lead, _, _ in jobs]
    return in_specs, out_specs, out_shape, [w for w, _, _, _ in jobs]


def _split_cast_refs(refs, n_in, n_out, n_cast):
    ins, refs = refs[:n_in], refs[n_in:]
    cast_in, refs = refs[:n_cast], refs[n_cast:]
    outs, refs = refs[:n_out], refs[n_out:]
    cast_out, scratch = refs[:n_cast], refs[n_cast:]
    return ins, outs, scratch, list(zip(cast_in, cast_out))


def _run_casts(casts):
    for src, dst in casts:
        dst[...] = src[...].astype(BF16)


def _ffn_kernel(*refs, n_cast):
    ins, (o_ref,), (h_ref, rs_ref), casts = _split_cast_refs(refs, 6, 1, n_cast)
    x_ref, pre_g_ref, wg_ref, wu_ref, wo_ref, post_g_ref = ins
    j = pl.program_id(1)
    tm = x_ref.shape[0]

    @pl.when(j == 0)
    def _():
        def rows_in(rows):
            h_ref[rows, :] = _rmsnorm(x_ref[rows, :], pre_g_ref[...]).astype(BF16)
            o_ref[rows, :] = jnp.zeros((NORM_ROWS, o_ref.shape[1]), F32)

        _for_row_chunks(tm, rows_in)

    h = h_ref[...]
    gate = _dot(h, wg_ref[...].astype(BF16))
    up = _dot(h, wu_ref[...].astype(BF16))
    act = (gate * jax.nn.sigmoid(gate) * up).astype(BF16)
    _run_casts(casts)
    o_ref[...] += _dot(act, wo_ref[...].astype(BF16))

    @pl.when(j == pl.num_programs(1) - 1)
    def _():
        _row_rsqrt_ms(o_ref, rs_ref)
        half_g = 0.5 * post_g_ref[...]

        def rows_out(rows):
            o_ref[rows, :] = x_ref[rows, :] + o_ref[rows, :] * rs_ref[rows, 0:1] * half_g

        _for_row_chunks(tm, rows_out)


def _grid_cast_job(w, lead, n_i, n_j, rows_by_j=False):
    if rows_by_j:
        blk = (w.shape[-2] // n_j, w.shape[-1] // n_i)
        return (w, lead, blk, lambda i, j: (jnp.minimum(j, n_j - 1), i))
    blk = (w.shape[-2] // n_i, w.shape[-1] // n_j)
    return (w, lead, blk, lambda i, j: (i, jnp.minimum(j, n_j - 1)))


def _ffn(x, gains, g_lead, weights, w_lead, *, tf, tm=1024, cast_jobs=()):
    t, d = x.shape
    nt, nf = t // tm, D_FF // tf
    pre_g, post_g = gains
    w_in, w_out = weights
    c_in, c_out, c_shape, c_args = _cast_specs(cast_jobs)
    res = pl.pallas_call(
        functools.partial(_ffn_kernel, n_cast=len(cast_jobs)),
        grid=(nt, nf),
        in_specs=[
            pl.BlockSpec((tm, d), lambda i, j: (i, 0)),
            _stacked_spec(g_lead, (1, d), lambda i, j: (0, 0)),
            _stacked_spec(w_lead, (d, tf), lambda i, j: (0, j)),
            _stacked_spec(w_lead, (d, tf), lambda i, j: (0, j + nf)),
            _stacked_spec(w_lead, (tf, d), lambda i, j: (j, 0)),
            _stacked_spec(g_lead, (1, d), lambda i, j: (0, 0)),
        ] + c_in,
        out_specs=[pl.BlockSpec((tm, d), lambda i, j: (i, 0))] + c_out,
        out_shape=[jax.ShapeDtypeStruct((t, d), F32)] + c_shape,
        scratch_shapes=[pltpu.VMEM((tm, d), BF16), pltpu.VMEM((tm, LANE), F32)],
        compiler_params=pltpu.CompilerParams(
            dimension_semantics=("parallel", "arbitrary"), vmem_limit_bytes=VMEM_LIMIT_FFN),
        name="ffn",
    )(x, pre_g, w_in, w_in, w_out, post_g, *c_args)
    return res[0], tuple(res[1:])


def _norm_matmul_kernel(x_ref, g_ref, w_ref, o_ref, h_ref):
    @pl.when(pl.program_id(1) == 0)
    def _():
        h_ref[...] = _rmsnorm(x_ref[...], g_ref[...]).astype(BF16)

    o_ref[...] = _dot(h_ref[...], w_ref[...].astype(BF16)).astype(o_ref.dtype)


def _norm_matmul(x, g, g_lead, w, w_lead, *, tm, tn, name):
    t, d = x.shape
    n = w.shape[-1]
    return pl.pallas_call(
        _norm_matmul_kernel,
        grid=(t // tm, n // tn),
        in_specs=[
            pl.BlockSpec((tm, d), lambda i, j: (i, 0)),
            _stacked_spec(g_lead, (1, d), lambda i, j: (0, 0)),
            _stacked_spec(w_lead, (d, tn), lambda i, j: (0, j)),
        ],
        out_specs=pl.BlockSpec((tm, tn), lambda i, j: (i, j)),
        out_shape=jax.ShapeDtypeStruct((t, n), BF16),
        scratch_shapes=[pltpu.VMEM((tm, d), BF16)],
        compiler_params=pltpu.CompilerParams(
            dimension_semantics=("parallel", "arbitrary"), vmem_limit_bytes=VMEM_LIMIT),
        name=name,
    )(x, g, w)


def _gla_proj_kernel(x_ref, g_ref, wm_ref, ws_ref, om_ref, os_ref, h_ref):
    j = pl.program_id(1)

    @pl.when(j == 0)
    def _():
        h_ref[...] = _rmsnorm(x_ref[...], g_ref[...]).astype(BF16)
        os_ref[...] = _dot_nt(h_ref[...], ws_ref[...].astype(BF16)).astype(BF16)

    @pl.when(j > 0)
    def _():
        om_ref[...] = _dot_nt(h_ref[...], wm_ref[...].astype(BF16)).astype(BF16)


def _gla_proj(x, g, g_lead, wt, wt_side, *, tm=1024, tn=768):
    t, d = x.shape
    n_main = GLA_MAIN_N // tn
    blk = lambda j: jnp.maximum(j - 1, 0)
    w_blk = lambda j: jnp.where(j == 0, n_main - 1, j - 1)
    return pl.pallas_call(
        _gla_proj_kernel,
        grid=(t // tm, n_main + 1),
        in_specs=[
            pl.BlockSpec((tm, d), lambda i, j: (i, 0)),
            _stacked_spec(g_lead, (1, d), lambda i, j: (0, 0)),
            pl.BlockSpec((tn, d), lambda i, j: (w_blk(j), 0)),
            pl.BlockSpec((GLA_SIDE_N, d), lambda i, j: (0, 0)),
        ],
        out_specs=[
            pl.BlockSpec((tm, tn), lambda i, j: (i, blk(j))),
            pl.BlockSpec((tm, GLA_SIDE_N), lambda i, j: (i, 0)),
        ],
        out_shape=[jax.ShapeDtypeStruct((t, GLA_MAIN_N), BF16),
                   jax.ShapeDtypeStruct((t, GLA_SIDE_N), BF16)],
        scratch_shapes=[pltpu.VMEM((tm, d), BF16)],
        compiler_params=pltpu.CompilerParams(
            dimension_semantics=("parallel", "arbitrary"), vmem_limit_bytes=VMEM_LIMIT),
        name="gla_proj",
    )(x, g, wt, wt_side)


def _shift_groups(a, n):
    return jnp.concatenate([a[n:], a[:n]], axis=0)


def _block_first_group(a, n):
    return jnp.concatenate([a[g - g % n:g - g % n + 1] for g in range(a.shape[0])], axis=0)


def _gla_kernel(*refs, n_cast):
    ins, (o_ref,), (st_ref,), casts = _split_cast_refs(refs, 6, 1, n_cast)
    qkv_ref, r_ref, gl_ref, w2_ref, b2_ref, hg_ref = ins
    c_len = GLA_CHUNK
    ng = c_len // SUBLANE
    nb = qkv_ref.shape[0]
    nck = qkv_ref.shape[1] // c_len

    @pl.when(pl.program_id(0) == 0)
    def _():
        st_ref[...] = jnp.zeros_like(st_ref)

    r3 = lax.broadcasted_iota(jnp.int32, (ng, SUBLANE, GLA_DKP), 1)
    ti = lax.broadcasted_iota(jnp.int32, (c_len, c_len), 0)
    si = lax.broadcasted_iota(jnp.int32, (c_len, c_len), 1)
    txs = jnp.where(ti > si, ti ^ si, 0)
    to3 = lambda a: a.reshape(ng, SUBLANE, GLA_DKP)
    to2 = lambda a: a.reshape(c_len, GLA_DKP)

    heads = [divmod(s, GLA_HEADS) for s in range(nb * GLA_HEADS)]
    streams = [(bi, hd, ck) for ck in range(nck) for bi, hd in heads]
    crow = [pl.ds(ck * c_len, c_len) for ck in range(nck)]
    each = lambda f, *lists: [f(*args) for args in zip(*lists)]

    z = [_dot(gl_ref[bi, crow[ck], :], w2_ref[hd]) + b2_ref[hd] for bi, hd, ck in streams]
    _run_casts(casts)
    qkv = {(bi, ck): qkv_ref[bi, crow[ck], :2 * GLA_KW + GLA_DKP].astype(F32)
           for bi in range(nb) for ck in range(nck)}
    lane = lax.broadcasted_iota(jnp.int32, (c_len, GLA_DKP), 1)
    head = lambda a, lo: to3(jnp.where(lane < GLA_DK, a[:, lo:lo + GLA_DKP], 0.0))
    q = [head(qkv[bi, ck], hd * GLA_DK) for bi, hd, ck in streams]
    k = [head(qkv[bi, ck], GLA_KW + hd * GLA_DK) for bi, hd, ck in streams]

    def log2_decay(zs):
        soft = jnp.log2(1.0 + jnp.exp2(jnp.abs(zs) * -LOG2E))
        return to3((jnp.minimum(zs, 0.0) * LOG2E - soft) * (1.0 / GLA_TAU))

    cum = each(log2_decay, z)
    d = 1
    while d < SUBLANE:
        cum = each(lambda a: a + jnp.where(r3 >= d, pltpu.roll(a, d, axis=1), 0.0), cum)
        d *= 2
    tot = each(lambda a: a[:, SUBLANE - 1:SUBLANE, :], cum)
    offs = [[jnp.zeros_like(t[0:1])] for t in tot]
    for g in range(1, ng):
        for o_s, t in zip(offs, tot):
            o_s.append(o_s[-1] + t[g - 1:g])
    cum = each(lambda a, o_s: a + jnp.concatenate(o_s, axis=0), cum, offs)

    nt = lambda a, b: _dot_nt(to2(a).astype(BF16), to2(b).astype(BF16))
    attn = each(lambda a, b: jnp.where(ti == si, nt(a, b), 0.0), q, k)
    g_h = cum
    h = 1
    while h < c_len:
        if h < SUBLANE:
            if h > 1:
                g_h = each(lambda a: jnp.where((r3 & (h // 2)) != 0,
                                               pltpu.roll(a, h // 2, axis=1), a), g_h)
            g_next = each(lambda a: pltpu.roll(a, SUBLANE - h, axis=1), g_h)
        else:
            n = h // SUBLANE
            g_h = each(lambda a: jnp.broadcast_to(a[:, 0:1, :], a.shape), cum)
            if n > 1:
                g_h = each(lambda a: _block_first_group(a, n), g_h)
            g_next = each(lambda a: _shift_groups(a, n), g_h)
        q_h = q if h == 1 else each(lambda a, c, g: a * jnp.exp2(c - g), q, cum, g_h)
        k_h = each(lambda a, c, g: a * jnp.exp2(g - c), k, cum, g_next)
        a_h = each(nt, q_h, k_h)
        attn = each(lambda new, old: jnp.where((txs >= h) & (txs < 2 * h), new, old), a_h, attn)
        h *= 2

    last = each(lambda a: a[ng - 1, SUBLANE - 1:SUBLANE, :], cum)
    q_in = each(lambda a, c: to2(a * jnp.exp2(c)).astype(BF16), q, cum)
    k_out = each(lambda a, c, l: to2(a * jnp.exp2(l - c)).astype(BF16), k, cum, last)
    v = [qkv_ref[bi, crow[ck], pl.ds(2 * GLA_KW + hd * GLA_DV, GLA_DV)] for bi, hd, ck in streams]

    st = [st_ref[s] for s in range(len(heads))]
    o = []
    for ck in range(nck):
        of = slice(ck * len(heads), (ck + 1) * len(heads))
        o += each(lambda qi, s_, a, v_: _dot_nt(qi, s_.astype(BF16)) + _dot(a.astype(BF16), v_),
                  q_in[of], st, attn[of], v[of])
        st = each(lambda s_, l, v_, ko: s_ * jnp.exp2(l) + _dot_tn(v_, ko),
                  st, last[of], v[of], k_out[of])
    for s, s_ in enumerate(st):
        st_ref[s] = s_

    o = each(lambda a: _rmsnorm(a, hg_ref[...], EPS * GLA_DK), o)
    for (bi, hd, ck), a in zip(streams, o):
        c = pl.ds(hd * GLA_DV, GLA_DV)
        r = r_ref[bi, crow[ck], c].astype(F32)
        o_ref[bi, crow[ck], c] = (a * (r * jax.nn.sigmoid(r))).astype(BF16)


def _gla(p_main, p_side, w2, b2, head_g, cast_weights, *, batch, seq):
    t = p_main.shape[0]
    p_main = p_main.reshape(batch, seq, GLA_MAIN_N)
    p_side = p_side.reshape(batch, seq, GLA_SIDE_N)
    rows = GLA_CHUNK * GLA_STEP_CHUNKS
    steps = seq // rows
    assert 2 * GLA_KW == TOK_W
    c_split = 4
    r_split = steps // c_split
    by_step = lambda i: (i // c_split, i % c_split)
    jobs = [(w, lead, (w.shape[-2] // r_split, w.shape[-1] // c_split), by_step)
            for w, lead in cast_weights]
    c_in, c_out, c_shape, c_args = _cast_specs(jobs)
    res = pl.pallas_call(
        functools.partial(_gla_kernel, n_cast=len(jobs)),
        grid=(steps,),
        in_specs=[
            pl.BlockSpec((batch, rows, 2 * TOK_W), lambda i: (0, i, 0)),
            pl.BlockSpec((batch, rows, TOK_W), lambda i: (0, i, 2)),
            pl.BlockSpec((batch, rows, GLA_RANKP), lambda i: (0, i, X_W // GLA_RANKP)),
            pl.BlockSpec((GLA_HEADS, GLA_RANKP, GLA_DKP), lambda i: (0, 0, 0)),
            pl.BlockSpec((GLA_HEADS, 1, GLA_DKP), lambda i: (0, 0, 0)),
            pl.BlockSpec((1, GLA_DV), lambda i: (0, 0)),
        ] + c_in,
        out_specs=[pl.BlockSpec((batch, rows, TOK_W), lambda i: (0, i, 0))] + c_out,
        out_shape=[jax.ShapeDtypeStruct((batch, seq, TOK_W), BF16)] + c_shape,
        scratch_shapes=[pltpu.VMEM((batch * GLA_HEADS, GLA_DV, GLA_DKP), F32)],
        compiler_params=pltpu.CompilerParams(
            dimension_semantics=("arbitrary",), vmem_limit_bytes=VMEM_LIMIT),
        name="gla",
    )(p_main, p_main, p_side, w2, b2, head_g, *c_args)
    return res[0].reshape(t, TOK_W), tuple(res[1:])


def _conv_mixer_kernel(x_ref, g_ref, wb_ref, wc_ref, wx_ref, cw_ref, cb_ref, tok_ref, xq_ref,
                       h_ref, carry_ref, *, tiles_per_seq):
    i = pl.program_id(0)
    j = pl.program_id(1)
    c = jnp.maximum(j - 1, 0)
    tm = x_ref.shape[0]

    @pl.when(j == 0)
    def _():
        h_ref[...] = _rmsnorm(x_ref[...], g_ref[...]).astype(BF16)
        xq_ref[...] = _dot(h_ref[...], wb_ref[...]).astype(BF16)

    @pl.when((j > 0) & (i % tiles_per_seq == 0))
    def _():
        carry_ref[c] = jnp.zeros(carry_ref.shape[1:], F32)

    @pl.when(j > 0)
    def _():
        h = h_ref[...]
        u = _dot(h, wc_ref[...]) * _dot(h, wx_ref[...])
        u_prev = carry_ref[c]
        carry_ref[c] = u[tm - SUBLANE:, :]
        row = lax.broadcasted_iota(jnp.int32, u.shape, 0)
        u1 = jnp.where(row == 0, u_prev[SUBLANE - 1:SUBLANE, :], pltpu.roll(u, 1, axis=0))
        u2 = pltpu.roll(u, 2, axis=0)
        u2 = jnp.where(row == 0, u_prev[SUBLANE - 2:SUBLANE - 1, :], u2)
        u2 = jnp.where(row == 1, u_prev[SUBLANE - 1:SUBLANE, :], u2)
        w = cw_ref[...]
        y = u2 * w[0:1, :] + u1 * w[1:2, :] + u * w[2:3, :] + cb_ref[...]
        tok_ref[...] = (_dot(h, wb_ref[...]) * y).astype(BF16)


def _conv_mixer(x, g, g_lead, w, conv_w, conv_b, lead, *, seq, tm=1024, tw=512):
    t, d = x.shape
    assert tw == X_W
    n_conv = CONV_W // tw
    col = lambda j: jnp.maximum(j - 1, 0)
    return pl.pallas_call(
        functools.partial(_conv_mixer_kernel, tiles_per_seq=seq // tm),
        grid=(t // tm, n_conv + 1),
        in_specs=[
            pl.BlockSpec((tm, d), lambda i, j: (i, 0)),
            _stacked_spec(g_lead, (1, d), lambda i, j: (0, 0)),
            pl.BlockSpec((d, tw), lambda i, j: (0, jnp.where(j == 0, 3 * n_conv, j - 1))),
            pl.BlockSpec((d, tw), lambda i, j: (0, n_conv + col(j))),
            pl.BlockSpec((d, tw), lambda i, j: (0, 2 * n_conv + col(j))),
            _stacked_spec(lead, (CONV_K, tw), lambda i, j: (0, col(j))),
            _stacked_spec(lead, (1, tw), lambda i, j: (0, col(j))),
        ],
        out_specs=[
            pl.BlockSpec((tm, tw), lambda i, j: (i, col(j))),
            pl.BlockSpec((tm, tw), lambda i, j: (i, 0)),
        ],
        out_shape=[jax.ShapeDtypeStruct((t, CONV_W), BF16), jax.ShapeDtypeStruct((t, X_W), BF16)],
        scratch_shapes=[pltpu.VMEM((tm, d), BF16), pltpu.VMEM((n_conv, SUBLANE, tw), F32)],
        compiler_params=pltpu.CompilerParams(
            dimension_semantics=("arbitrary", "arbitrary"), vmem_limit_bytes=VMEM_LIMIT),
        name="conv_mixer",
    )(x, g, w, w, w, conv_w, conv_b)


def _mixout_kernel(tok_ref, xq_ref, kv_ref, wm_ref, x_ref, g_ref, o_ref):
    mixed = _dot(tok_ref[...], wm_ref[:TOK_W, :])
    xo = []
    for h in range(X_HEADS):
        lo = h * X_HEAD_DIM
        qh = xq_ref[:, lo:lo + X_HEAD_DIM]
        kh = kv_ref[:, lo:lo + X_HEAD_DIM]
        vh = kv_ref[:, X_W + lo:X_W + lo + X_HEAD_DIM]
        s = _dot_nt(qh, kh) * (X_HEAD_DIM ** -0.5)
        e = jnp.exp(s - jnp.max(s, axis=-1, keepdims=True))
        pr = (e / jnp.sum(e, axis=-1, keepdims=True)).astype(BF16)
        xo.append(_dot(pr, vh).astype(BF16))
    mixed = mixed + _dot(jnp.concatenate(xo, axis=1), wm_ref[TOK_W:, :])
    o_ref[...] = x_ref[...] + _rmsnorm(mixed, g_ref[...])


def _mixout(tok, p, xq_block, kv, wm, x, g, lead, *, seq, tm=512):
    t, d = x.shape
    per_seq = seq // tm
    return pl.pallas_call(
        _mixout_kernel,
        grid=(t // tm,),
        in_specs=[
            pl.BlockSpec((tm, TOK_W), lambda i: (i, 0)),
            pl.BlockSpec((tm, X_W), lambda i: (i, xq_block)),
            pl.BlockSpec((MEM_LEN, 2 * X_W), lambda i: (i // per_seq, 0)),
            pl.BlockSpec((d, d), lambda i: (0, 0)),
            pl.BlockSpec((tm, d), lambda i: (i, 0)),
            _stacked_spec(lead, (1, d), lambda i: (0, 0)),
        ],
        out_specs=pl.BlockSpec((tm, d), lambda i: (i, 0)),
        out_shape=jax.ShapeDtypeStruct((t, d), F32),
        compiler_params=pltpu.CompilerParams(
            dimension_semantics=("parallel",), vmem_limit_bytes=VMEM_LIMIT),
        name="mixout",
    )(tok, p, kv, wm, x, g)


def _gla_side_weight(wt):
    lo = GLA_MAIN_N
    pad = jnp.zeros((GLA_RANKP - GLA_RANK, wt.shape[1]), wt.dtype)
    return jnp.concatenate([wt[lo + GLA_RANK:], wt[lo:lo + GLA_RANK], pad], axis=0)


def _gla_gate_weights(gate_w2, gate_b):
    w2 = gate_w2.reshape(GLA_RANK, GLA_HEADS, GLA_DK).transpose(1, 0, 2)
    w2 = jnp.pad(w2, ((0, 0), (0, GLA_RANKP - GLA_RANK), (0, GLA_DKP - GLA_DK))).astype(BF16)
    b2 = jnp.pad(gate_b.reshape(GLA_HEADS, 1, GLA_DK), ((0, 0), (0, 0), (0, GLA_DKP - GLA_DK)))
    return w2, b2


def kernel(x, mem, ffn_pre_g, ffn_w_in, ffn_w_out, ffn_post_g, mix_pre_g, mix_post_g, mem_g,
           w_mem_kv, w_mix_out, gla_w_in, gla_gate_w2, gla_gate_b, gla_head_g,
           conv_w_in, conv_w, conv_b):
    batch, seq, d = x.shape
    x = x.reshape(batch * seq, d)
    mem2 = mem.reshape(batch * MEM_LEN, d)
    ffn_pre_g = ffn_pre_g[:, :, None, :]
    ffn_post_g = ffn_post_g[:, :, None, :]
    mix_pre_g = mix_pre_g[:, None, :]
    mix_post_g = mix_post_g[:, None, :]
    mem_g = mem_g[:, None, :]
    conv_b = conv_b[:, None, :]

    ffn_gains = (ffn_pre_g, ffn_post_g)
    ffn_stacks = (ffn_w_in, ffn_w_out)
    assert DEPTH == 2
    nt = batch * seq // 1024
    nf = D_FF // FFN_TF_BF16

    def mem_kv(i):
        return _norm_matmul(mem2, mem_g, (i,), w_mem_kv, (i,),
                            tm=batch * MEM_LEN, tn=2 * X_W, name="mem_kv")

    def ffn_pair_jobs(lead):
        return [_grid_cast_job(ffn_w_in, lead, nt, nf),
                _grid_cast_job(ffn_w_out, lead, nt, nf, rows_by_j=True)]


    x, _ = _ffn(x, ffn_gains, (0, 0), ffn_stacks, (0, 0), tf=FFN_TF_F32)
    w2, b2 = _gla_gate_weights(gla_gate_w2[0], gla_gate_b[0])
    wt = gla_w_in[0].T
    p, p_side = _gla_proj(x, mix_pre_g, (0,), wt, _gla_side_weight(wt))
    tok, (w01_in, w01_out, wm0, wconv) = _gla(
        p, p_side, w2, b2, gla_head_g[0][None],
        [(ffn_w_in, (0, 1)), (ffn_w_out, (0, 1)), (w_mix_out, (0,)), (conv_w_in, (0,))],
        batch=batch, seq=seq)
    x = _mixout(tok, p_side, 0, mem_kv(0), wm0, x, mix_post_g, (0,), seq=seq)
    x, (w10_in, w10_out, wm1) = _ffn(
        x, ffn_gains, (0, 1), (w01_in, w01_out), (), tf=FFN_TF_BF16,
        cast_jobs=ffn_pair_jobs((1, 0)) + [_grid_cast_job(w_mix_out, (1,), nt, nt)])

    x, (w11_in, w11_out) = _ffn(x, ffn_gains, (1, 0), (w10_in, w10_out), (), tf=FFN_TF_BF16,
                                cast_jobs=ffn_pair_jobs((1, 1)))
    tok, xq = _conv_mixer(x, mix_pre_g, (1,), wconv, conv_w, conv_b, (0,), seq=seq)
    x = _mixout(tok, xq, 0, mem_kv(1), wm1, x, mix_post_g, (1,), seq=seq)
    x, _ = _ffn(x, ffn_gains, (1, 1), (w11_in, w11_out), (), tf=FFN_TF_BF16)
    return x.reshape(batch, seq, d)
```

```python
import functools

import jax
import jax.numpy as jnp
from jax import lax
from jax.experimental import pallas as pl
from jax.experimental.pallas import tpu as pltpu

F32 = jnp.float32
BF16 = jnp.bfloat16

D_MODEL = 2048
DEPTH = 2
MEM_LEN = 256
TOK_W = 1536
X_HEADS = 4
X_HEAD_DIM = 128
X_W = 512
GLA_HEADS = 4
GLA_DV = 384
GLA_DK = 192
GLA_KW = 768
GLA_RANK = 16
GLA_TAU = 16.0
CONV_W = 1536
CONV_K = 3
D_FF = 5632
EPS = 1e-6
LOG2E = 1.4426950408889634

LANE = 128
SUBLANE = 8
MXU_DIM = 256
VMEM_LIMIT = 56 * 1024 * 1024
VMEM_LIMIT_FFN = 60 * 1024 * 1024
NORM_ROWS = 256
FFN_TF_F32 = 256
FFN_TF_BF16 = 512

GLA_RANKP = LANE
GLA_DKP = MXU_DIM
GLA_CHUNK = 64
GLA_STEP_CHUNKS = 2
GLA_MAIN_N = 2 * GLA_KW + 2 * TOK_W
GLA_SIDE_N = X_W + GLA_RANKP
CONV_NP = 3 * CONV_W + X_W


def _rmsnorm(x, g, eps=EPS):
    ms = jnp.mean(x * x, axis=-1, keepdims=True)
    return x * lax.rsqrt(ms + eps) * g


def _dot(a, b):
    return jnp.dot(a, b, preferred_element_type=F32)


def _dot_nt(a, b):
    return lax.dot_general(a, b, (((1,), (1,)), ((), ())), preferred_element_type=F32)


def _dot_tn(a, b):
    return lax.dot_general(a, b, (((0,), (0,)), ((), ())), preferred_element_type=F32)


def _stacked_spec(lead, block, index_map):
    nlead = len(lead)
    return pl.BlockSpec((None,) * nlead + block, lambda *g: tuple(lead) + tuple(index_map(*g)))


def _for_row_chunks(n_rows, body):
    def step(c, carry):
        body(pl.ds(pl.multiple_of(c * NORM_ROWS, NORM_ROWS), NORM_ROWS))
        return carry

    lax.fori_loop(0, n_rows // NORM_ROWS, step, 0, unroll=2)


def _row_rsqrt_ms(src_ref, rs_ref):
    def rows_rs(rows):
        v = src_ref[rows, :]
        ms = jnp.mean(v * v, axis=-1, keepdims=True)
        rs_ref[rows, :] = jnp.broadcast_to(lax.rsqrt(ms + EPS), (NORM_ROWS, LANE))

    _for_row_chunks(src_ref.shape[0], rows_rs)


def _cast_specs(jobs):
    in_specs = [_stacked_spec(lead, blk, imap) for _, lead, blk, imap in jobs]
    out_specs = [pl.BlockSpec(blk, imap) for _, _, blk, imap in jobs]
    out_shape = [jax.ShapeDtypeStruct(w.shape[len(lead):], BF16) for w, lead, _, _ in jobs]
    return in_specs, out_specs, out_shape, [w for w, _, _, _ in jobs]


def _split_cast_refs(refs, n_in, n_out, n_cast):
    ins, refs = refs[:n_in], refs[n_in:]
    cast_in, refs = refs[:n_cast], refs[n_cast:]
    outs, refs = refs[:n_out], refs[n_out:]
    cast_out, scratch = refs[:n_cast], refs[n_cast:]
    return ins, outs, scratch, list(zip(cast_in, cast_out))


def _run_casts(casts):
    for src, dst in casts:
        dst[...] = src[...].astype(BF16)


def _ffn_kernel(*refs, n_cast):
    ins, (o_ref,), (h_ref, rs_ref), casts = _split_cast_refs(refs, 6, 1, n_cast)
    x_ref, pre_g_ref, wg_ref, wu_ref, wo_ref, post_g_ref = ins
    j = pl.program_id(1)
    tm = x_ref.shape[0]

    @pl.when(j == 0)
    def _():
        def rows_in(rows):
            h_ref[rows, :] = _rmsnorm(x_ref[rows, :], pre_g_ref[...]).astype(BF16)
            o_ref[rows, :] = jnp.zeros((NORM_ROWS, o_ref.shape[1]), F32)

        _for_row_chunks(tm, rows_in)

    h = h_ref[...]
    gate = _dot(h, wg_ref[...].astype(BF16))
    up = _dot(h, wu_ref[...].astype(BF16))
    act = (gate * jax.nn.sigmoid(gate) * up).astype(BF16)
    _run_casts(casts)
    o_ref[...] += _dot(act, wo_ref[...].astype(BF16))

    @pl.when(j == pl.num_programs(1) - 1)
    def _():
        _row_rsqrt_ms(o_ref, rs_ref)
        half_g = 0.5 * post_g_ref[...]

        def rows_out(rows):
            o_ref[rows, :] = x_ref[rows, :] + o_ref[rows, :] * rs_ref[rows, 0:1] * half_g

        _for_row_chunks(tm, rows_out)


def _grid_cast_job(w, lead, n_i, n_j, rows_by_j=False):
    if rows_by_j:
        blk = (w.shape[-2] // n_j, w.shape[-1] // n_i)
        return (w, lead, blk, lambda i, j: (jnp.minimum(j, n_j - 1), i))
    blk = (w.shape[-2] // n_i, w.shape[-1] // n_j)
    return (w, lead, blk, lambda i, j: (i, jnp.minimum(j, n_j - 1)))


def _ffn(x, gains, g_lead, weights, w_lead, *, tf, tm=1024, cast_jobs=()):
    t, d = x.shape
    nt, nf = t // tm, D_FF // tf
    pre_g, post_g = gains
    w_in, w_out = weights
    c_in, c_out, c_shape, c_args = _cast_specs(cast_jobs)
    res = pl.pallas_call(
        functools.partial(_ffn_kernel, n_cast=len(cast_jobs)),
        grid=(nt, nf),
        in_specs=[
            pl.BlockSpec((tm, d), lambda i, j: (i, 0)),
            _stacked_spec(g_lead, (1, d), lambda i, j: (0, 0)),
            _stacked_spec(w_lead, (d, tf), lambda i, j: (0, j)),
            _stacked_spec(w_lead, (d, tf), lambda i, j: (0, j + nf)),
            _stacked_spec(w_lead, (tf, d), lambda i, j: (j, 0)),
            _stacked_spec(g_lead, (1, d), lambda i, j: (0, 0)),
        ] + c_in,
        out_specs=[pl.BlockSpec((tm, d), lambda i, j: (i, 0))] + c_out,
        out_shape=[jax.ShapeDtypeStruct((t, d), F32)] + c_shape,
        scratch_shapes=[pltpu.VMEM((tm, d), BF16), pltpu.VMEM((tm, LANE), F32)],
        compiler_params=pltpu.CompilerParams(
            dimension_semantics=("parallel", "arbitrary"), vmem_limit_bytes=VMEM_LIMIT_FFN),
        name="ffn",
    )(x, pre_g, w_in, w_in, w_out, post_g, *c_args)
    return res[0], tuple(res[1:])


def _norm_matmul_kernel(x_ref, g_ref, w_ref, o_ref, h_ref):
    @pl.when(pl.program_id(1) == 0)
    def _():
        h_ref[...] = _rmsnorm(x_ref[...], g_ref[...]).astype(BF16)

    o_ref[...] = _dot(h_ref[...], w_ref[...].astype(BF16)).astype(o_ref.dtype)


def _norm_matmul(x, g, g_lead, w, w_lead, *, tm, tn, name):
    t, d = x.shape
    n = w.shape[-1]
    return pl.pallas_call(
        _norm_matmul_kernel,
        grid=(t // tm, n // tn),
        in_specs=[
            pl.BlockSpec((tm, d), lambda i, j: (i, 0)),
            _stacked_spec(g_lead, (1, d), lambda i, j: (0, 0)),
            _stacked_spec(w_lead, (d, tn), lambda i, j: (0, j)),
        ],
        out_specs=pl.BlockSpec((tm, tn), lambda i, j: (i, j)),
        out_shape=jax.ShapeDtypeStruct((t, n), BF16),
        scratch_shapes=[pltpu.VMEM((tm, d), BF16)],
        compiler_params=pltpu.CompilerParams(
            dimension_semantics=("parallel", "arbitrary"), vmem_limit_bytes=VMEM_LIMIT),
        name=name,
    )(x, g, w)


def _gla_proj_kernel(x_ref, g_ref, wm_ref, ws_ref, om_ref, os_ref, h_ref):
    j = pl.program_id(1)

    @pl.when(j == 0)
    def _():
        h_ref[...] = _rmsnorm(x_ref[...], g_ref[...]).astype(BF16)
        os_ref[...] = _dot_nt(h_ref[...], ws_ref[...].astype(BF16)).astype(BF16)

    @pl.when(j > 0)
    def _():
        om_ref[...] = _dot_nt(h_ref[...], wm_ref[...].astype(BF16)).astype(BF16)


def _gla_proj(x, g, g_lead, wt, wt_side, *, tm=1024, tn=768):
    t, d = x.shape
    n_main = GLA_MAIN_N // tn
    blk = lambda j: jnp.maximum(j - 1, 0)
    w_blk = lambda j: jnp.where(j == 0, n_main - 1, j - 1)
    return pl.pallas_call(
        _gla_proj_kernel,
        grid=(t // tm, n_main + 1),
        in_specs=[
            pl.BlockSpec((tm, d), lambda i, j: (i, 0)),
            _stacked_spec(g_lead, (1, d), lambda i, j: (0, 0)),
            pl.BlockSpec((tn, d), lambda i, j: (w_blk(j), 0)),
            pl.BlockSpec((GLA_SIDE_N, d), lambda i, j: (0, 0)),
        ],
        out_specs=[
            pl.BlockSpec((tm, tn), lambda i, j: (i, blk(j))),
            pl.BlockSpec((tm, GLA_SIDE_N), lambda i, j: (i, 0)),
        ],
        out_shape=[jax.ShapeDtypeStruct((t, GLA_MAIN_N), BF16),
                   jax.ShapeDtypeStruct((t, GLA_SIDE_N), BF16)],
        scratch_shapes=[pltpu.VMEM((tm, d), BF16)],
        compiler_params=pltpu.CompilerParams(
            dimension_semantics=("parallel", "arbitrary"), vmem_limit_bytes=VMEM_LIMIT),
        name="gla_proj",
    )(x, g, wt, wt_side)


def _shift_groups(a, n):
    return jnp.concatenate([a[n:], a[:n]], axis=0)


def _block_first_group(a, n):
    return jnp.concatenate([a[g - g % n:g - g % n + 1] for g in range(a.shape[0])], axis=0)


def _gla_kernel(*refs, n_cast):
    ins, (o_ref,), (st_ref,), casts = _split_cast_refs(refs, 6, 1, n_cast)
    qkv_ref, r_ref, gl_ref, w2_ref, b2_ref, hg_ref = ins
    c_len = GLA_CHUNK
    ng = c_len // SUBLANE
    nb = qkv_ref.shape[0]
    nck = qkv_ref.shape[1] // c_len

    @pl.when(pl.program_id(0) == 0)
    def _():
        st_ref[...] = jnp.zeros_like(st_ref)

    r3 = lax.broadcasted_iota(jnp.int32, (ng, SUBLANE, GLA_DKP), 1)
    ti = lax.broadcasted_iota(jnp.int32, (c_len, c_len), 0)
    si = lax.broadcasted_iota(jnp.int32, (c_len, c_len), 1)
    txs = jnp.where(ti > si, ti ^ si, 0)
    to3 = lambda a: a.reshape(ng, SUBLANE, GLA_DKP)
    to2 = lambda a: a.reshape(c_len, GLA_DKP)

    heads = [divmod(s, GLA_HEADS) for s in range(nb * GLA_HEADS)]
    streams = [(bi, hd, ck) for ck in range(nck) for bi, hd in heads]
    crow = [pl.ds(ck * c_len, c_len) for ck in range(nck)]
    each = lambda f, *lists: [f(*args) for args in zip(*lists)]

    z = [_dot(gl_ref[bi, crow[ck], :], w2_ref[hd]) + b2_ref[hd] for bi, hd, ck in streams]
    _run_casts(casts)
    qkv = {(bi, ck): qkv_ref[bi, crow[ck], :2 * GLA_KW + GLA_DKP].astype(F32)
           for bi in range(nb) for ck in range(nck)}
    lane = lax.broadcasted_iota(jnp.int32, (c_len, GLA_DKP), 1)
    head = lambda a, lo: to3(jnp.where(lane < GLA_DK, a[:, lo:lo + GLA_DKP], 0.0))
    q = [head(qkv[bi, ck], hd * GLA_DK) for bi, hd, ck in streams]
    k = [head(qkv[bi, ck], GLA_KW + hd * GLA_DK) for bi, hd, ck in streams]

    def log2_decay(zs):
        soft = jnp.log2(1.0 + jnp.exp2(jnp.abs(zs) * -LOG2E))
        return (jnp.minimum(zs, 0.0) * LOG2E - soft) * (1.0 / GLA_TAU)

    tril = jnp.where(ti >= si, 1.0, 0.0).astype(BF16)

    def prefix(a):
        hi = a.astype(BF16)
        lo = (a - hi.astype(F32)).astype(BF16)
        return to3(_dot(tril, hi) + _dot(tril, lo))

    cum = each(prefix, each(log2_decay, z))

    nt = lambda a, b: _dot_nt(to2(a).astype(BF16), to2(b).astype(BF16))
    attn = each(lambda a, b: jnp.where(ti == si, nt(a, b), 0.0), q, k)
    g_h = cum
    g_group = each(lambda a: jnp.broadcast_to(a[:, 0:1, :], a.shape), cum)
    h = 1
    while h < c_len:
        if h < SUBLANE:
            if h > 1:
                g_h = each(lambda a: jnp.where((r3 & (h // 2)) != 0,
                                               pltpu.roll(a, h // 2, axis=1), a), g_h)
            g_next = each(lambda a: pltpu.roll(a, SUBLANE - h, axis=1), g_h)
        else:
            n = h // SUBLANE
            g_h = g_group
            if n > 1:
                g_h = each(lambda a: _block_first_group(a, n), g_h)
            g_next = each(lambda a: _shift_groups(a, n), g_h)
        q_h = q if h == 1 else each(lambda a, c, g: a * jnp.exp2(c - g), q, cum, g_h)
        k_h = each(lambda a, c, g: a * jnp.exp2(g - c), k, cum, g_next)
        a_h = each(nt, q_h, k_h)
        attn = each(lambda new, old: jnp.where((txs >= h) & (txs < 2 * h), new, old), a_h, attn)
        h *= 2

    last = each(lambda a: a[ng - 1, SUBLANE - 1:SUBLANE, :], cum)
    q_in = each(lambda a, c: to2(a * jnp.exp2(c)).astype(BF16), q, cum)
    k_out = each(lambda a, c, l: to2(a * jnp.exp2(l - c)).astype(BF16), k, cum, last)
    v = [qkv_ref[bi, crow[ck], pl.ds(2 * GLA_KW + hd * GLA_DV, GLA_DV)] for bi, hd, ck in streams]

    st = [st_ref[s] for s in range(len(heads))]
    o = []
    for ck in range(nck):
        of = slice(ck * len(heads), (ck + 1) * len(heads))
        o += each(lambda qi, s_, a, v_: _dot_nt(qi, s_.astype(BF16)) + _dot(a.astype(BF16), v_),
                  q_in[of], st, attn[of], v[of])
        st = each(lambda s_, l, v_, ko: s_ * jnp.exp2(l) + _dot_tn(v_, ko),
                  st, last[of], v[of], k_out[of])
    for s, s_ in enumerate(st):
        st_ref[s] = s_

    o = each(lambda a: _rmsnorm(a, hg_ref[...], EPS * GLA_DK), o)
    for (bi, hd, ck), a in zip(streams, o):
        c = pl.ds(hd * GLA_DV, GLA_DV)
        r = r_ref[bi, crow[ck], c].astype(F32)
        o_ref[bi, crow[ck], c] = (a * (r * jax.nn.sigmoid(r))).astype(BF16)


def _gla(p_main, p_side, w2, b2, head_g, cast_weights, *, batch, seq):
    t = p_main.shape[0]
    p_main = p_main.reshape(batch, seq, GLA_MAIN_N)
    p_side = p_side.reshape(batch, seq, GLA_SIDE_N)
    rows = GLA_CHUNK * GLA_STEP_CHUNKS
    steps = seq // rows
    assert 2 * GLA_KW == TOK_W
    c_split = 4
    r_split = steps // c_split
    by_step = lambda i: (i // c_split, i % c_split)
    jobs = [(w, lead, (w.shape[-2] // r_split, w.shape[-1] // c_split), by_step)
            for w, lead in cast_weights]
    c_in, c_out, c_shape, c_args = _cast_specs(jobs)
    res = pl.pallas_call(
        functools.partial(_gla_kernel, n_cast=len(jobs)),
        grid=(steps,),
        in_specs=[
            pl.BlockSpec((batch, rows, 2 * TOK_W), lambda i: (0, i, 0)),
            pl.BlockSpec((batch, rows, TOK_W), lambda i: (0, i, 2)),
            pl.BlockSpec((batch, rows, GLA_RANKP), lambda i: (0, i, X_W // GLA_RANKP)),
            pl.BlockSpec((GLA_HEADS, GLA_RANKP, GLA_DKP), lambda i: (0, 0, 0)),
            pl.BlockSpec((GLA_HEADS, 1, GLA_DKP), lambda i: (0, 0, 0)),
            pl.BlockSpec((1, GLA_DV), lambda i: (0, 0)),
        ] + c_in,
        out_specs=[pl.BlockSpec((batch, rows, TOK_W), lambda i: (0, i, 0))] + c_out,
        out_shape=[jax.ShapeDtypeStruct((batch, seq, TOK_W), BF16)] + c_shape,
        scratch_shapes=[pltpu.VMEM((batch * GLA_HEADS, GLA_DV, GLA_DKP), F32)],
        compiler_params=pltpu.CompilerParams(
            dimension_semantics=("arbitrary",), vmem_limit_bytes=VMEM_LIMIT),
        name="gla",
    )(p_main, p_main, p_side, w2, b2, head_g, *c_args)
    return res[0].reshape(t, TOK_W), tuple(res[1:])


def _conv_mixer_kernel(x_ref, g_ref, wb_ref, wc_ref, wx_ref, cw_ref, cb_ref, tok_ref, xq_ref,
                       h_ref, carry_ref, *, tiles_per_seq):
    i = pl.program_id(0)
    j = pl.program_id(1)
    c = jnp.maximum(j - 1, 0)
    tm = x_ref.shape[0]

    @pl.when(j == 0)
    def _():
        h_ref[...] = _rmsnorm(x_ref[...], g_ref[...]).astype(BF16)
        xq_ref[...] = _dot(h_ref[...], wb_ref[...]).astype(BF16)

    @pl.when((j > 0) & (i % tiles_per_seq == 0))
    def _():
        carry_ref[c] = jnp.zeros(carry_ref.shape[1:], F32)

    @pl.when(j > 0)
    def _():
        h = h_ref[...]
        u = _dot(h, wc_ref[...]) * _dot(h, wx_ref[...])
        u_prev = carry_ref[c]
        carry_ref[c] = u[tm - SUBLANE:, :]
        row = lax.broadcasted_iota(jnp.int32, u.shape, 0)
        u1 = jnp.where(row == 0, u_prev[SUBLANE - 1:SUBLANE, :], pltpu.roll(u, 1, axis=0))
        u2 = pltpu.roll(u, 2, axis=0)
        u2 = jnp.where(row == 0, u_prev[SUBLANE - 2:SUBLANE - 1, :], u2)
        u2 = jnp.where(row == 1, u_prev[SUBLANE - 1:SUBLANE, :], u2)
        w = cw_ref[...]
        y = u2 * w[0:1, :] + u1 * w[1:2, :] + u * w[2:3, :] + cb_ref[...]
        tok_ref[...] = (_dot(h, wb_ref[...]) * y).astype(BF16)


def _conv_mixer(x, g, g_lead, w, conv_w, conv_b, lead, *, seq, tm=1024, tw=512):
    t, d = x.shape
    assert tw == X_W
    n_conv = CONV_W // tw
    col = lambda j: jnp.maximum(j - 1, 0)
    return pl.pallas_call(
        functools.partial(_conv_mixer_kernel, tiles_per_seq=seq // tm),
        grid=(t // tm, n_conv + 1),
        in_specs=[
            pl.BlockSpec((tm, d), lambda i, j: (i, 0)),
            _stacked_spec(g_lead, (1, d), lambda i, j: (0, 0)),
            pl.BlockSpec((d, tw), lambda i, j: (0, jnp.where(j == 0, 3 * n_conv, j - 1))),
            pl.BlockSpec((d, tw), lambda i, j: (0, n_conv + col(j))),
            pl.BlockSpec((d, tw), lambda i, j: (0, 2 * n_conv + col(j))),
            _stacked_spec(lead, (CONV_K, tw), lambda i, j: (0, col(j))),
            _stacked_spec(lead, (1, tw), lambda i, j: (0, col(j))),
        ],
        out_specs=[
            pl.BlockSpec((tm, tw), lambda i, j: (i, col(j))),
            pl.BlockSpec((tm, tw), lambda i, j: (i, 0)),
        ],
        out_shape=[jax.ShapeDtypeStruct((t, CONV_W), BF16), jax.ShapeDtypeStruct((t, X_W), BF16)],
        scratch_shapes=[pltpu.VMEM((tm, d), BF16), pltpu.VMEM((n_conv, SUBLANE, tw), F32)],
        compiler_params=pltpu.CompilerParams(
            dimension_semantics=("arbitrary", "arbitrary"), vmem_limit_bytes=VMEM_LIMIT),
        name="conv_mixer",
    )(x, g, w, w, w, conv_w, conv_b)


def _mixout_kernel(tok_ref, xq_ref, kv_ref, wm_ref, x_ref, g_ref, o_ref):
    mixed = _dot(tok_ref[...], wm_ref[:TOK_W, :])
    xo = []
    for h in range(X_HEADS):
        lo = h * X_HEAD_DIM
        qh = xq_ref[:, lo:lo + X_HEAD_DIM]
        kh = kv_ref[:, lo:lo + X_HEAD_DIM]
        vh = kv_ref[:, X_W + lo:X_W + lo + X_HEAD_DIM]
        s = _dot_nt(qh, kh) * (X_HEAD_DIM ** -0.5)
        e = jnp.exp(s - jnp.max(s, axis=-1, keepdims=True))
        pr = (e / jnp.sum(e, axis=-1, keepdims=True)).astype(BF16)
        xo.append(_dot(pr, vh).astype(BF16))
    mixed = mixed + _dot(jnp.concatenate(xo, axis=1), wm_ref[TOK_W:, :])
    o_ref[...] = x_ref[...] + _rmsnorm(mixed, g_ref[...])


def _mixout(tok, p, xq_block, kv, wm, x, g, lead, *, seq, tm=512):
    t, d = x.shape
    per_seq = seq // tm
    return pl.pallas_call(
        _mixout_kernel,
        grid=(t // tm,),
        in_specs=[
            pl.BlockSpec((tm, TOK_W), lambda i: (i, 0)),
            pl.BlockSpec((tm, X_W), lambda i: (i, xq_block)),
            pl.BlockSpec((MEM_LEN, 2 * X_W), lambda i: (i // per_seq, 0)),
            pl.BlockSpec((d, d), lambda i: (0, 0)),
            pl.BlockSpec((tm, d), lambda i: (i, 0)),
            _stacked_spec(lead, (1, d), lambda i: (0, 0)),
        ],
        out_specs=pl.BlockSpec((tm, d), lambda i: (i, 0)),
        out_shape=jax.ShapeDtypeStruct((t, d), F32),
        compiler_params=pltpu.CompilerParams(
            dimension_semantics=("parallel",), vmem_limit_bytes=VMEM_LIMIT),
        name="mixout",
    )(tok, p, kv, wm, x, g)


def _gla_side_weight(wt):
    lo = GLA_MAIN_N
    pad = jnp.zeros((GLA_RANKP - GLA_RANK, wt.shape[1]), wt.dtype)
    return jnp.concatenate([wt[lo + GLA_RANK:], wt[lo:lo + GLA_RANK], pad], axis=0)


def _gla_gate_weights(gate_w2, gate_b):
    w2 = gate_w2.reshape(GLA_RANK, GLA_HEADS, GLA_DK).transpose(1, 0, 2)
    w2 = jnp.pad(w2, ((0, 0), (0, GLA_RANKP - GLA_RANK), (0, GLA_DKP - GLA_DK))).astype(BF16)
    b2 = jnp.pad(gate_b.reshape(GLA_HEADS, 1, GLA_DK), ((0, 0), (0, 0), (0, GLA_DKP - GLA_DK)))
    return w2, b2


def kernel(x, mem, ffn_pre_g, ffn_w_in, ffn_w_out, ffn_post_g, mix_pre_g, mix_post_g, mem_g,
           w_mem_kv, w_mix_out, gla_w_in, gla_gate_w2, gla_gate_b, gla_head_g,
           conv_w_in, conv_w, conv_b):
    batch, seq, d = x.shape
    x = x.reshape(batch * seq, d)
    mem2 = mem.reshape(batch * MEM_LEN, d)
    ffn_pre_g = ffn_pre_g[:, :, None, :]
    ffn_post_g = ffn_post_g[:, :, None, :]
    mix_pre_g = mix_pre_g[:, None, :]
    mix_post_g = mix_post_g[:, None, :]
    mem_g = mem_g[:, None, :]
    conv_b = conv_b[:, None, :]

    ffn_gains = (ffn_pre_g, ffn_post_g)
    ffn_stacks = (ffn_w_in, ffn_w_out)
    assert DEPTH == 2
    nt = batch * seq // 1024
    nf = D_FF // FFN_TF_BF16

    def mem_kv(i):
        return _norm_matmul(mem2, mem_g, (i,), w_mem_kv, (i,),
                            tm=batch * MEM_LEN, tn=2 * X_W, name="mem_kv")

    def ffn_pair_jobs(lead):
        return [_grid_cast_job(ffn_w_in, lead, nt, nf),
                _grid_cast_job(ffn_w_out, lead, nt, nf, rows_by_j=True)]


    x, _ = _ffn(x, ffn_gains, (0, 0), ffn_stacks, (0, 0), tf=FFN_TF_F32)
    w2, b2 = _gla_gate_weights(gla_gate_w2[0], gla_gate_b[0])
    wt = gla_w_in[0].T
    p, p_side = _gla_proj(x, mix_pre_g, (0,), wt, _gla_side_weight(wt))
    tok, (w01_in, w01_out, wm0, wconv) = _gla(
        p, p_side, w2, b2, gla_head_g[0][None],
        [(ffn_w_in, (0, 1)), (ffn_w_out, (0, 1)), (w_mix_out, (0,)), (conv_w_in, (0,))],
        batch=batch, seq=seq)
    x = _mixout(tok, p_side, 0, mem_kv(0), wm0, x, mix_post_g, (0,), seq=seq)
    x, (w10_in, w10_out, wm1) = _ffn(
        x, ffn_gains, (0, 1), (w01_in, w01_out), (), tf=FFN_TF_BF16,
        cast_jobs=ffn_pair_jobs((1, 0)) + [_grid_cast_job(w_mix_out, (1,), nt, nt)])

    x, (w11_in, w11_out) = _ffn(x, ffn_gains, (1, 0), (w10_in, w10_out), (), tf=FFN_TF_BF16,
                                cast_jobs=ffn_pair_jobs((1, 1)))
    tok, xq = _conv_mixer(x, mix_pre_g, (1,), wconv, conv_w, conv_b, (0,), seq=seq)
    x = _mixout(tok, xq, 0, mem_kv(1), wm1, x, mix_post_g, (1,), seq=seq)
    x, _ = _ffn(x, ffn_gains, (1, 1), (w11_in, w11_out), (), tf=FFN_TF_BF16)
    return x.reshape(batch, seq, d)
```

```python
import functools

import jax
import jax.numpy as jnp
from jax import lax
from jax.experimental import pallas as pl
from jax.experimental.pallas import tpu as pltpu

F32 = jnp.float32
BF16 = jnp.bfloat16

D_MODEL = 2048
DEPTH = 2
MEM_LEN = 256
TOK_W = 1536
X_HEADS = 4
X_HEAD_DIM = 128
X_W = 512
GLA_HEADS = 4
GLA_DV = 384
GLA_DK = 192
GLA_KW = 768
GLA_RANK = 16
GLA_TAU = 16.0
CONV_W = 1536
CONV_K = 3
D_FF = 5632
EPS = 1e-6
LOG2E = 1.4426950408889634

LANE = 128
SUBLANE = 8
MXU_DIM = 256
VMEM_LIMIT = 56 * 1024 * 1024
VMEM_LIMIT_FFN = 60 * 1024 * 1024
NORM_ROWS = 256
FFN_TF_F32 = 256
FFN_TF_BF16 = 512

GLA_RANKP = LANE
GLA_DKP = MXU_DIM
GLA_CHUNK = 64
GLA_STEP_CHUNKS = 2
GLA_MAIN_N = 2 * GLA_KW + 2 * TOK_W
GLA_SIDE_N = X_W + GLA_RANKP
CONV_NP = 3 * CONV_W + X_W


def _rmsnorm(x, g, eps=EPS):
    ms = jnp.mean(x * x, axis=-1, keepdims=True)
    return x * lax.rsqrt(ms + eps) * g


def _dot(a, b):
    return jnp.dot(a, b, preferred_element_type=F32)


def _dot_nt(a, b):
    return lax.dot_general(a, b, (((1,), (1,)), ((), ())), preferred_element_type=F32)


def _dot_tn(a, b):
    return lax.dot_general(a, b, (((0,), (0,)), ((), ())), preferred_element_type=F32)


def _stacked_spec(lead, block, index_map):
    nlead = len(lead)
    return pl.BlockSpec((None,) * nlead + block, lambda *g: tuple(lead) + tuple(index_map(*g)))


def _cast_specs(jobs):
    in_specs = [_stacked_spec(lead, blk, imap) for _, lead, blk, imap in jobs]
    out_specs = [pl.BlockSpec(blk, imap) for _, _, blk, imap in jobs]
    out_shape = [jax.ShapeDtypeStruct(w.shape[len(lead):], BF16) for w, lead, _, _ in jobs]
    return in_specs, out_specs, out_shape, [w for w, _, _, _ in jobs]


def _split_cast_refs(refs, n_in, n_out, n_cast):
    ins, refs = refs[:n_in], refs[n_in:]
    cast_in, refs = refs[:n_cast], refs[n_cast:]
    outs, refs = refs[:n_out], refs[n_out:]
    cast_out, scratch = refs[:n_cast], refs[n_cast:]
    return ins, outs, scratch, list(zip(cast_in, cast_out))


def _run_casts(casts):
    for src, dst in casts:
        dst[...] = src[...].astype(BF16)


def _ffn_kernel(*refs, n_cast):
    ins, (o_ref,), (h_ref,), casts = _split_cast_refs(refs, 6, 1, n_cast)
    x_ref, pre_g_ref, wg_ref, wu_ref, wo_ref, post_g_ref = ins
    j = pl.program_id(1)
    last = pl.num_programs(1) - 1
    tm = x_ref.shape[0]

    def weights():
        return tuple(w[...].astype(BF16) for w in (wg_ref, wu_ref, wo_ref))

    def step(rows, first, final, w):
        wg, wu, wo = w
        if first:
            h = _rmsnorm(x_ref[rows, :], pre_g_ref[...]).astype(BF16)
            h_ref[rows, :] = h
        else:
            h = h_ref[rows, :]
        gate = _dot(h, wg)
        act = (gate * jax.nn.sigmoid(gate) * _dot(h, wu)).astype(BF16)
        y = _dot(act, wo)
        acc = y if first else o_ref[rows, :] + y
        if final:
            acc = x_ref[rows, :] + _rmsnorm(acc, 0.5 * post_g_ref[...])
        o_ref[rows, :] = acc

    pieces = [pl.ds(r, NORM_ROWS) for r in range(0, tm, NORM_ROWS)]

    @pl.when(j == 0)
    def _():
        _run_casts(casts)
        w = weights()
        for rows in pieces:
            step(rows, True, False, w)

    @pl.when((j > 0) & (j < last))
    def _():
        _run_casts(casts)
        step(slice(None), False, False, weights())

    @pl.when(j == last)
    def _():
        _run_casts(casts)
        w = weights()
        for rows in pieces:
            step(rows, False, True, w)


def _grid_cast_job(w, lead, n_i, n_j, rows_by_j=False):
    if rows_by_j:
        blk = (w.shape[-2] // n_j, w.shape[-1] // n_i)
        return (w, lead, blk, lambda i, j: (jnp.minimum(j, n_j - 1), i))
    blk = (w.shape[-2] // n_i, w.shape[-1] // n_j)
    return (w, lead, blk, lambda i, j: (i, jnp.minimum(j, n_j - 1)))


def _ffn(x, gains, g_lead, weights, w_lead, *, tf, tm=1024, cast_jobs=()):
    t, d = x.shape
    nt, nf = t // tm, D_FF // tf
    pre_g, post_g = gains
    w_in, w_out = weights
    c_in, c_out, c_shape, c_args = _cast_specs(cast_jobs)
    res = pl.pallas_call(
        functools.partial(_ffn_kernel, n_cast=len(cast_jobs)),
        grid=(nt, nf),
        in_specs=[
            pl.BlockSpec((tm, d), lambda i, j: (i, 0)),
            _stacked_spec(g_lead, (1, d), lambda i, j: (0, 0)),
            _stacked_spec(w_lead, (d, tf), lambda i, j: (0, j)),
            _stacked_spec(w_lead, (d, tf), lambda i, j: (0, j + nf)),
            _stacked_spec(w_lead, (tf, d), lambda i, j: (j, 0)),
            _stacked_spec(g_lead, (1, d), lambda i, j: (0, 0)),
        ] + c_in,
        out_specs=[pl.BlockSpec((tm, d), lambda i, j: (i, 0))] + c_out,
        out_shape=[jax.ShapeDtypeStruct((t, d), F32)] + c_shape,
        scratch_shapes=[pltpu.VMEM((tm, d), BF16)],
        compiler_params=pltpu.CompilerParams(
            dimension_semantics=("parallel", "arbitrary"), vmem_limit_bytes=VMEM_LIMIT_FFN),
        name="ffn",
    )(x, pre_g, w_in, w_in, w_out, post_g, *c_args)
    return res[0], tuple(res[1:])


def _norm_matmul_kernel(x_ref, g_ref, w_ref, o_ref, h_ref):
    @pl.when(pl.program_id(1) == 0)
    def _():
        h_ref[...] = _rmsnorm(x_ref[...], g_ref[...]).astype(BF16)

    o_ref[...] = _dot(h_ref[...], w_ref[...].astype(BF16)).astype(o_ref.dtype)


def _norm_matmul(x, g, g_lead, w, w_lead, *, tm, tn, name):
    t, d = x.shape
    n = w.shape[-1]
    return pl.pallas_call(
        _norm_matmul_kernel,
        grid=(t // tm, n // tn),
        in_specs=[
            pl.BlockSpec((tm, d), lambda i, j: (i, 0)),
            _stacked_spec(g_lead, (1, d), lambda i, j: (0, 0)),
            _stacked_spec(w_lead, (d, tn), lambda i, j: (0, j)),
        ],
        out_specs=pl.BlockSpec((tm, tn), lambda i, j: (i, j)),
        out_shape=jax.ShapeDtypeStruct((t, n), BF16),
        scratch_shapes=[pltpu.VMEM((tm, d), BF16)],
        compiler_params=pltpu.CompilerParams(
            dimension_semantics=("parallel", "arbitrary"), vmem_limit_bytes=VMEM_LIMIT),
        name=name,
    )(x, g, w)


def _gla_proj_kernel(x_ref, g_ref, wm_ref, ws_ref, om_ref, os_ref, h_ref):
    j = pl.program_id(1)

    @pl.when(j == 0)
    def _():
        h_ref[...] = _rmsnorm(x_ref[...], g_ref[...]).astype(BF16)
        os_ref[...] = _dot_nt(h_ref[...], ws_ref[...].astype(BF16)).astype(BF16)

    @pl.when(j > 0)
    def _():
        om_ref[...] = _dot_nt(h_ref[...], wm_ref[...].astype(BF16)).astype(BF16)


def _gla_proj(x, g, g_lead, wt, wt_side, *, tm=1024, tn=768):
    t, d = x.shape
    n_main = GLA_MAIN_N // tn
    blk = lambda j: jnp.maximum(j - 1, 0)
    w_blk = lambda j: jnp.where(j == 0, n_main - 1, j - 1)
    return pl.pallas_call(
        _gla_proj_kernel,
        grid=(t // tm, n_main + 1),
        in_specs=[
            pl.BlockSpec((tm, d), lambda i, j: (i, 0)),
            _stacked_spec(g_lead, (1, d), lambda i, j: (0, 0)),
            pl.BlockSpec((tn, d), lambda i, j: (w_blk(j), 0)),
            pl.BlockSpec((GLA_SIDE_N, d), lambda i, j: (0, 0)),
        ],
        out_specs=[
            pl.BlockSpec((tm, tn), lambda i, j: (i, blk(j))),
            pl.BlockSpec((tm, GLA_SIDE_N), lambda i, j: (i, 0)),
        ],
        out_shape=[jax.ShapeDtypeStruct((t, GLA_MAIN_N), BF16),
                   jax.ShapeDtypeStruct((t, GLA_SIDE_N), BF16)],
        scratch_shapes=[pltpu.VMEM((tm, d), BF16)],
        compiler_params=pltpu.CompilerParams(
            dimension_semantics=("parallel", "arbitrary"), vmem_limit_bytes=VMEM_LIMIT),
        name="gla_proj",
    )(x, g, wt, wt_side)


def _shift_groups(a, n):
    return jnp.concatenate([a[n:], a[:n]], axis=0)


def _block_first_group(a, n):
    return jnp.concatenate([a[g - g % n:g - g % n + 1] for g in range(a.shape[0])], axis=0)


def _gla_kernel(*refs, n_cast):
    ins, (o_ref,), (st_ref,), casts = _split_cast_refs(refs, 6, 1, n_cast)
    qkv_ref, r_ref, gl_ref, w2_ref, b2_ref, hg_ref = ins
    c_len = GLA_CHUNK
    ng = c_len // SUBLANE
    nb = qkv_ref.shape[0]
    nck = qkv_ref.shape[1] // c_len

    @pl.when(pl.program_id(0) == 0)
    def _():
        st_ref[...] = jnp.zeros_like(st_ref)

    r3 = lax.broadcasted_iota(jnp.int32, (ng, SUBLANE, GLA_DKP), 1)
    ti = lax.broadcasted_iota(jnp.int32, (c_len, c_len), 0)
    si = lax.broadcasted_iota(jnp.int32, (c_len, c_len), 1)
    txs = jnp.where(ti > si, ti ^ si, 0)
    to3 = lambda a: a.reshape(ng, SUBLANE, GLA_DKP)
    to2 = lambda a: a.reshape(c_len, GLA_DKP)

    heads = [divmod(s, GLA_HEADS) for s in range(nb * GLA_HEADS)]
    streams = [(bi, hd, ck) for ck in range(nck) for bi, hd in heads]
    crow = [pl.ds(ck * c_len, c_len) for ck in range(nck)]
    each = lambda f, *lists: [f(*args) for args in zip(*lists)]

    z = [_dot(gl_ref[bi, crow[ck], :], w2_ref[hd]) + b2_ref[hd] for bi, hd, ck in streams]
    _run_casts(casts)
    qkv = {(bi, ck): qkv_ref[bi, crow[ck], :2 * GLA_KW + GLA_DKP].astype(F32)
           for bi in range(nb) for ck in range(nck)}
    lane = lax.broadcasted_iota(jnp.int32, (c_len, GLA_DKP), 1)
    head = lambda a, lo: to3(jnp.where(lane < GLA_DK, a[:, lo:lo + GLA_DKP], 0.0))
    q = [head(qkv[bi, ck], hd * GLA_DK) for bi, hd, ck in streams]
    k = [head(qkv[bi, ck], GLA_KW + hd * GLA_DK) for bi, hd, ck in streams]

    def log2_decay(zs):
        soft = jnp.log2(1.0 + jnp.exp2(jnp.abs(zs) * -LOG2E))
        return (jnp.minimum(zs, 0.0) * LOG2E - soft) * (1.0 / GLA_TAU)

    tril = jnp.where(ti >= si, 1.0, 0.0).astype(BF16)

    def prefix(a):
        hi = a.astype(BF16)
        lo = (a - hi.astype(F32)).astype(BF16)
        return to3(_dot(tril, hi) + _dot(tril, lo))

    cum = each(prefix, each(log2_decay, z))

    nt = lambda a, b: _dot_nt(to2(a).astype(BF16), to2(b).astype(BF16))
    attn = each(lambda a, b: jnp.where(ti == si, nt(a, b), 0.0), q, k)
    g_h = cum
    g_group = each(lambda a: jnp.broadcast_to(a[:, 0:1, :], a.shape), cum)
    h = 1
    while h < c_len:
        if h < SUBLANE:
            if h > 1:
                g_h = each(lambda a: jnp.where((r3 & (h // 2)) != 0,
                                               pltpu.roll(a, h // 2, axis=1), a), g_h)
            g_next = each(lambda a: pltpu.roll(a, SUBLANE - h, axis=1), g_h)
        else:
            n = h // SUBLANE
            g_h = g_group
            if n > 1:
                g_h = each(lambda a: _block_first_group(a, n), g_h)
            g_next = each(lambda a: _shift_groups(a, n), g_h)
        q_h = q if h == 1 else each(lambda a, c, g: a * jnp.exp2(c - g), q, cum, g_h)
        k_h = each(lambda a, c, g: a * jnp.exp2(g - c), k, cum, g_next)
        a_h = each(nt, q_h, k_h)
        attn = each(lambda new, old: jnp.where((txs >= h) & (txs < 2 * h), new, old), a_h, attn)
        h *= 2

    last = each(lambda a: a[ng - 1, SUBLANE - 1:SUBLANE, :], cum)
    q_in = each(lambda a, c: to2(a * jnp.exp2(c)).astype(BF16), q, cum)
    k_out = each(lambda a, c, l: to2(a * jnp.exp2(l - c)).astype(BF16), k, cum, last)
    v = [qkv_ref[bi, crow[ck], pl.ds(2 * GLA_KW + hd * GLA_DV, GLA_DV)] for bi, hd, ck in streams]

    st = [st_ref[s] for s in range(len(heads))]
    o = []
    for ck in range(nck):
        of = slice(ck * len(heads), (ck + 1) * len(heads))
        o += each(lambda qi, s_, a, v_: _dot_nt(qi, s_.astype(BF16)) + _dot(a.astype(BF16), v_),
                  q_in[of], st, attn[of], v[of])
        st = each(lambda s_, l, v_, ko: s_ * jnp.exp2(l) + _dot_tn(v_, ko),
                  st, last[of], v[of], k_out[of])
    for s, s_ in enumerate(st):
        st_ref[s] = s_

    o = each(lambda a: _rmsnorm(a, hg_ref[...], EPS * GLA_DK), o)
    for (bi, hd, ck), a in zip(streams, o):
        c = pl.ds(hd * GLA_DV, GLA_DV)
        r = r_ref[bi, crow[ck], c].astype(F32)
        o_ref[bi, crow[ck], c] = (a * (r * jax.nn.sigmoid(r))).astype(BF16)


def _gla(p_main, p_side, w2, b2, head_g, cast_weights, *, batch, seq):
    t = p_main.shape[0]
    p_main = p_main.reshape(batch, seq, GLA_MAIN_N)
    p_side = p_side.reshape(batch, seq, GLA_SIDE_N)
    rows = GLA_CHUNK * GLA_STEP_CHUNKS
    steps = seq // rows
    assert 2 * GLA_KW == TOK_W
    c_split = 4
    r_split = steps // c_split
    by_step = lambda i: (i // c_split, i % c_split)
    jobs = [(w, lead, (w.shape[-2] // r_split, w.shape[-1] // c_split), by_step)
            for w, lead in cast_weights]
    c_in, c_out, c_shape, c_args = _cast_specs(jobs)
    res = pl.pallas_call(
        functools.partial(_gla_kernel, n_cast=len(jobs)),
        grid=(steps,),
        in_specs=[
            pl.BlockSpec((batch, rows, 2 * TOK_W), lambda i: (0, i, 0)),
            pl.BlockSpec((batch, rows, TOK_W), lambda i: (0, i, 2)),
            pl.BlockSpec((batch, rows, GLA_RANKP), lambda i: (0, i, X_W // GLA_RANKP)),
            pl.BlockSpec((GLA_HEADS, GLA_RANKP, GLA_DKP), lambda i: (0, 0, 0)),
            pl.BlockSpec((GLA_HEADS, 1, GLA_DKP), lambda i: (0, 0, 0)),
            pl.BlockSpec((1, GLA_DV), lambda i: (0, 0)),
        ] + c_in,
        out_specs=[pl.BlockSpec((batch, rows, TOK_W), lambda i: (0, i, 0))] + c_out,
        out_shape=[jax.ShapeDtypeStruct((batch, seq, TOK_W), BF16)] + c_shape,
        scratch_shapes=[pltpu.VMEM((batch * GLA_HEADS, GLA_DV, GLA_DKP), F32)],
        compiler_params=pltpu.CompilerParams(
            dimension_semantics=("arbitrary",), vmem_limit_bytes=VMEM_LIMIT),
        name="gla",
    )(p_main, p_main, p_side, w2, b2, head_g, *c_args)
    return res[0].reshape(t, TOK_W), tuple(res[1:])


def _conv_mixer_kernel(x_ref, g_ref, wb_ref, wc_ref, wx_ref, cw_ref, cb_ref, tok_ref, xq_ref,
                       h_ref, carry_ref, *, tiles_per_seq):
    i = pl.program_id(0)
    j = pl.program_id(1)
    c = jnp.maximum(j - 1, 0)
    tm = x_ref.shape[0]

    @pl.when(j == 0)
    def _():
        h_ref[...] = _rmsnorm(x_ref[...], g_ref[...]).astype(BF16)
        xq_ref[...] = _dot(h_ref[...], wb_ref[...]).astype(BF16)

    @pl.when((j > 0) & (i % tiles_per_seq == 0))
    def _():
        carry_ref[c] = jnp.zeros(carry_ref.shape[1:], F32)

    @pl.when(j > 0)
    def _():
        h = h_ref[...]
        u = _dot(h, wc_ref[...]) * _dot(h, wx_ref[...])
        u_prev = carry_ref[c]
        carry_ref[c] = u[tm - SUBLANE:, :]
        row = lax.broadcasted_iota(jnp.int32, u.shape, 0)
        u1 = jnp.where(row == 0, u_prev[SUBLANE - 1:SUBLANE, :], pltpu.roll(u, 1, axis=0))
        u2 = pltpu.roll(u, 2, axis=0)
        u2 = jnp.where(row == 0, u_prev[SUBLANE - 2:SUBLANE - 1, :], u2)
        u2 = jnp.where(row == 1, u_prev[SUBLANE - 1:SUBLANE, :], u2)
        w = cw_ref[...]
        y = u2 * w[0:1, :] + u1 * w[1:2, :] + u * w[2:3, :] + cb_ref[...]
        tok_ref[...] = (_dot(h, wb_ref[...]) * y).astype(BF16)


def _conv_mixer(x, g, g_lead, w, conv_w, conv_b, lead, *, seq, tm=1024, tw=512):
    t, d = x.shape
    assert tw == X_W
    n_conv = CONV_W // tw
    col = lambda j: jnp.maximum(j - 1, 0)
    return pl.pallas_call(
        functools.partial(_conv_mixer_kernel, tiles_per_seq=seq // tm),
        grid=(t // tm, n_conv + 1),
        in_specs=[
            pl.BlockSpec((tm, d), lambda i, j: (i, 0)),
            _stacked_spec(g_lead, (1, d), lambda i, j: (0, 0)),
            pl.BlockSpec((d, tw), lambda i, j: (0, jnp.where(j == 0, 3 * n_conv, j - 1))),
            pl.BlockSpec((d, tw), lambda i, j: (0, n_conv + col(j))),
            pl.BlockSpec((d, tw), lambda i, j: (0, 2 * n_conv + col(j))),
            _stacked_spec(lead, (CONV_K, tw), lambda i, j: (0, col(j))),
            _stacked_spec(lead, (1, tw), lambda i, j: (0, col(j))),
        ],
        out_specs=[
            pl.BlockSpec((tm, tw), lambda i, j: (i, col(j))),
            pl.BlockSpec((tm, tw), lambda i, j: (i, 0)),
        ],
        out_shape=[jax.ShapeDtypeStruct((t, CONV_W), BF16), jax.ShapeDtypeStruct((t, X_W), BF16)],
        scratch_shapes=[pltpu.VMEM((tm, d), BF16), pltpu.VMEM((n_conv, SUBLANE, tw), F32)],
        compiler_params=pltpu.CompilerParams(
            dimension_semantics=("arbitrary", "arbitrary"), vmem_limit_bytes=VMEM_LIMIT),
        name="conv_mixer",
    )(x, g, w, w, w, conv_w, conv_b)


def _mixout_kernel(tok_ref, xq_ref, kv_ref, wm_ref, x_ref, g_ref, o_ref):
    d = o_ref.shape[1]
    slab = d // X_HEADS
    tok = tok_ref[...]
    mixed, xo = [], []
    for h in range(X_HEADS):
        mixed.append(_dot(tok, wm_ref[:TOK_W, h * slab:(h + 1) * slab]))
        lo = h * X_HEAD_DIM
        qh = xq_ref[:, lo:lo + X_HEAD_DIM]
        kh = kv_ref[:, lo:lo + X_HEAD_DIM]
        vh = kv_ref[:, X_W + lo:X_W + lo + X_HEAD_DIM]
        s = _dot_nt(qh, kh) * (X_HEAD_DIM ** -0.5)
        e = jnp.exp(s - jnp.max(s, axis=-1, keepdims=True))
        pr = (e / jnp.sum(e, axis=-1, keepdims=True)).astype(BF16)
        xo.append(_dot(pr, vh).astype(BF16))
    mixed = jnp.concatenate(mixed, axis=1) + _dot(jnp.concatenate(xo, axis=1), wm_ref[TOK_W:, :])
    o_ref[...] = x_ref[...] + _rmsnorm(mixed, g_ref[...])


def _mixout(tok, p, xq_block, kv, wm, x, g, lead, *, seq, tm=512):
    t, d = x.shape
    per_seq = seq // tm
    return pl.pallas_call(
        _mixout_kernel,
        grid=(t // tm,),
        in_specs=[
            pl.BlockSpec((tm, TOK_W), lambda i: (i, 0)),
            pl.BlockSpec((tm, X_W), lambda i: (i, xq_block)),
            pl.BlockSpec((MEM_LEN, 2 * X_W), lambda i: (i // per_seq, 0)),
            pl.BlockSpec((d, d), lambda i: (0, 0)),
            pl.BlockSpec((tm, d), lambda i: (i, 0)),
            _stacked_spec(lead, (1, d), lambda i: (0, 0)),
        ],
        out_specs=pl.BlockSpec((tm, d), lambda i: (i, 0)),
        out_shape=jax.ShapeDtypeStruct((t, d), F32),
        compiler_params=pltpu.CompilerParams(
            dimension_semantics=("parallel",), vmem_limit_bytes=VMEM_LIMIT),
        name="mixout",
    )(tok, p, kv, wm, x, g)


def _gla_side_weight(wt):
    lo = GLA_MAIN_N
    pad = jnp.zeros((GLA_RANKP - GLA_RANK, wt.shape[1]), wt.dtype)
    return jnp.concatenate([wt[lo + GLA_RANK:], wt[lo:lo + GLA_RANK], pad], axis=0)


def _gla_gate_weights(gate_w2, gate_b):
    w2 = gate_w2.reshape(GLA_RANK, GLA_HEADS, GLA_DK).transpose(1, 0, 2)
    w2 = jnp.pad(w2, ((0, 0), (0, GLA_RANKP - GLA_RANK), (0, GLA_DKP - GLA_DK))).astype(BF16)
    b2 = jnp.pad(gate_b.reshape(GLA_HEADS, 1, GLA_DK), ((0, 0), (0, 0), (0, GLA_DKP - GLA_DK)))
    return w2, b2


def kernel(x, mem, ffn_pre_g, ffn_w_in, ffn_w_out, ffn_post_g, mix_pre_g, mix_post_g, mem_g,
           w_mem_kv, w_mix_out, gla_w_in, gla_gate_w2, gla_gate_b, gla_head_g,
           conv_w_in, conv_w, conv_b):
    batch, seq, d = x.shape
    x = x.reshape(batch * seq, d)
    mem2 = mem.reshape(batch * MEM_LEN, d)
    ffn_pre_g = ffn_pre_g[:, :, None, :]
    ffn_post_g = ffn_post_g[:, :, None, :]
    mix_pre_g = mix_pre_g[:, None, :]
    mix_post_g = mix_post_g[:, None, :]
    mem_g = mem_g[:, None, :]
    conv_b = conv_b[:, None, :]

    ffn_gains = (ffn_pre_g, ffn_post_g)
    ffn_stacks = (ffn_w_in, ffn_w_out)
    assert DEPTH == 2
    nt = batch * seq // 1024
    nf = D_FF // FFN_TF_BF16

    def mem_kv(i):
        return _norm_matmul(mem2, mem_g, (i,), w_mem_kv, (i,),
                            tm=batch * MEM_LEN, tn=2 * X_W, name="mem_kv")

    def ffn_pair_jobs(lead):
        return [_grid_cast_job(ffn_w_in, lead, nt, nf),
                _grid_cast_job(ffn_w_out, lead, nt, nf, rows_by_j=True)]


    x, _ = _ffn(x, ffn_gains, (0, 0), ffn_stacks, (0, 0), tf=FFN_TF_F32)
    w2, b2 = _gla_gate_weights(gla_gate_w2[0], gla_gate_b[0])
    wt = gla_w_in[0].T
    p, p_side = _gla_proj(x, mix_pre_g, (0,), wt, _gla_side_weight(wt))
    tok, (w01_in, w01_out, wm0, wconv) = _gla(
        p, p_side, w2, b2, gla_head_g[0][None],
        [(ffn_w_in, (0, 1)), (ffn_w_out, (0, 1)), (w_mix_out, (0,)), (conv_w_in, (0,))],
        batch=batch, seq=seq)
    x = _mixout(tok, p_side, 0, mem_kv(0), wm0, x, mix_post_g, (0,), seq=seq)
    x, (w10_in, w10_out, wm1) = _ffn(
        x, ffn_gains, (0, 1), (w01_in, w01_out), (), tf=FFN_TF_BF16,
        cast_jobs=ffn_pair_jobs((1, 0)) + [_grid_cast_job(w_mix_out, (1,), nt, nt)])

    x, (w11_in, w11_out) = _ffn(x, ffn_gains, (1, 0), (w10_in, w10_out), (), tf=FFN_TF_BF16,
                                cast_jobs=ffn_pair_jobs((1, 1)))
    tok, xq = _conv_mixer(x, mix_pre_g, (1,), wconv, conv_w, conv_b, (0,), seq=seq)
    x = _mixout(tok, xq, 0, mem_kv(1), wm1, x, mix_post_g, (1,), seq=seq)
    x, _ = _ffn(x, ffn_gains, (1, 1), (w11_in, w11_out), (), tf=FFN_TF_BF16)
    return x.reshape(batch, seq, d)
```

```python
import functools

import jax
import jax.numpy as jnp
from jax import lax
from jax.experimental import pallas as pl
from jax.experimental.pallas import tpu as pltpu

F32 = jnp.float32
BF16 = jnp.bfloat16

D_MODEL = 2048
DEPTH = 2
MEM_LEN = 256
TOK_W = 1536
X_HEADS = 4
X_HEAD_DIM = 128
X_W = 512
GLA_HEADS = 4
GLA_DV = 384
GLA_DK = 192
GLA_KW = 768
GLA_RANK = 16
GLA_TAU = 16.0
CONV_W = 1536
CONV_K = 3
D_FF = 5632
EPS = 1e-6
LOG2E = 1.4426950408889634

LANE = 128
SUBLANE = 8
MXU_DIM = 256
VMEM_LIMIT = 56 * 1024 * 1024
VMEM_LIMIT_FFN = 60 * 1024 * 1024
NORM_ROWS = 256
FFN_TF_F32 = 256
FFN_TF_BF16 = 512

GLA_RANKP = LANE
GLA_DKP = MXU_DIM
GLA_CHUNK = 64
GLA_STEP_CHUNKS = 2
GLA_MAIN_N = 2 * GLA_KW + 2 * TOK_W
GLA_SIDE_N = X_W + GLA_RANKP
CONV_NP = 3 * CONV_W + X_W


def _rmsnorm(x, g, eps=EPS):
    ms = jnp.mean(x * x, axis=-1, keepdims=True)
    return x * lax.rsqrt(ms + eps) * g


def _dot(a, b):
    return jnp.dot(a, b, preferred_element_type=F32)


def _dot_nt(a, b):
    return lax.dot_general(a, b, (((1,), (1,)), ((), ())), preferred_element_type=F32)


def _dot_tn(a, b):
    return lax.dot_general(a, b, (((0,), (0,)), ((), ())), preferred_element_type=F32)


def _stacked_spec(lead, block, index_map):
    nlead = len(lead)
    return pl.BlockSpec((None,) * nlead + block, lambda *g: tuple(lead) + tuple(index_map(*g)))


def _cast_specs(jobs):
    in_specs = [_stacked_spec(lead, blk, imap) for _, lead, blk, imap in jobs]
    out_specs = [pl.BlockSpec(blk, imap) for _, _, blk, imap in jobs]
    out_shape = [jax.ShapeDtypeStruct(w.shape[len(lead):], BF16) for w, lead, _, _ in jobs]
    return in_specs, out_specs, out_shape, [w for w, _, _, _ in jobs]


def _split_cast_refs(refs, n_in, n_out, n_cast):
    ins, refs = refs[:n_in], refs[n_in:]
    cast_in, refs = refs[:n_cast], refs[n_cast:]
    outs, refs = refs[:n_out], refs[n_out:]
    cast_out, scratch = refs[:n_cast], refs[n_cast:]
    return ins, outs, scratch, list(zip(cast_in, cast_out))


def _run_casts(casts):
    for src, dst in casts:
        dst[...] = src[...].astype(BF16)


def _ffn_kernel(*refs, n_cast):
    ins, (o_ref,), (h_ref,), casts = _split_cast_refs(refs, 6, 1, n_cast)
    x_ref, pre_g_ref, wg_ref, wu_ref, wo_ref, post_g_ref = ins
    j = pl.program_id(1)
    last = pl.num_programs(1) - 1
    tm = x_ref.shape[0]

    def weights():
        return tuple(w[...].astype(BF16) for w in (wg_ref, wu_ref, wo_ref))

    def step(rows, first, final, w):
        wg, wu, wo = w
        if first:
            h = _rmsnorm(x_ref[rows, :], pre_g_ref[...]).astype(BF16)
            h_ref[rows, :] = h
        else:
            h = h_ref[rows, :]
        gate = _dot(h, wg)
        act = (gate * jax.nn.sigmoid(gate) * _dot(h, wu)).astype(BF16)
        y = _dot(act, wo)
        acc = y if first else o_ref[rows, :] + y
        if final:
            acc = x_ref[rows, :] + _rmsnorm(acc, 0.5 * post_g_ref[...])
        o_ref[rows, :] = acc

    piece = NORM_ROWS if wg_ref.dtype == BF16 else 2 * NORM_ROWS
    pieces = [pl.ds(r, piece) for r in range(0, tm, piece)]

    @pl.when(j == 0)
    def _():
        _run_casts(casts)
        w = weights()
        for rows in pieces:
            step(rows, True, False, w)

    @pl.when((j > 0) & (j < last))
    def _():
        _run_casts(casts)
        step(slice(None), False, False, weights())

    @pl.when(j == last)
    def _():
        _run_casts(casts)
        w = weights()
        for rows in pieces:
            step(rows, False, True, w)


def _grid_cast_job(w, lead, n_i, n_j, rows_by_j=False):
    if rows_by_j:
        blk = (w.shape[-2] // n_j, w.shape[-1] // n_i)
        return (w, lead, blk, lambda i, j: (jnp.minimum(j, n_j - 1), i))
    blk = (w.shape[-2] // n_i, w.shape[-1] // n_j)
    return (w, lead, blk, lambda i, j: (i, jnp.minimum(j, n_j - 1)))


def _ffn(x, gains, g_lead, weights, w_lead, *, tf, tm=1024, cast_jobs=()):
    t, d = x.shape
    nt, nf = t // tm, D_FF // tf
    pre_g, post_g = gains
    w_in, w_out = weights
    c_in, c_out, c_shape, c_args = _cast_specs(cast_jobs)
    res = pl.pallas_call(
        functools.partial(_ffn_kernel, n_cast=len(cast_jobs)),
        grid=(nt, nf),
        in_specs=[
            pl.BlockSpec((tm, d), lambda i, j: (i, 0)),
            _stacked_spec(g_lead, (1, d), lambda i, j: (0, 0)),
            _stacked_spec(w_lead, (d, tf), lambda i, j: (0, j)),
            _stacked_spec(w_lead, (d, tf), lambda i, j: (0, j + nf)),
            _stacked_spec(w_lead, (tf, d), lambda i, j: (j, 0)),
            _stacked_spec(g_lead, (1, d), lambda i, j: (0, 0)),
        ] + c_in,
        out_specs=[pl.BlockSpec((tm, d), lambda i, j: (i, 0))] + c_out,
        out_shape=[jax.ShapeDtypeStruct((t, d), F32)] + c_shape,
        scratch_shapes=[pltpu.VMEM((tm, d), BF16)],
        compiler_params=pltpu.CompilerParams(
            dimension_semantics=("parallel", "arbitrary"), vmem_limit_bytes=VMEM_LIMIT_FFN),
        name="ffn",
    )(x, pre_g, w_in, w_in, w_out, post_g, *c_args)
    return res[0], tuple(res[1:])


def _norm_matmul_kernel(x_ref, g_ref, w_ref, o_ref, h_ref):
    @pl.when(pl.program_id(1) == 0)
    def _():
        h_ref[...] = _rmsnorm(x_ref[...], g_ref[...]).astype(BF16)

    o_ref[...] = _dot(h_ref[...], w_ref[...].astype(BF16)).astype(o_ref.dtype)


def _norm_matmul(x, g, g_lead, w, w_lead, *, tm, tn, name):
    t, d = x.shape
    n = w.shape[-1]
    return pl.pallas_call(
        _norm_matmul_kernel,
        grid=(t // tm, n // tn),
        in_specs=[
            pl.BlockSpec((tm, d), lambda i, j: (i, 0)),
            _stacked_spec(g_lead, (1, d), lambda i, j: (0, 0)),
            _stacked_spec(w_lead, (d, tn), lambda i, j: (0, j)),
        ],
        out_specs=pl.BlockSpec((tm, tn), lambda i, j: (i, j)),
        out_shape=jax.ShapeDtypeStruct((t, n), BF16),
        scratch_shapes=[pltpu.VMEM((tm, d), BF16)],
        compiler_params=pltpu.CompilerParams(
            dimension_semantics=("parallel", "arbitrary"), vmem_limit_bytes=VMEM_LIMIT),
        name=name,
    )(x, g, w)


def _gla_proj_kernel(x_ref, g_ref, wm_ref, ws_ref, om_ref, os_ref, h_ref):
    j = pl.program_id(1)

    @pl.when(j == 0)
    def _():
        ws = ws_ref[...].astype(BF16)
        for r in range(0, x_ref.shape[0], NORM_ROWS):
            rows = pl.ds(r, NORM_ROWS)
            h = _rmsnorm(x_ref[rows, :], g_ref[...]).astype(BF16)
            h_ref[rows, :] = h
            os_ref[rows, :] = _dot_nt(h, ws).astype(BF16)

    @pl.when(j > 0)
    def _():
        om_ref[...] = _dot_nt(h_ref[...], wm_ref[...].astype(BF16)).astype(BF16)


def _gla_proj(x, g, g_lead, wt, wt_side, *, tm=1024, tn=768):
    t, d = x.shape
    n_main = GLA_MAIN_N // tn
    blk = lambda j: jnp.maximum(j - 1, 0)
    w_blk = lambda j: jnp.where(j == 0, n_main - 1, j - 1)
    return pl.pallas_call(
        _gla_proj_kernel,
        grid=(t // tm, n_main + 1),
        in_specs=[
            pl.BlockSpec((tm, d), lambda i, j: (i, 0)),
            _stacked_spec(g_lead, (1, d), lambda i, j: (0, 0)),
            pl.BlockSpec((tn, d), lambda i, j: (w_blk(j), 0)),
            pl.BlockSpec((GLA_SIDE_N, d), lambda i, j: (0, 0)),
        ],
        out_specs=[
            pl.BlockSpec((tm, tn), lambda i, j: (i, blk(j))),
            pl.BlockSpec((tm, GLA_SIDE_N), lambda i, j: (i, 0)),
        ],
        out_shape=[jax.ShapeDtypeStruct((t, GLA_MAIN_N), BF16),
                   jax.ShapeDtypeStruct((t, GLA_SIDE_N), BF16)],
        scratch_shapes=[pltpu.VMEM((tm, d), BF16)],
        compiler_params=pltpu.CompilerParams(
            dimension_semantics=("parallel", "arbitrary"), vmem_limit_bytes=VMEM_LIMIT),
        name="gla_proj",
    )(x, g, wt, wt_side)


def _shift_groups(a, n):
    return jnp.concatenate([a[n:], a[:n]], axis=0)


def _block_first_group(a, n):
    return jnp.concatenate([a[g - g % n:g - g % n + 1] for g in range(a.shape[0])], axis=0)


def _gla_kernel(*refs, n_cast):
    ins, (o_ref,), (st_ref,), casts = _split_cast_refs(refs, 6, 1, n_cast)
    qkv_ref, r_ref, gl_ref, w2_ref, b2_ref, hg_ref = ins
    c_len = GLA_CHUNK
    ng = c_len // SUBLANE
    nb = qkv_ref.shape[0]
    nck = qkv_ref.shape[1] // c_len

    @pl.when(pl.program_id(0) == 0)
    def _():
        st_ref[...] = jnp.zeros_like(st_ref)

    r3 = lax.broadcasted_iota(jnp.int32, (ng, SUBLANE, GLA_DKP), 1)
    ti = lax.broadcasted_iota(jnp.int32, (c_len, c_len), 0)
    si = lax.broadcasted_iota(jnp.int32, (c_len, c_len), 1)
    txs = jnp.where(ti > si, ti ^ si, 0)
    to3 = lambda a: a.reshape(ng, SUBLANE, GLA_DKP)
    to2 = lambda a: a.reshape(c_len, GLA_DKP)

    heads = [divmod(s, GLA_HEADS) for s in range(nb * GLA_HEADS)]
    streams = [(bi, hd, ck) for ck in range(nck) for bi, hd in heads]
    crow = [pl.ds(ck * c_len, c_len) for ck in range(nck)]
    each = lambda f, *lists: [f(*args) for args in zip(*lists)]

    z = [_dot(gl_ref[bi, crow[ck], :], w2_ref[hd]) + b2_ref[hd] for bi, hd, ck in streams]
    _run_casts(casts)
    qkv = {(bi, ck): qkv_ref[bi, crow[ck], :2 * GLA_KW + GLA_DKP].astype(F32)
           for bi in range(nb) for ck in range(nck)}
    lane = lax.broadcasted_iota(jnp.int32, (c_len, GLA_DKP), 1)
    head = lambda a, lo: to3(jnp.where(lane < GLA_DK, a[:, lo:lo + GLA_DKP], 0.0))
    q = [head(qkv[bi, ck], hd * GLA_DK) for bi, hd, ck in streams]
    k = [head(qkv[bi, ck], GLA_KW + hd * GLA_DK) for bi, hd, ck in streams]

    def log2_decay(zs):
        soft = jnp.log2(1.0 + jnp.exp2(jnp.abs(zs) * -LOG2E))
        return (jnp.minimum(zs, 0.0) * LOG2E - soft) * (1.0 / GLA_TAU)

    tril = jnp.where(ti >= si, 1.0, 0.0).astype(BF16)

    def prefix(a):
        hi = a.astype(BF16)
        lo = (a - hi.astype(F32)).astype(BF16)
        return to3(_dot(tril, hi) + _dot(tril, lo))

    cum = each(prefix, each(log2_decay, z))

    nt = lambda a, b: _dot_nt(to2(a).astype(BF16), to2(b).astype(BF16))
    attn = each(lambda a, b: jnp.where(ti == si, nt(a, b), 0.0), q, k)
    g_h = cum
    g_group = each(lambda a: jnp.broadcast_to(a[:, 0:1, :], a.shape), cum)
    h = 1
    while h < c_len:
        if h < SUBLANE:
            if h > 1:
                g_h = each(lambda a: jnp.where((r3 & (h // 2)) != 0,
                                               pltpu.roll(a, h // 2, axis=1), a), g_h)
            g_next = each(lambda a: pltpu.roll(a, SUBLANE - h, axis=1), g_h)
        else:
            n = h // SUBLANE
            g_h = g_group
            if n > 1:
                g_h = each(lambda a: _block_first_group(a, n), g_h)
            g_next = each(lambda a: _shift_groups(a, n), g_h)
        q_h = q if h == 1 else each(lambda a, c, g: a * jnp.exp2(c - g), q, cum, g_h)
        k_h = each(lambda a, c, g: a * jnp.exp2(g - c), k, cum, g_next)
        a_h = each(nt, q_h, k_h)
        attn = each(lambda new, old: jnp.where((txs >= h) & (txs < 2 * h), new, old), a_h, attn)
        h *= 2

    last = each(lambda a: a[ng - 1, SUBLANE - 1:SUBLANE, :], cum)
    q_in = each(lambda a, c: to2(a * jnp.exp2(c)).astype(BF16), q, cum)
    k_out = each(lambda a, c, l: to2(a * jnp.exp2(l - c)).astype(BF16), k, cum, last)
    v = [qkv_ref[bi, crow[ck], pl.ds(2 * GLA_KW + hd * GLA_DV, GLA_DV)] for bi, hd, ck in streams]

    st = [st_ref[s] for s in range(len(heads))]
    o = []
    for ck in range(nck):
        of = slice(ck * len(heads), (ck + 1) * len(heads))
        o += each(lambda qi, s_, a, v_: _dot_nt(qi, s_.astype(BF16)) + _dot(a.astype(BF16), v_),
                  q_in[of], st, attn[of], v[of])
        st = each(lambda s_, l, v_, ko: s_ * jnp.exp2(l) + _dot_tn(v_, ko),
                  st, last[of], v[of], k_out[of])
    for s, s_ in enumerate(st):
        st_ref[s] = s_

    o = each(lambda a: _rmsnorm(a, hg_ref[...], EPS * GLA_DK), o)
    for (bi, hd, ck), a in zip(streams, o):
        c = pl.ds(hd * GLA_DV, GLA_DV)
        r = r_ref[bi, crow[ck], c].astype(F32)
        o_ref[bi, crow[ck], c] = (a * (r * jax.nn.sigmoid(r))).astype(BF16)


def _gla(p_main, p_side, w2, b2, head_g, cast_weights, *, batch, seq):
    t = p_main.shape[0]
    p_main = p_main.reshape(batch, seq, GLA_MAIN_N)
    p_side = p_side.reshape(batch, seq, GLA_SIDE_N)
    rows = GLA_CHUNK * GLA_STEP_CHUNKS
    steps = seq // rows
    assert 2 * GLA_KW == TOK_W
    c_split = 4
    r_split = steps // c_split
    by_step = lambda i: (i // c_split, i % c_split)
    jobs = [(w, lead, (w.shape[-2] // r_split, w.shape[-1] // c_split), by_step)
            for w, lead in cast_weights]
    c_in, c_out, c_shape, c_args = _cast_specs(jobs)
    res = pl.pallas_call(
        functools.partial(_gla_kernel, n_cast=len(jobs)),
        grid=(steps,),
        in_specs=[
            pl.BlockSpec((batch, rows, 2 * TOK_W), lambda i: (0, i, 0)),
            pl.BlockSpec((batch, rows, TOK_W), lambda i: (0, i, 2)),
            pl.BlockSpec((batch, rows, GLA_RANKP), lambda i: (0, i, X_W // GLA_RANKP)),
            pl.BlockSpec((GLA_HEADS, GLA_RANKP, GLA_DKP), lambda i: (0, 0, 0)),
            pl.BlockSpec((GLA_HEADS, 1, GLA_DKP), lambda i: (0, 0, 0)),
            pl.BlockSpec((1, GLA_DV), lambda i: (0, 0)),
        ] + c_in,
        out_specs=[pl.BlockSpec((batch, rows, TOK_W), lambda i: (0, i, 0))] + c_out,
        out_shape=[jax.ShapeDtypeStruct((batch, seq, TOK_W), BF16)] + c_shape,
        scratch_shapes=[pltpu.VMEM((batch * GLA_HEADS, GLA_DV, GLA_DKP), F32)],
        compiler_params=pltpu.CompilerParams(
            dimension_semantics=("arbitrary",), vmem_limit_bytes=VMEM_LIMIT),
        name="gla",
    )(p_main, p_main, p_side, w2, b2, head_g, *c_args)
    return res[0].reshape(t, TOK_W), tuple(res[1:])


def _conv_mixer_kernel(x_ref, g_ref, wb_ref, wc_ref, wx_ref, cw_ref, cb_ref, tok_ref, xq_ref,
                       h_ref, carry_ref, *, tiles_per_seq):
    i = pl.program_id(0)
    j = pl.program_id(1)
    c = jnp.maximum(j - 1, 0)
    tm = x_ref.shape[0]

    @pl.when(j == 0)
    def _():
        for r in range(0, tm, NORM_ROWS):
            rows = pl.ds(r, NORM_ROWS)
            h = _rmsnorm(x_ref[rows, :], g_ref[...]).astype(BF16)
            h_ref[rows, :] = h
            xq_ref[rows, :] = _dot(h, wb_ref[...]).astype(BF16)

    @pl.when((j > 0) & (i % tiles_per_seq == 0))
    def _():
        carry_ref[c] = jnp.zeros(carry_ref.shape[1:], F32)

    @pl.when(j > 0)
    def _():
        h = h_ref[...]
        u = _dot(h, wc_ref[...]) * _dot(h, wx_ref[...])
        u_prev = carry_ref[c]
        carry_ref[c] = u[tm - SUBLANE:, :]
        row = lax.broadcasted_iota(jnp.int32, u.shape, 0)
        u1 = jnp.where(row == 0, u_prev[SUBLANE - 1:SUBLANE, :], pltpu.roll(u, 1, axis=0))
        u2 = pltpu.roll(u, 2, axis=0)
        u2 = jnp.where(row == 0, u_prev[SUBLANE - 2:SUBLANE - 1, :], u2)
        u2 = jnp.where(row == 1, u_prev[SUBLANE - 1:SUBLANE, :], u2)
        w = cw_ref[...]
        y = u2 * w[0:1, :] + u1 * w[1:2, :] + u * w[2:3, :] + cb_ref[...]
        tok_ref[...] = (_dot(h, wb_ref[...]) * y).astype(BF16)


def _conv_mixer(x, g, g_lead, w, conv_w, conv_b, lead, *, seq, tm=1024, tw=512):
    t, d = x.shape
    assert tw == X_W
    n_conv = CONV_W // tw
    col = lambda j: jnp.maximum(j - 1, 0)
    return pl.pallas_call(
        functools.partial(_conv_mixer_kernel, tiles_per_seq=seq // tm),
        grid=(t // tm, n_conv + 1),
        in_specs=[
            pl.BlockSpec((tm, d), lambda i, j: (i, 0)),
            _stacked_spec(g_lead, (1, d), lambda i, j: (0, 0)),
            pl.BlockSpec((d, tw), lambda i, j: (0, jnp.where(j == 0, 3 * n_conv, j - 1))),
            pl.BlockSpec((d, tw), lambda i, j: (0, n_conv + col(j))),
            pl.BlockSpec((d, tw), lambda i, j: (0, 2 * n_conv + col(j))),
            _stacked_spec(lead, (CONV_K, tw), lambda i, j: (0, col(j))),
            _stacked_spec(lead, (1, tw), lambda i, j: (0, col(j))),
        ],
        out_specs=[
            pl.BlockSpec((tm, tw), lambda i, j: (i, col(j))),
            pl.BlockSpec((tm, tw), lambda i, j: (i, 0)),
        ],
        out_shape=[jax.ShapeDtypeStruct((t, CONV_W), BF16), jax.ShapeDtypeStruct((t, X_W), BF16)],
        scratch_shapes=[pltpu.VMEM((tm, d), BF16), pltpu.VMEM((n_conv, SUBLANE, tw), F32)],
        compiler_params=pltpu.CompilerParams(
            dimension_semantics=("arbitrary", "arbitrary"), vmem_limit_bytes=VMEM_LIMIT),
        name="conv_mixer",
    )(x, g, w, w, w, conv_w, conv_b)


def _mixout_kernel(tok_ref, xq_ref, kv_ref, wm_ref, x_ref, g_ref, o_ref):
    d = o_ref.shape[1]
    slab = d // X_HEADS
    tok = tok_ref[...]
    mixed, xo = [], []
    for h in range(X_HEADS):
        mixed.append(_dot(tok, wm_ref[:TOK_W, h * slab:(h + 1) * slab]))
        lo = h * X_HEAD_DIM
        qh = xq_ref[:, lo:lo + X_HEAD_DIM]
        kh = kv_ref[:, lo:lo + X_HEAD_DIM]
        vh = kv_ref[:, X_W + lo:X_W + lo + X_HEAD_DIM]
        s = _dot_nt(qh, kh) * (X_HEAD_DIM ** -0.5)
        e = jnp.exp(s - jnp.max(s, axis=-1, keepdims=True))
        pr = (e / jnp.sum(e, axis=-1, keepdims=True)).astype(BF16)
        xo.append(_dot(pr, vh).astype(BF16))
    mixed = jnp.concatenate(mixed, axis=1) + _dot(jnp.concatenate(xo, axis=1), wm_ref[TOK_W:, :])
    o_ref[...] = x_ref[...] + _rmsnorm(mixed, g_ref[...])


def _mixout(tok, p, xq_block, kv, wm, x, g, lead, *, seq, tm=512):
    t, d = x.shape
    per_seq = seq // tm
    return pl.pallas_call(
        _mixout_kernel,
        grid=(t // tm,),
        in_specs=[
            pl.BlockSpec((tm, TOK_W), lambda i: (i, 0)),
            pl.BlockSpec((tm, X_W), lambda i: (i, xq_block)),
            pl.BlockSpec((MEM_LEN, 2 * X_W), lambda i: (i // per_seq, 0)),
            pl.BlockSpec((d, d), lambda i: (0, 0)),
            pl.BlockSpec((tm, d), lambda i: (i, 0)),
            _stacked_spec(lead, (1, d), lambda i: (0, 0)),
        ],
        out_specs=pl.BlockSpec((tm, d), lambda i: (i, 0)),
        out_shape=jax.ShapeDtypeStruct((t, d), F32),
        compiler_params=pltpu.CompilerParams(
            dimension_semantics=("parallel",), vmem_limit_bytes=VMEM_LIMIT),
        name="mixout",
    )(tok, p, kv, wm, x, g)


def _gla_side_weight(wt):
    lo = GLA_MAIN_N
    pad = jnp.zeros((GLA_RANKP - GLA_RANK, wt.shape[1]), wt.dtype)
    return jnp.concatenate([wt[lo + GLA_RANK:], wt[lo:lo + GLA_RANK], pad], axis=0)


def _gla_gate_weights(gate_w2, gate_b):
    w2 = gate_w2.reshape(GLA_RANK, GLA_HEADS, GLA_DK).transpose(1, 0, 2)
    w2 = jnp.pad(w2, ((0, 0), (0, GLA_RANKP - GLA_RANK), (0, GLA_DKP - GLA_DK))).astype(BF16)
    b2 = jnp.pad(gate_b.reshape(GLA_HEADS, 1, GLA_DK), ((0, 0), (0, 0), (0, GLA_DKP - GLA_DK)))
    return w2, b2


def kernel(x, mem, ffn_pre_g, ffn_w_in, ffn_w_out, ffn_post_g, mix_pre_g, mix_post_g, mem_g,
           w_mem_kv, w_mix_out, gla_w_in, gla_gate_w2, gla_gate_b, gla_head_g,
           conv_w_in, conv_w, conv_b):
    batch, seq, d = x.shape
    x = x.reshape(batch * seq, d)
    mem2 = mem.reshape(batch * MEM_LEN, d)
    ffn_pre_g = ffn_pre_g[:, :, None, :]
    ffn_post_g = ffn_post_g[:, :, None, :]
    mix_pre_g = mix_pre_g[:, None, :]
    mix_post_g = mix_post_g[:, None, :]
    mem_g = mem_g[:, None, :]
    conv_b = conv_b[:, None, :]

    ffn_gains = (ffn_pre_g, ffn_post_g)
    ffn_stacks = (ffn_w_in, ffn_w_out)
    assert DEPTH == 2
    nt = batch * seq // 1024
    nf = D_FF // FFN_TF_BF16

    def mem_kv(i):
        return _norm_matmul(mem2, mem_g, (i,), w_mem_kv, (i,),
                            tm=batch * MEM_LEN, tn=2 * X_W, name="mem_kv")

    def ffn_pair_jobs(lead):
        return [_grid_cast_job(ffn_w_in, lead, nt, nf),
                _grid_cast_job(ffn_w_out, lead, nt, nf, rows_by_j=True)]


    x, _ = _ffn(x, ffn_gains, (0, 0), ffn_stacks, (0, 0), tf=FFN_TF_F32)
    w2, b2 = _gla_gate_weights(gla_gate_w2[0], gla_gate_b[0])
    wt = gla_w_in[0].T
    p, p_side = _gla_proj(x, mix_pre_g, (0,), wt, _gla_side_weight(wt))
    tok, (w01_in, w01_out, wm0, wconv) = _gla(
        p, p_side, w2, b2, gla_head_g[0][None],
        [(ffn_w_in, (0, 1)), (ffn_w_out, (0, 1)), (w_mix_out, (0,)), (conv_w_in, (0,))],
        batch=batch, seq=seq)
    x = _mixout(tok, p_side, 0, mem_kv(0), wm0, x, mix_post_g, (0,), seq=seq)
    x, (w10_in, w10_out, wm1) = _ffn(
        x, ffn_gains, (0, 1), (w01_in, w01_out), (), tf=FFN_TF_BF16,
        cast_jobs=ffn_pair_jobs((1, 0)) + [_grid_cast_job(w_mix_out, (1,), nt, nt)])

    x, (w11_in, w11_out) = _ffn(x, ffn_gains, (1, 0), (w10_in, w10_out), (), tf=FFN_TF_BF16,
                                cast_jobs=ffn_pair_jobs((1, 1)))
    tok, xq = _conv_mixer(x, mix_pre_g, (1,), wconv, conv_w, conv_b, (0,), seq=seq)
    x = _mixout(tok, xq, 0, mem_kv(1), wm1, x, mix_post_g, (1,), seq=seq)
    x, _ = _ffn(x, ffn_gains, (1, 1), (w11_in, w11_out), (), tf=FFN_TF_BF16)
    return x.reshape(batch, seq, d)
```

```python
import functools

import jax
import jax.numpy as jnp
from jax import lax
from jax.experimental import pallas as pl
from jax.experimental.pallas import tpu as pltpu

F32 = jnp.float32
BF16 = jnp.bfloat16

D_MODEL = 2048
DEPTH = 2
MEM_LEN = 256
TOK_W = 1536
X_HEADS = 4
X_HEAD_DIM = 128
X_W = 512
GLA_HEADS = 4
GLA_DV = 384
GLA_DK = 192
GLA_KW = 768
GLA_RANK = 16
GLA_TAU = 16.0
CONV_W = 1536
CONV_K = 3
D_FF = 5632
EPS = 1e-6
LOG2E = 1.4426950408889634

LANE = 128
SUBLANE = 8
MXU_DIM = 256
VMEM_LIMIT = 56 * 1024 * 1024
VMEM_LIMIT_FFN = 60 * 1024 * 1024
NORM_ROWS = 256
FFN_TF_F32 = 256
FFN_TF_BF16 = 512

GLA_RANKP = LANE
GLA_DKP = MXU_DIM
GLA_CHUNK = 64
GLA_STEP_CHUNKS = 2
GLA_MAIN_N = 2 * GLA_KW + 2 * TOK_W
GLA_SIDE_N = X_W + GLA_RANKP
CONV_NP = 3 * CONV_W + X_W


def _rmsnorm(x, g, eps=EPS):
    ms = jnp.mean(x * x, axis=-1, keepdims=True)
    return x * lax.rsqrt(ms + eps) * g


def _dot(a, b):
    return jnp.dot(a, b, preferred_element_type=F32)


def _dot_nt(a, b):
    return lax.dot_general(a, b, (((1,), (1,)), ((), ())), preferred_element_type=F32)


def _dot_tn(a, b):
    return lax.dot_general(a, b, (((0,), (0,)), ((), ())), preferred_element_type=F32)


def _stacked_spec(lead, block, index_map):
    nlead = len(lead)
    return pl.BlockSpec((None,) * nlead + block, lambda *g: tuple(lead) + tuple(index_map(*g)))


def _cast_specs(jobs):
    in_specs = [_stacked_spec(lead, blk, imap) for _, lead, blk, imap in jobs]
    out_specs = [pl.BlockSpec(blk, imap) for _, _, blk, imap in jobs]
    out_shape = [jax.ShapeDtypeStruct(w.shape[len(lead):], BF16) for w, lead, _, _ in jobs]
    return in_specs, out_specs, out_shape, [w for w, _, _, _ in jobs]


def _split_cast_refs(refs, n_in, n_out, n_cast):
    ins, refs = refs[:n_in], refs[n_in:]
    cast_in, refs = refs[:n_cast], refs[n_cast:]
    outs, refs = refs[:n_out], refs[n_out:]
    cast_out, scratch = refs[:n_cast], refs[n_cast:]
    return ins, outs, scratch, list(zip(cast_in, cast_out))


def _run_casts(casts):
    for src, dst in casts:
        dst[...] = src[...].astype(BF16)


def _ffn_kernel(*refs, n_cast):
    ins, (o_ref,), (h_ref,), casts = _split_cast_refs(refs, 6, 1, n_cast)
    x_ref, pre_g_ref, wg_ref, wu_ref, wo_ref, post_g_ref = ins
    j = pl.program_id(1)
    last = pl.num_programs(1) - 1
    tm = x_ref.shape[0]

    def weights():
        return tuple(w[...].astype(BF16) for w in (wg_ref, wu_ref, wo_ref))

    def step(rows, first, final, w):
        wg, wu, wo = w
        if first:
            h = _rmsnorm(x_ref[rows, :], pre_g_ref[...]).astype(BF16)
            h_ref[rows, :] = h
        else:
            h = h_ref[rows, :]
        gate = _dot(h, wg)
        act = (gate * jax.nn.sigmoid(gate) * _dot(h, wu)).astype(BF16)
        y = _dot(act, wo)
        acc = y if first else o_ref[rows, :] + y
        if final:
            acc = x_ref[rows, :] + _rmsnorm(acc, 0.5 * post_g_ref[...])
        o_ref[rows, :] = acc

    piece = 2 * NORM_ROWS
    pieces = [pl.ds(r, piece) for r in range(0, tm, piece)]

    @pl.when(j == 0)
    def _():
        _run_casts(casts)
        w = weights()
        for rows in pieces:
            step(rows, True, False, w)

    @pl.when((j > 0) & (j < last))
    def _():
        _run_casts(casts)
        step(slice(None), False, False, weights())

    @pl.when(j == last)
    def _():
        _run_casts(casts)
        w = weights()
        for rows in pieces:
            step(rows, False, True, w)


def _grid_cast_job(w, lead, n_i, n_j, rows_by_j=False):
    if rows_by_j:
        blk = (w.shape[-2] // n_j, w.shape[-1] // n_i)
        return (w, lead, blk, lambda i, j: (jnp.minimum(j, n_j - 1), i))
    blk = (w.shape[-2] // n_i, w.shape[-1] // n_j)
    return (w, lead, blk, lambda i, j: (i, jnp.minimum(j, n_j - 1)))


def _ffn(x, gains, g_lead, weights, w_lead, *, tf, tm=1024, cast_jobs=()):
    t, d = x.shape
    nt, nf = t // tm, D_FF // tf
    pre_g, post_g = gains
    w_in, w_out = weights
    c_in, c_out, c_shape, c_args = _cast_specs(cast_jobs)
    res = pl.pallas_call(
        functools.partial(_ffn_kernel, n_cast=len(cast_jobs)),
        grid=(nt, nf),
        in_specs=[
            pl.BlockSpec((tm, d), lambda i, j: (i, 0)),
            _stacked_spec(g_lead, (1, d), lambda i, j: (0, 0)),
            _stacked_spec(w_lead, (d, tf), lambda i, j: (0, j)),
            _stacked_spec(w_lead, (d, tf), lambda i, j: (0, j + nf)),
            _stacked_spec(w_lead, (tf, d), lambda i, j: (j, 0)),
            _stacked_spec(g_lead, (1, d), lambda i, j: (0, 0)),
        ] + c_in,
        out_specs=[pl.BlockSpec((tm, d), lambda i, j: (i, 0))] + c_out,
        out_shape=[jax.ShapeDtypeStruct((t, d), F32)] + c_shape,
        scratch_shapes=[pltpu.VMEM((tm, d), BF16)],
        compiler_params=pltpu.CompilerParams(
            dimension_semantics=("parallel", "arbitrary"), vmem_limit_bytes=VMEM_LIMIT_FFN),
        name="ffn",
    )(x, pre_g, w_in, w_in, w_out, post_g, *c_args)
    return res[0], tuple(res[1:])


def _norm_matmul_kernel(x_ref, g_ref, w_ref, o_ref, h_ref):
    @pl.when(pl.program_id(1) == 0)
    def _():
        h_ref[...] = _rmsnorm(x_ref[...], g_ref[...]).astype(BF16)

    o_ref[...] = _dot(h_ref[...], w_ref[...].astype(BF16)).astype(o_ref.dtype)


def _norm_matmul(x, g, g_lead, w, w_lead, *, tm, tn, name):
    t, d = x.shape
    n = w.shape[-1]
    return pl.pallas_call(
        _norm_matmul_kernel,
        grid=(t // tm, n // tn),
        in_specs=[
            pl.BlockSpec((tm, d), lambda i, j: (i, 0)),
            _stacked_spec(g_lead, (1, d), lambda i, j: (0, 0)),
            _stacked_spec(w_lead, (d, tn), lambda i, j: (0, j)),
        ],
        out_specs=pl.BlockSpec((tm, tn), lambda i, j: (i, j)),
        out_shape=jax.ShapeDtypeStruct((t, n), BF16),
        scratch_shapes=[pltpu.VMEM((tm, d), BF16)],
        compiler_params=pltpu.CompilerParams(
            dimension_semantics=("parallel", "arbitrary"), vmem_limit_bytes=VMEM_LIMIT),
        name=name,
    )(x, g, w)


def _gla_proj_kernel(x_ref, g_ref, wm_ref, ws_ref, om_ref, os_ref, h_ref):
    j = pl.program_id(1)

    @pl.when(j == 0)
    def _():
        ws = ws_ref[...].astype(BF16)
        for r in range(0, x_ref.shape[0], NORM_ROWS):
            rows = pl.ds(r, NORM_ROWS)
            h = _rmsnorm(x_ref[rows, :], g_ref[...]).astype(BF16)
            h_ref[rows, :] = h
            os_ref[rows, :] = _dot_nt(h, ws).astype(BF16)

    @pl.when(j > 0)
    def _():
        om_ref[...] = _dot_nt(h_ref[...], wm_ref[...].astype(BF16)).astype(BF16)


def _gla_proj(x, g, g_lead, wt, wt_side, *, tm=1024, tn=768):
    t, d = x.shape
    n_main = GLA_MAIN_N // tn
    blk = lambda j: jnp.maximum(j - 1, 0)
    w_blk = lambda j: jnp.where(j == 0, n_main - 1, j - 1)
    return pl.pallas_call(
        _gla_proj_kernel,
        grid=(t // tm, n_main + 1),
        in_specs=[
            pl.BlockSpec((tm, d), lambda i, j: (i, 0)),
            _stacked_spec(g_lead, (1, d), lambda i, j: (0, 0)),
            pl.BlockSpec((tn, d), lambda i, j: (w_blk(j), 0)),
            pl.BlockSpec((GLA_SIDE_N, d), lambda i, j: (0, 0)),
        ],
        out_specs=[
            pl.BlockSpec((tm, tn), lambda i, j: (i, blk(j))),
            pl.BlockSpec((tm, GLA_SIDE_N), lambda i, j: (i, 0)),
        ],
        out_shape=[jax.ShapeDtypeStruct((t, GLA_MAIN_N), BF16),
                   jax.ShapeDtypeStruct((t, GLA_SIDE_N), BF16)],
        scratch_shapes=[pltpu.VMEM((tm, d), BF16)],
        compiler_params=pltpu.CompilerParams(
            dimension_semantics=("parallel", "arbitrary"), vmem_limit_bytes=VMEM_LIMIT),
        name="gla_proj",
    )(x, g, wt, wt_side)


def _shift_groups(a, n):
    return jnp.concatenate([a[n:], a[:n]], axis=0)


def _block_first_group(a, n):
    return jnp.concatenate([a[g - g % n:g - g % n + 1] for g in range(a.shape[0])], axis=0)


def _gla_kernel(*refs, n_cast):
    ins, (o_ref,), (st_ref,), casts = _split_cast_refs(refs, 6, 1, n_cast)
    qkv_ref, r_ref, gl_ref, w2_ref, b2_ref, hg_ref = ins
    c_len = GLA_CHUNK
    ng = c_len // SUBLANE
    nb = qkv_ref.shape[0]
    nck = qkv_ref.shape[1] // c_len

    @pl.when(pl.program_id(0) == 0)
    def _():
        st_ref[...] = jnp.zeros_like(st_ref)

    r3 = lax.broadcasted_iota(jnp.int32, (ng, SUBLANE, GLA_DKP), 1)
    ti = lax.broadcasted_iota(jnp.int32, (c_len, c_len), 0)
    si = lax.broadcasted_iota(jnp.int32, (c_len, c_len), 1)
    txs = jnp.where(ti > si, ti ^ si, 0)
    to3 = lambda a: a.reshape(ng, SUBLANE, GLA_DKP)
    to2 = lambda a: a.reshape(c_len, GLA_DKP)

    heads = [divmod(s, GLA_HEADS) for s in range(nb * GLA_HEADS)]
    streams = [(bi, hd, ck) for ck in range(nck) for bi, hd in heads]
    crow = [pl.ds(ck * c_len, c_len) for ck in range(nck)]
    each = lambda f, *lists: [f(*args) for args in zip(*lists)]

    z = [_dot(gl_ref[bi, crow[ck], :], w2_ref[hd]) + b2_ref[hd] for bi, hd, ck in streams]
    _run_casts(casts)
    qkv = {(bi, ck): qkv_ref[bi, crow[ck], :2 * GLA_KW + GLA_DKP].astype(F32)
           for bi in range(nb) for ck in range(nck)}
    lane = lax.broadcasted_iota(jnp.int32, (c_len, GLA_DKP), 1)
    head = lambda a, lo: to3(jnp.where(lane < GLA_DK, a[:, lo:lo + GLA_DKP], 0.0))
    q = [head(qkv[bi, ck], hd * GLA_DK) for bi, hd, ck in streams]
    k = [head(qkv[bi, ck], GLA_KW + hd * GLA_DK) for bi, hd, ck in streams]

    def log2_decay(zs):
        soft = jnp.log2(1.0 + jnp.exp2(jnp.abs(zs) * -LOG2E))
        return (jnp.minimum(zs, 0.0) * LOG2E - soft) * (1.0 / GLA_TAU)

    tril = jnp.where(ti >= si, 1.0, 0.0).astype(BF16)

    def prefix(a):
        hi = a.astype(BF16)
        lo = (a - hi.astype(F32)).astype(BF16)
        return to3(_dot(tril, hi) + _dot(tril, lo))

    cum = each(prefix, each(log2_decay, z))

    nt = lambda a, b: _dot_nt(to2(a).astype(BF16), to2(b).astype(BF16))
    attn = each(lambda a, b: jnp.where(ti == si, nt(a, b), 0.0), q, k)
    g_h = cum
    g_group = each(lambda a: jnp.broadcast_to(a[:, 0:1, :], a.shape), cum)
    h = 1
    while h < c_len:
        if h < SUBLANE:
            if h > 1:
                g_h = each(lambda a: jnp.where((r3 & (h // 2)) != 0,
                                               pltpu.roll(a, h // 2, axis=1), a), g_h)
            g_next = each(lambda a: pltpu.roll(a, SUBLANE - h, axis=1), g_h)
        else:
            n = h // SUBLANE
            g_h = g_group
            if n > 1:
                g_h = each(lambda a: _block_first_group(a, n), g_h)
            g_next = each(lambda a: _shift_groups(a, n), g_h)
        q_h = q if h == 1 else each(lambda a, c, g: a * jnp.exp2(c - g), q, cum, g_h)
        k_h = each(lambda a, c, g: a * jnp.exp2(g - c), k, cum, g_next)
        a_h = each(nt, q_h, k_h)
        attn = each(lambda new, old: jnp.where((txs >= h) & (txs < 2 * h), new, old), a_h, attn)
        h *= 2

    last = each(lambda a: a[ng - 1, SUBLANE - 1:SUBLANE, :], cum)
    q_in = each(lambda a, c: to2(a * jnp.exp2(c)).astype(BF16), q, cum)
    k_out = each(lambda a, c, l: to2(a * jnp.exp2(l - c)).astype(BF16), k, cum, last)
    v = [qkv_ref[bi, crow[ck], pl.ds(2 * GLA_KW + hd * GLA_DV, GLA_DV)] for bi, hd, ck in streams]

    st = [st_ref[s] for s in range(len(heads))]
    o = []
    for ck in range(nck):
        of = slice(ck * len(heads), (ck + 1) * len(heads))
        o += each(lambda qi, s_, a, v_: _dot_nt(qi, s_.astype(BF16)) + _dot(a.astype(BF16), v_),
                  q_in[of], st, attn[of], v[of])
        st = each(lambda s_, l, v_, ko: s_ * jnp.exp2(l) + _dot_tn(v_, ko),
                  st, last[of], v[of], k_out[of])
    for s, s_ in enumerate(st):
        st_ref[s] = s_

    o = each(lambda a: _rmsnorm(a, hg_ref[...], EPS * GLA_DK), o)
    for (bi, hd, ck), a in zip(streams, o):
        c = pl.ds(hd * GLA_DV, GLA_DV)
        r = r_ref[bi, crow[ck], c].astype(F32)
        o_ref[bi, crow[ck], c] = (a * (r * jax.nn.sigmoid(r))).astype(BF16)


def _gla(p_main, p_side, w2, b2, head_g, cast_weights, *, batch, seq):
    t = p_main.shape[0]
    p_main = p_main.reshape(batch, seq, GLA_MAIN_N)
    p_side = p_side.reshape(batch, seq, GLA_SIDE_N)
    rows = GLA_CHUNK * GLA_STEP_CHUNKS
    steps = seq // rows
    assert 2 * GLA_KW == TOK_W
    c_split = 4
    r_split = steps // c_split
    by_step = lambda i: (i // c_split, i % c_split)
    jobs = [(w, lead, (w.shape[-2] // r_split, w.shape[-1] // c_split), by_step)
            for w, lead in cast_weights]
    c_in, c_out, c_shape, c_args = _cast_specs(jobs)
    res = pl.pallas_call(
        functools.partial(_gla_kernel, n_cast=len(jobs)),
        grid=(steps,),
        in_specs=[
            pl.BlockSpec((batch, rows, 2 * TOK_W), lambda i: (0, i, 0)),
            pl.BlockSpec((batch, rows, TOK_W), lambda i: (0, i, 2)),
            pl.BlockSpec((batch, rows, GLA_RANKP), lambda i: (0, i, X_W // GLA_RANKP)),
            pl.BlockSpec((GLA_HEADS, GLA_RANKP, GLA_DKP), lambda i: (0, 0, 0)),
            pl.BlockSpec((GLA_HEADS, 1, GLA_DKP), lambda i: (0, 0, 0)),
            pl.BlockSpec((1, GLA_DV), lambda i: (0, 0)),
        ] + c_in,
        out_specs=[pl.BlockSpec((batch, rows, TOK_W), lambda i: (0, i, 0))] + c_out,
        out_shape=[jax.ShapeDtypeStruct((batch, seq, TOK_W), BF16)] + c_shape,
        scratch_shapes=[pltpu.VMEM((batch * GLA_HEADS, GLA_DV, GLA_DKP), F32)],
        compiler_params=pltpu.CompilerParams(
            dimension_semantics=("arbitrary",), vmem_limit_bytes=VMEM_LIMIT),
        name="gla",
    )(p_main, p_main, p_side, w2, b2, head_g, *c_args)
    return res[0].reshape(t, TOK_W), tuple(res[1:])


def _conv_mixer_kernel(x_ref, g_ref, wb_ref, wc_ref, wx_ref, cw_ref, cb_ref, tok_ref, xq_ref,
                       h_ref, carry_ref, *, tiles_per_seq):
    i = pl.program_id(0)
    j = pl.program_id(1)
    c = jnp.maximum(j - 1, 0)
    tm = x_ref.shape[0]

    @pl.when(j == 0)
    def _():
        for r in range(0, tm, NORM_ROWS):
            rows = pl.ds(r, NORM_ROWS)
            h = _rmsnorm(x_ref[rows, :], g_ref[...]).astype(BF16)
            h_ref[rows, :] = h
            xq_ref[rows, :] = _dot(h, wb_ref[...]).astype(BF16)

    @pl.when((j > 0) & (i % tiles_per_seq == 0))
    def _():
        carry_ref[c] = jnp.zeros(carry_ref.shape[1:], F32)

    @pl.when(j > 0)
    def _():
        h = h_ref[...]
        u = _dot(h, wc_ref[...]) * _dot(h, wx_ref[...])
        u_prev = carry_ref[c]
        carry_ref[c] = u[tm - SUBLANE:, :]
        row = lax.broadcasted_iota(jnp.int32, u.shape, 0)
        u1 = jnp.where(row == 0, u_prev[SUBLANE - 1:SUBLANE, :], pltpu.roll(u, 1, axis=0))
        u2 = pltpu.roll(u, 2, axis=0)
        u2 = jnp.where(row == 0, u_prev[SUBLANE - 2:SUBLANE - 1, :], u2)
        u2 = jnp.where(row == 1, u_prev[SUBLANE - 1:SUBLANE, :], u2)
        w = cw_ref[...]
        y = u2 * w[0:1, :] + u1 * w[1:2, :] + u * w[2:3, :] + cb_ref[...]
        tok_ref[...] = (_dot(h, wb_ref[...]) * y).astype(BF16)


def _conv_mixer(x, g, g_lead, w, conv_w, conv_b, lead, *, seq, tm=1024, tw=512):
    t, d = x.shape
    assert tw == X_W
    n_conv = CONV_W // tw
    col = lambda j: jnp.maximum(j - 1, 0)
    return pl.pallas_call(
        functools.partial(_conv_mixer_kernel, tiles_per_seq=seq // tm),
        grid=(t // tm, n_conv + 1),
        in_specs=[
            pl.BlockSpec((tm, d), lambda i, j: (i, 0)),
            _stacked_spec(g_lead, (1, d), lambda i, j: (0, 0)),
            pl.BlockSpec((d, tw), lambda i, j: (0, jnp.where(j == 0, 3 * n_conv, j - 1))),
            pl.BlockSpec((d, tw), lambda i, j: (0, n_conv + col(j))),
            pl.BlockSpec((d, tw), lambda i, j: (0, 2 * n_conv + col(j))),
            _stacked_spec(lead, (CONV_K, tw), lambda i, j: (0, col(j))),
            _stacked_spec(lead, (1, tw), lambda i, j: (0, col(j))),
        ],
        out_specs=[
            pl.BlockSpec((tm, tw), lambda i, j: (i, col(j))),
            pl.BlockSpec((tm, tw), lambda i, j: (i, 0)),
        ],
        out_shape=[jax.ShapeDtypeStruct((t, CONV_W), BF16), jax.ShapeDtypeStruct((t, X_W), BF16)],
        scratch_shapes=[pltpu.VMEM((tm, d), BF16), pltpu.VMEM((n_conv, SUBLANE, tw), F32)],
        compiler_params=pltpu.CompilerParams(
            dimension_semantics=("arbitrary", "arbitrary"), vmem_limit_bytes=VMEM_LIMIT),
        name="conv_mixer",
    )(x, g, w, w, w, conv_w, conv_b)


def _mixout_kernel(tok_ref, xq_ref, kv_ref, wm_ref, x_ref, g_ref, o_ref):
    d = o_ref.shape[1]
    slab = d // X_HEADS
    tok = tok_ref[...]
    mixed, xo = [], []
    for h in range(X_HEADS):
        mixed.append(_dot(tok, wm_ref[:TOK_W, h * slab:(h + 1) * slab]))
        lo = h * X_HEAD_DIM
        qh = xq_ref[:, lo:lo + X_HEAD_DIM]
        kh = kv_ref[:, lo:lo + X_HEAD_DIM]
        vh = kv_ref[:, X_W + lo:X_W + lo + X_HEAD_DIM]
        s = _dot_nt(qh, kh) * (X_HEAD_DIM ** -0.5)
        e = jnp.exp(s - jnp.max(s, axis=-1, keepdims=True))
        pr = (e / jnp.sum(e, axis=-1, keepdims=True)).astype(BF16)
        xo.append(_dot(pr, vh).astype(BF16))
    mixed = jnp.concatenate(mixed, axis=1) + _dot(jnp.concatenate(xo, axis=1), wm_ref[TOK_W:, :])
    o_ref[...] = x_ref[...] + _rmsnorm(mixed, g_ref[...])


def _mixout(tok, p, xq_block, kv, wm, x, g, lead, *, seq, tm=512):
    t, d = x.shape
    per_seq = seq // tm
    return pl.pallas_call(
        _mixout_kernel,
        grid=(t // tm,),
        in_specs=[
            pl.BlockSpec((tm, TOK_W), lambda i: (i, 0)),
            pl.BlockSpec((tm, X_W), lambda i: (i, xq_block)),
            pl.BlockSpec((MEM_LEN, 2 * X_W), lambda i: (i // per_seq, 0)),
            pl.BlockSpec((d, d), lambda i: (0, 0)),
            pl.BlockSpec((tm, d), lambda i: (i, 0)),
            _stacked_spec(lead, (1, d), lambda i: (0, 0)),
        ],
        out_specs=pl.BlockSpec((tm, d), lambda i: (i, 0)),
        out_shape=jax.ShapeDtypeStruct((t, d), F32),
        compiler_params=pltpu.CompilerParams(
            dimension_semantics=("parallel",), vmem_limit_bytes=VMEM_LIMIT),
        name="mixout",
    )(tok, p, kv, wm, x, g)


def _gla_side_weight(wt):
    lo = GLA_MAIN_N
    pad = jnp.zeros((GLA_RANKP - GLA_RANK, wt.shape[1]), wt.dtype)
    return jnp.concatenate([wt[lo + GLA_RANK:], wt[lo:lo + GLA_RANK], pad], axis=0)


def _gla_gate_weights(gate_w2, gate_b):
    w2 = gate_w2.reshape(GLA_RANK, GLA_HEADS, GLA_DK).transpose(1, 0, 2)
    w2 = jnp.pad(w2, ((0, 0), (0, GLA_RANKP - GLA_RANK), (0, GLA_DKP - GLA_DK))).astype(BF16)
    b2 = jnp.pad(gate_b.reshape(GLA_HEADS, 1, GLA_DK), ((0, 0), (0, 0), (0, GLA_DKP - GLA_DK)))
    return w2, b2


def kernel(x, mem, ffn_pre_g, ffn_w_in, ffn_w_out, ffn_post_g, mix_pre_g, mix_post_g, mem_g,
           w_mem_kv, w_mix_out, gla_w_in, gla_gate_w2, gla_gate_b, gla_head_g,
           conv_w_in, conv_w, conv_b):
    batch, seq, d = x.shape
    x = x.reshape(batch * seq, d)
    mem2 = mem.reshape(batch * MEM_LEN, d)
    ffn_pre_g = ffn_pre_g[:, :, None, :]
    ffn_post_g = ffn_post_g[:, :, None, :]
    mix_pre_g = mix_pre_g[:, None, :]
    mix_post_g = mix_post_g[:, None, :]
    mem_g = mem_g[:, None, :]
    conv_b = conv_b[:, None, :]

    ffn_gains = (ffn_pre_g, ffn_post_g)
    ffn_stacks = (ffn_w_in, ffn_w_out)
    assert DEPTH == 2
    nt = batch * seq // 1024
    nf = D_FF // FFN_TF_BF16

    def mem_kv(i):
        return _norm_matmul(mem2, mem_g, (i,), w_mem_kv, (i,),
                            tm=batch * MEM_LEN, tn=2 * X_W, name="mem_kv")

    def ffn_pair_jobs(lead):
        return [_grid_cast_job(ffn_w_in, lead, nt, nf),
                _grid_cast_job(ffn_w_out, lead, nt, nf, rows_by_j=True)]


    x, _ = _ffn(x, ffn_gains, (0, 0), ffn_stacks, (0, 0), tf=FFN_TF_F32)
    w2, b2 = _gla_gate_weights(gla_gate_w2[0], gla_gate_b[0])
    wt = gla_w_in[0].T
    p, p_side = _gla_proj(x, mix_pre_g, (0,), wt, _gla_side_weight(wt))
    tok, (w01_in, w01_out, wm0, wconv) = _gla(
        p, p_side, w2, b2, gla_head_g[0][None],
        [(ffn_w_in, (0, 1)), (ffn_w_out, (0, 1)), (w_mix_out, (0,)), (conv_w_in, (0,))],
        batch=batch, seq=seq)
    x = _mixout(tok, p_side, 0, mem_kv(0), wm0, x, mix_post_g, (0,), seq=seq)
    x, (w10_in, w10_out, wm1) = _ffn(
        x, ffn_gains, (0, 1), (w01_in, w01_out), (), tf=FFN_TF_BF16,
        cast_jobs=ffn_pair_jobs((1, 0)) + [_grid_cast_job(w_mix_out, (1,), nt, nt)])

    x, (w11_in, w11_out) = _ffn(x, ffn_gains, (1, 0), (w10_in, w10_out), (), tf=FFN_TF_BF16,
                                cast_jobs=ffn_pair_jobs((1, 1)))
    tok, xq = _conv_mixer(x, mix_pre_g, (1,), wconv, conv_w, conv_b, (0,), seq=seq)
    x = _mixout(tok, xq, 0, mem_kv(1), wm1, x, mix_post_g, (1,), seq=seq)
    x, _ = _ffn(x, ffn_gains, (1, 1), (w11_in, w11_out), (), tf=FFN_TF_BF16)
    return x.reshape(batch, seq, d)
```
